```python
import math
import jax, jax.numpy as jnp
from jax import lax
import numpy as np

D_MODEL = 1024
BATCH = 8
SEQ = 2048
DEPTH = 2
DEC_BATCH = 128
DEC_SEQ = 8
PAST_LEN = 2048
PAGE_SIZE = 128

N_EVEN = (DEPTH + 1) // 2
N_ODD = DEPTH // 2
EPS = 1e-6

A_HEADS = 4
A_DK = 64
A_DV = 128
A_WIDTH = A_HEADS * A_DV
A_QK = A_HEADS * A_DK
A_GATE_RANK = 16
A_GATE_NORM = 16.0
A_CHUNK = 64
B_WIDTH = D_MODEL // 2
B_CONV = 31
EVEN_IN = 2 * A_QK + A_WIDTH + A_GATE_RANK + A_WIDTH + 2 * B_WIDTH
EVEN_OUT = A_WIDTH + B_WIDTH
C_PATTERNS = ((128, 1), (512, 4), (2048, 16))
C_GROUPS = len(C_PATTERNS)
C_HEADS = 4
C_DH = 64
C_QKV = 3 * C_GROUPS * C_HEADS * C_DH
C_WIDTH = C_HEADS * C_DH
C_QBLOCK = 128
D_GROUPS = 4
D_DH = 64
D_WIDTH = D_GROUPS * D_DH
D_CHUNK = 128
ODD_IN = C_QKV + 2 * D_WIDTH
ODD_OUT = C_WIDTH + D_WIDTH
N_BUCKETS = 32
MAX_DIST = 2048
D_FF = 2816
FFN_CONV = 3

kernel_name = "hybrid_gla_conformer_dilated_gmlp_step"


def rmsnorm(x, g):
    x32 = x.astype(jnp.float32)
    y = x32 * lax.rsqrt(jnp.mean(x32 * x32, axis=-1, keepdims=True) + EPS)
    return (y * g.astype(jnp.float32)).astype(x.dtype)


def layernorm(x, g, b):
    x32 = x.astype(jnp.float32)
    mu = jnp.mean(x32, axis=-1, keepdims=True)
    var = jnp.mean(jnp.square(x32 - mu), axis=-1, keepdims=True)
    y = (x32 - mu) * lax.rsqrt(var + EPS)
    return (y * g.astype(jnp.float32) + b.astype(jnp.float32)).astype(x.dtype)


def causal_dwconv(x, buf, w, b):
    xp = jnp.concatenate([buf.astype(x.dtype), x], axis=1)
    y = lax.conv_general_dilated(xp, w[:, None, :].astype(x.dtype), window_strides=(1,), padding='VALID',
                                 dimension_numbers=('NWC', 'WIO', 'NWC'), feature_group_count=x.shape[-1])
    return y + b.astype(x.dtype), xp[:, -(w.shape[0] - 1):]


def gla_scan(q, k, v, log_a, s0):
    Bn, L, H, _ = q.shape
    C = A_CHUNK if L % A_CHUNK == 0 else L
    n = L // C

    def to_chunks(t):
        return t.astype(jnp.float32).reshape(Bn, n, C, H, t.shape[-1]).transpose(1, 0, 3, 2, 4)

    qc, kc, vc, gc = to_chunks(q), to_chunks(k), to_chunks(v), to_chunks(log_a)
    mask = jnp.tril(jnp.ones((C, C), dtype=bool))

    def step(S, inp):
        qb, kb, vb, gb = inp
        b = jnp.cumsum(gb, axis=2)
        diff = b[:, :, :, None, :] - b[:, :, None, :, :]
        decay = jnp.exp(jnp.where(mask[None, None, :, :, None], diff, -jnp.inf))
        attn = jnp.einsum('bhtd,bhsd,bhtsd->bhts', qb, kb, decay)
        o = jnp.einsum('bhts,bhsv->bhtv', attn, vb) + jnp.einsum('bhtd,bhdv->bhtv', qb * jnp.exp(b), S)
        b_last = b[:, :, -1:, :]
        S_new = jnp.exp(b_last[:, :, 0, :])[..., None] * S + jnp.einsum('bhsd,bhsv->bhdv', kb * jnp.exp(b_last - b), vb)
        return S_new, o

    S, o = lax.scan(step, s0.astype(jnp.float32), (qc, kc, vc, gc))
    o = o.transpose(1, 0, 3, 2, 4).reshape(Bn, L, H, -1)
    return o, S.astype(s0.dtype)


def t5_bucket(dist):
    max_exact = N_BUCKETS // 2
    d32 = jnp.maximum(dist, 1).astype(jnp.float32)
    large = max_exact + (jnp.log(d32 / max_exact) / math.log(MAX_DIST / max_exact)
                         * (N_BUCKETS - max_exact)).astype(jnp.int32)
    large = jnp.minimum(large, N_BUCKETS - 1)
    return jnp.where(dist < max_exact, dist, large)


def dilated_attn(q, kv_full, offset, window, dilation, bias):
    Bn, Lq, H, dh = q.shape
    J = window // dilation + 1
    pos = offset + jnp.arange(Lq, dtype=jnp.int32)
    idx = pos[:, None] - dilation * jnp.arange(J, dtype=jnp.int32)[None, :]
    valid = idx >= 0
    idx = jnp.maximum(idx, 0)
    bias_hj = bias.T.astype(jnp.float32)

    def attend(args):
        qb, ib, vb = args
        kv_sel = kv_full[:, ib]
        s = jnp.einsum('bqhd,bqjhd->bqhj', qb, kv_sel[:, :, :, 0]).astype(jnp.float32) + bias_hj[None, None]
        s = jnp.where(vb[None, :, None, :], s, -jnp.inf)
        lse = jax.nn.logsumexp(s, axis=-1)
        p = jnp.exp(s - lse[..., None])
        o = jnp.einsum('bqhj,bqjhd->bqhd', p.astype(qb.dtype), kv_sel[:, :, :, 1])
        return o, lse

    qbs = C_QBLOCK if Lq % C_QBLOCK == 0 else Lq
    n = Lq // qbs
    if n == 1:
        return attend((q, idx, valid))
    qs = q.reshape(Bn, n, qbs, H, dh).transpose(1, 0, 2, 3, 4)
    o, lse = lax.map(attend, (qs, idx.reshape(n, qbs, J), valid.reshape(n, qbs, J)))
    o = o.transpose(1, 0, 2, 3, 4).reshape(Bn, Lq, H, dh)
    lse = lse.transpose(1, 0, 2, 3).reshape(Bn, Lq, H)
    return o, lse


def even_mixer(h, gla_s, conv_buf, w_in, w_gate2, b_gate, gla_norm, conv_w, conv_b, ln_g, ln_b, w_out):
    Bn, L, _ = h.shape
    z = h @ w_in
    splits = tuple(int(c) for c in np.cumsum([A_QK, A_QK, A_WIDTH, A_GATE_RANK, A_WIDTH]))
    q, k, v, glr, r, glu = jnp.split(z, splits, axis=-1)
    q = q.reshape(Bn, L, A_HEADS, A_DK) * (A_DK ** -0.5)
    k = k.reshape(Bn, L, A_HEADS, A_DK)
    v = v.reshape(Bn, L, A_HEADS, A_DV)
    log_a = (jax.nn.log_sigmoid((glr @ w_gate2 + b_gate).astype(jnp.float32)) / A_GATE_NORM)
    log_a = log_a.reshape(Bn, L, A_HEADS, A_DK)
    o, s_new = gla_scan(q, k, v, log_a, gla_s)
    oa = rmsnorm(o.astype(h.dtype), gla_norm).reshape(Bn, L, A_WIDTH) * jax.nn.silu(r)
    ga, gg = jnp.split(glu, 2, axis=-1)
    u = ga * jax.nn.sigmoid(gg)
    c, buf_new = causal_dwconv(u, conv_buf, conv_w, conv_b)
    c = jax.nn.silu(layernorm(c, ln_g, ln_b))
    y = jnp.concatenate([oa, c], axis=-1) @ w_out
    return y, s_new, buf_new


def odd_mixer(h, kv_bufs, w_in, rel_bias, sgu_ln_g, sgu_ln_b, sgu_w, sgu_b, w_out):
    Bn, L, _ = h.shape
    z = h @ w_in
    qkv, du, dv = jnp.split(z, (C_QKV, C_QKV + D_WIDTH), axis=-1)
    qkv = qkv.reshape(Bn, L, 3, C_GROUPS, C_HEADS, C_DH)
    q = qkv[:, :, 0] * (C_DH ** -0.5)
    kv = qkv[:, :, 1:]
    outs, lses, new_bufs = [], [], []
    for g, (window, dilation) in enumerate(C_PATTERNS):
        kv_g = kv[:, :, :, g]
        if kv_bufs is None:
            kv_full, offset = kv_g, 0
        else:
            kv_full = jnp.concatenate([kv_bufs[g].astype(kv_g.dtype), kv_g], axis=1)
            offset = kv_bufs[g].shape[1]
        J = window // dilation + 1
        bias = rel_bias[t5_bucket(dilation * jnp.arange(J, dtype=jnp.int32)), g * C_HEADS:(g + 1) * C_HEADS]
        o_g, lse_g = dilated_attn(q[:, :, g], kv_full, offset, window, dilation, bias)
        outs.append(o_g)
        lses.append(lse_g)
        new_bufs.append(kv_full[:, -min(window, kv_full.shape[1]):])
    wts = jax.nn.softmax(jnp.stack(lses, axis=0), axis=0)
    oc = jnp.sum(wts[..., None].astype(h.dtype) * jnp.stack(outs, axis=0), axis=0).reshape(Bn, L, C_WIDTH)
    u = jax.nn.gelu(du)
    vn = layernorm(jax.nn.gelu(dv), sgu_ln_g, sgu_ln_b)
    C = D_CHUNK if L >= D_CHUNK else L
    n = L // C
    vg = vn.reshape(Bn, n, C, D_GROUPS, D_DH)
    ws = sgu_w[:, :C, :C] * jnp.tril(jnp.ones((C, C), dtype=sgu_w.dtype))
    mixed = jnp.einsum('gts,bnsgd->bntgd', ws, vg) + sgu_b[:, :C].T[None, None, :, :, None]
    od = u * mixed.reshape(Bn, L, D_WIDTH)
    y = jnp.concatenate([oc, od], axis=-1) @ w_out
    return y, new_bufs, vn


def conv_ffn(h, buf, w_up, dw_w, dw_b, w_down):
    up = h @ w_up
    c, buf_new = causal_dwconv(up, buf, dw_w, dw_b)
    g, val = jnp.split(c, 2, axis=-1)
    return (jax.nn.gelu(g) * val) @ w_down, buf_new


def trunk(x, gla_s, convb_buf, kv_bufs, ffn_buf, p):
    new_gla, new_convb, new_v, new_ffn = [], [], [], []
    new_kv = [[] for _ in C_PATTERNS]
    for layer in range(DEPTH):
        h = rmsnorm(x, p['norm_pre_mix'][layer])
        if layer % 2 == 0:
            e = layer // 2
            y, s_new, cb_new = even_mixer(h, gla_s[e], convb_buf[e], p['w_in_even'][e], p['w_gate2'][e],
                                          p['b_gate'][e], p['gla_norm'][e], p['conv_b_w'][e], p['conv_b_b'][e],
                                          p['ln_b_g'][e], p['ln_b_b'][e], p['w_out_even'][e])
            new_gla.append(s_new)
            new_convb.append(cb_new)
        else:
            o = layer // 2
            bufs = None if kv_bufs is None else [b[o] for b in kv_bufs]
            y, kv_new, v_new = odd_mixer(h, bufs, p['w_in_odd'][o], p['rel_bias'], p['sgu_ln_g'][o],
                                         p['sgu_ln_b'][o], p['sgu_w'][o], p['sgu_b'][o], p['w_out_odd'][o])
            for g in range(C_GROUPS):
                new_kv[g].append(kv_new[g])
            new_v.append(v_new)
        x = x + rmsnorm(y, p['norm_post_mix'][layer])
        h = rmsnorm(x, p['norm_pre_ffn'][layer])
        f, fb = conv_ffn(h, ffn_buf[layer], p['w_up'][layer], p['ffn_dw_w'][layer], p['ffn_dw_b'][layer],
                         p['w_down'][layer])
        new_ffn.append(fb)
        x = x + rmsnorm(f, p['norm_post_ffn'][layer])
    return (x, jnp.stack(new_gla), jnp.stack(new_convb), [jnp.stack(l) for l in new_kv],
            jnp.stack(new_v), jnp.stack(new_ffn))


def setup_inputs(seed: int = 0) -> dict:
    key = jax.random.key(seed)
    ks = iter(jax.random.split(key, 48))

    def nrm(shape, scale):
        return scale * jax.random.normal(next(ks), shape, jnp.float32)

    def gain(shape):
        return 1.0 + 0.1 * jax.random.normal(next(ks), shape, jnp.float32)

    D = D_MODEL
    return {
        "x_prompt": nrm((BATCH, SEQ, D), 1.0),
        "x_sample": nrm((DEC_BATCH, DEC_SEQ, D), 1.0),
        "state_gla": nrm((N_EVEN, DEC_BATCH, A_HEADS, A_DK, A_DV), 0.1),
        "state_conv_b": nrm((N_EVEN, DEC_BATCH, B_CONV - 1, B_WIDTH), 0.5),
        "cache_c_w128": nrm((N_ODD, DEC_BATCH, min(C_PATTERNS[0][0], PAST_LEN), 2, C_HEADS, C_DH), 1.0),
        "cache_c_w512": nrm((N_ODD, DEC_BATCH, min(C_PATTERNS[1][0], PAST_LEN), 2, C_HEADS, C_DH), 1.0),
        "cache_c_w2048": nrm((N_ODD, DEC_BATCH, min(C_PATTERNS[2][0], PAST_LEN), 2, C_HEADS, C_DH), 1.0),
        "state_ffn_conv": nrm((DEPTH, DEC_BATCH, FFN_CONV - 1, 2 * D_FF), 1.0),
        "norm_pre_mix": gain((DEPTH, D)),
        "norm_post_mix": gain((DEPTH, D)),
        "norm_pre_ffn": gain((DEPTH, D)),
        "norm_post_ffn": gain((DEPTH, D)),
        "w_in_even": nrm((N_EVEN, D, EVEN_IN), D ** -0.5),
        "w_gate2": nrm((N_EVEN, A_GATE_RANK, A_QK), A_GATE_RANK ** -0.5),
        "b_gate": nrm((N_EVEN, A_QK), 0.1),
        "gla_norm": gain((N_EVEN, A_DV)),
        "conv_b_w": nrm((N_EVEN, B_CONV, B_WIDTH), B_CONV ** -0.5),
        "conv_b_b": nrm((N_EVEN, B_WIDTH), 0.02),
        "ln_b_g": gain((N_EVEN, B_WIDTH)),
        "ln_b_b": nrm((N_EVEN, B_WIDTH), 0.02),
        "w_out_even": nrm((N_EVEN, EVEN_OUT, D), EVEN_OUT ** -0.5),
        "w_in_odd": nrm((N_ODD, D, ODD_IN), D ** -0.5),
        "rel_bias": nrm((N_BUCKETS, C_GROUPS * C_HEADS), 0.5),
        "sgu_ln_g": gain((N_ODD, D_WIDTH)),
        "sgu_ln_b": nrm((N_ODD, D_WIDTH), 0.02),
        "sgu_w": nrm((N_ODD, D_GROUPS, D_CHUNK, D_CHUNK), D_CHUNK ** -0.5),
        "sgu_b": gain((N_ODD, D_GROUPS, D_CHUNK)),
        "w_out_odd": nrm((N_ODD, ODD_OUT, D), ODD_OUT ** -0.5),
        "w_up": nrm((DEPTH, D, 2 * D_FF), D ** -0.5),
        "ffn_dw_w": nrm((DEPTH, FFN_CONV, 2 * D_FF), FFN_CONV ** -0.5),
        "ffn_dw_b": nrm((DEPTH, 2 * D_FF), 0.02),
        "w_down": nrm((DEPTH, D_FF, D), D_FF ** -0.5),
    }


def reference(x_prompt, x_sample, state_gla, state_conv_b, cache_c_w128, cache_c_w512, cache_c_w2048,
              state_ffn_conv, norm_pre_mix, norm_post_mix, norm_pre_ffn, norm_post_ffn, w_in_even, w_gate2,
              b_gate, gla_norm, conv_b_w, conv_b_b, ln_b_g, ln_b_b, w_out_even, w_in_odd, rel_bias, sgu_ln_g,
              sgu_ln_b, sgu_w, sgu_b, w_out_odd, w_up, ffn_dw_w, ffn_dw_b, w_down):
    p = dict(norm_pre_mix=norm_pre_mix, norm_post_mix=norm_post_mix, norm_pre_ffn=norm_pre_ffn,
             norm_post_ffn=norm_post_ffn, w_in_even=w_in_even, w_gate2=w_gate2, b_gate=b_gate,
             gla_norm=gla_norm, conv_b_w=conv_b_w, conv_b_b=conv_b_b, ln_b_g=ln_b_g, ln_b_b=ln_b_b,
             w_out_even=w_out_even, w_in_odd=w_in_odd, rel_bias=rel_bias, sgu_ln_g=sgu_ln_g,
             sgu_ln_b=sgu_ln_b, sgu_w=sgu_w, sgu_b=sgu_b, w_out_odd=w_out_odd, w_up=w_up,
             ffn_dw_w=ffn_dw_w, ffn_dw_b=ffn_dw_b, w_down=w_down)
    dt = x_prompt.dtype
    gla0 = jnp.zeros((N_EVEN, BATCH, A_HEADS, A_DK, A_DV), dt)
    convb0 = jnp.zeros((N_EVEN, BATCH, B_CONV - 1, B_WIDTH), dt)
    ffn0 = jnp.zeros((DEPTH, BATCH, FFN_CONV - 1, 2 * D_FF), dt)
    y_prompt, p_gla, p_conv_b, p_kv, _, p_ffn_conv = trunk(x_prompt, gla0, convb0, None, ffn0, p)
    y_sample, s_gla, s_conv_b, s_kv, s_sgu_v, s_ffn_conv = trunk(
        x_sample, state_gla, state_conv_b, [cache_c_w128, cache_c_w512, cache_c_w2048], state_ffn_conv, p)
    return (y_prompt, y_sample, p_gla, p_conv_b, p_kv[0], p_kv[1], p_kv[2], p_ffn_conv,
            s_gla, s_conv_b, s_kv[0], s_kv[1], s_kv[2], s_sgu_v, s_ffn_conv)
```

```python
import functools
import math

import numpy as np
import jax
import jax.numpy as jnp
from jax import lax
from jax.experimental import pallas as pl
from jax.experimental.pallas import tpu as pltpu

F32 = jnp.float32
BF16 = jnp.bfloat16

D_MODEL = 1024
EPS = 1e-6
NEG = -1e30

A_HEADS = 4
A_DK = 64
A_DV = 128
A_QK = A_HEADS * A_DK
A_WIDTH = A_HEADS * A_DV
A_GATE_RANK = 16
A_GATE_NORM = 16.0
GLA_STEP = 16
B_WIDTH = 512
B_CONV = 31
CONV_HDR = 32
C_PATTERNS = ((128, 1), (512, 4), (2048, 16))
C_HEADS = 4
C_DH = 64
C_WIDTH = C_HEADS * C_DH
C_QBLOCK = 128
N_BUCKETS = 32
MAX_DIST = 2048
D_WIDTH = 256
D_DH = 64
D_CHUNK = 128
D_FF = 2816
FFN_CHUNK = 256
FFN_NCHUNK = D_FF // FFN_CHUNK
Z_WIDTH = 2816

VMEM_LIMIT_BYTES = 56 * 1024 * 1024

NT_DIMS = (((1,), (1,)), ((), ()))
TN_DIMS = (((0,), (0,)), ((), ()))


def _params(n_axes):
    return pltpu.CompilerParams(dimension_semantics=("arbitrary",) * n_axes,
                                vmem_limit_bytes=VMEM_LIMIT_BYTES)


def _rms(x, g):
    return x * lax.rsqrt(jnp.mean(x * x, axis=-1, keepdims=True) + EPS) * g


def _layernorm(x, g, b):
    mu = jnp.mean(x, axis=-1, keepdims=True)
    xc = x - mu
    var = jnp.mean(xc * xc, axis=-1, keepdims=True)
    return xc * lax.rsqrt(var + EPS) * g + b


def _sigmoid(x):
    return 1.0 / (1.0 + jnp.exp(-x))


def _head_mask(rows, lanes, rows_per_head, lanes_per_head):
    r = lax.broadcasted_iota(jnp.int32, (rows, lanes), 0) >> int(math.log2(rows_per_head))
    c = lax.broadcasted_iota(jnp.int32, (rows, lanes), 1) >> int(math.log2(lanes_per_head))
    return r == c


def _gelu(x):
    c = math.sqrt(2.0 / math.pi)
    return 0.5 * x * (1.0 + jnp.tanh(c * (x + 0.044715 * (x * x * x))))


def _inproj_body(x_ref, g_ref, w_ref, o_ref):
    h = _rms(x_ref[...], g_ref[...])
    o_ref[...] = jnp.dot(h.astype(BF16), w_ref[...], preferred_element_type=F32)


def _inproj(x2, g, w, tm):
    rows, d = x2.shape
    n = w.shape[1]
    return pl.pallas_call(
        _inproj_body,
        grid=(rows // tm,),
        in_specs=[pl.BlockSpec((tm, d), lambda i: (i, 0)),
                  pl.BlockSpec((1, d), lambda i: (0, 0)),
                  pl.BlockSpec((d, n), lambda i: (0, 0))],
        out_specs=pl.BlockSpec((tm, n), lambda i: (i, 0)),
        out_shape=jax.ShapeDtypeStruct((rows, n), F32),
        compiler_params=_params(1),
        name="inproj",
    )(x2, g, w)


def _gla_body(q_ref, k_ref, v_ref, glr_ref, wg_ref, bg_ref, segb_ref, s0_ref,
              o_ref, st_ref, b_sc, *, L, SB):
    xg = jnp.dot(glr_ref[0].astype(BF16), wg_ref[...], preferred_element_type=F32) + bg_ref[...]
    la = (jnp.minimum(xg, 0.0) - jnp.log(1.0 + jnp.exp(-jnp.abs(xg)))) * (1.0 / A_GATE_NORM)
    tl = lax.broadcasted_iota(jnp.int32, (L, A_QK), 0) & (SB - 1)
    sh = 1
    while sh < SB:
        la = la + jnp.where(tl >= sh, pltpu.roll(la, sh, axis=0), 0.0)
        sh *= 2
    b_sc[...] = la
    st_ref[0] = s0_ref[0]

    hm = _head_mask(A_HEADS * SB, A_QK, SB, A_DK)
    trow = lax.broadcasted_iota(jnp.int32, (SB, A_QK), 0)

    def step(i, carry):
        rows = pl.ds(pl.multiple_of(i * SB, SB), SB)
        q = q_ref[0, rows, :] * (A_DK ** -0.5)
        k = k_ref[0, rows, :]
        v = v_ref[0, rows, :]
        b = b_sc[rows, :]
        bl = b[SB - 1:SB, :]
        qt = q * jnp.exp(b)
        kh = k * jnp.exp(bl - b)
        st = st_ref[0]
        qbd = jnp.where(hm, jnp.concatenate([qt] * A_HEADS, axis=0), 0.0).astype(BF16)
        o_int = lax.dot_general(qbd, st.astype(BF16), NT_DIMS, preferred_element_type=F32)
        kbd = jnp.where(hm, jnp.concatenate([kh] * A_HEADS, axis=0), 0.0).astype(BF16)
        vst = jnp.concatenate([v[:, h * A_DV:(h + 1) * A_DV] for h in range(A_HEADS)],
                              axis=0).astype(BF16)
        upd = lax.dot_general(vst, kbd, TN_DIMS, preferred_element_type=F32)
        st_ref[0] = st * jnp.exp(bl) + upd
        ps = []
        for s in range(SB):
            e = jnp.exp(jnp.where(trow >= s, b - b[s:s + 1, :], NEG))
            ps.append(q * e * k[s:s + 1, :])
        pall = jnp.concatenate(ps, axis=0).astype(BF16)
        r = jnp.dot(pall, segb_ref[...], preferred_element_type=F32)
        od = r[0:SB] * v[0:1, :]
        for s in range(1, SB):
            od = od + r[s * SB:(s + 1) * SB] * v[s:s + 1, :]
        o = jnp.concatenate([o_int[h * SB:(h + 1) * SB] for h in range(A_HEADS)], axis=1) + od
        o_ref[0, rows, :] = o
        return carry

    lax.fori_loop(0, L // SB, step, 0)


def _gla(z3, wg2p, bgate, segb, s0t):
    nb, L, _ = z3.shape
    SB = GLA_STEP if L % GLA_STEP == 0 else L
    body = functools.partial(_gla_body, L=L, SB=SB)
    return pl.pallas_call(
        body,
        grid=(nb,),
        in_specs=[pl.BlockSpec((1, L, A_QK), lambda b: (b, 0, 0)),
                  pl.BlockSpec((1, L, A_QK), lambda b: (b, 0, 1)),
                  pl.BlockSpec((1, L, A_WIDTH), lambda b: (b, 0, 1)),
                  pl.BlockSpec((1, L, A_QK), lambda b: (b, 0, 10)),
                  pl.BlockSpec((A_QK, A_QK), lambda b: (0, 0)),
                  pl.BlockSpec((1, A_QK), lambda b: (0, 0)),
                  pl.BlockSpec((A_QK, A_WIDTH), lambda b: (0, 0)),
                  pl.BlockSpec((1, A_DV, A_QK), lambda b: (b, 0, 0))],
        out_specs=[pl.BlockSpec((1, L, A_WIDTH), lambda b: (b, 0, 0)),
                   pl.BlockSpec((1, A_DV, A_QK), lambda b: (b, 0, 0))],
        out_shape=[jax.ShapeDtypeStruct((nb, L, A_WIDTH), F32),
                   jax.ShapeDtypeStruct((nb, A_DV, A_QK), F32)],
        scratch_shapes=[pltpu.VMEM((L, A_QK), F32)],
        compiler_params=_params(1),
        name="gla",
    )(z3, z3, z3, z3, wg2p, bgate, segb, s0t)


def _conv_body(ga_ref, gg_ref, hdr_ref, cw_ref, cb_ref, lng_ref, lnb_ref,
               c_ref, ut_ref, xp, *, tm, tail):
    j = pl.program_id(1)

    @pl.when(j == 0)
    def _():
        xp[0:CONV_HDR, :] = hdr_ref[0]

    u = ga_ref[0] * _sigmoid(gg_ref[0])
    xp[CONV_HDR:CONV_HDR + tm, :] = u
    acc = jnp.zeros((tm, B_WIDTH), F32) + cb_ref[...]
    off = CONV_HDR - (B_CONV - 1)
    for kk in range(B_CONV):
        acc = acc + cw_ref[kk:kk + 1, :] * xp[off + kk:off + kk + tm, :]
    y = _layernorm(acc, lng_ref[...], lnb_ref[...])
    c_ref[0] = y * _sigmoid(y)
    ut_ref[0] = u[tm - tail:tm, :]
    nxt = xp[tm:tm + CONV_HDR, :]
    xp[0:CONV_HDR, :] = nxt


def _convmod(z3, hdr, cw, cb, lng, lnb, tm):
    nb, L, _ = z3.shape
    tail = min(CONV_HDR, tm)
    body = functools.partial(_conv_body, tm=tm, tail=tail)
    return pl.pallas_call(
        body,
        grid=(nb, L // tm),
        in_specs=[pl.BlockSpec((1, tm, B_WIDTH), lambda b, j: (b, j, 3)),
                  pl.BlockSpec((1, tm, B_WIDTH), lambda b, j: (b, j, 4)),
                  pl.BlockSpec((1, CONV_HDR, B_WIDTH), lambda b, j: (b, 0, 0)),
                  pl.BlockSpec((CONV_HDR, B_WIDTH), lambda b, j: (0, 0)),
                  pl.BlockSpec((1, B_WIDTH), lambda b, j: (0, 0)),
                  pl.BlockSpec((1, B_WIDTH), lambda b, j: (0, 0)),
                  pl.BlockSpec((1, B_WIDTH), lambda b, j: (0, 0))],
        out_specs=[pl.BlockSpec((1, tm, B_WIDTH), lambda b, j: (b, j, 0)),
                   pl.BlockSpec((1, tail, B_WIDTH), lambda b, j: (b, 0, 0))],
        out_shape=[jax.ShapeDtypeStruct((nb, L, B_WIDTH), F32),
                   jax.ShapeDtypeStruct((nb, tail, B_WIDTH), F32)],
        scratch_shapes=[pltpu.VMEM((CONV_HDR + tm, B_WIDTH), F32)],
        compiler_params=_params(2),
        name="convmod",
    )(z3, z3, hdr, cw, cb, lng, lnb)


def _evenout_body(o_ref, r_ref, c_ref, x_ref, gn_ref, w_ref, post_ref, xo_ref):
    o = o_ref[...]
    gn = gn_ref[...]
    oa = jnp.concatenate([_rms(o[:, h * A_DV:(h + 1) * A_DV], gn) for h in range(A_HEADS)], axis=1)
    r = r_ref[...]
    oa = oa * (r * _sigmoid(r))
    y = (jnp.dot(oa.astype(BF16), w_ref[0:A_WIDTH, :], preferred_element_type=F32)
         + jnp.dot(c_ref[...].astype(BF16), w_ref[A_WIDTH:A_WIDTH + B_WIDTH, :],
                   preferred_element_type=F32))
    xo_ref[...] = x_ref[...] + _rms(y, post_ref[...])


def _evenout(o2, z2, c2, x2, gn, w, post, tm):
    rows = x2.shape[0]
    return pl.pallas_call(
        _evenout_body,
        grid=(rows // tm,),
        in_specs=[pl.BlockSpec((tm, A_WIDTH), lambda i: (i, 0)),
                  pl.BlockSpec((tm, A_WIDTH), lambda i: (i, 2)),
                  pl.BlockSpec((tm, B_WIDTH), lambda i: (i, 0)),
                  pl.BlockSpec((tm, D_MODEL), lambda i: (i, 0)),
                  pl.BlockSpec((1, A_DV), lambda i: (0, 0)),
                  pl.BlockSpec((A_WIDTH + B_WIDTH, D_MODEL), lambda i: (0, 0)),
                  pl.BlockSpec((1, D_MODEL), lambda i: (0, 0))],
        out_specs=pl.BlockSpec((tm, D_MODEL), lambda i: (i, 0)),
        out_shape=jax.ShapeDtypeStruct((rows, D_MODEL), F32),
        compiler_params=_params(1),
        name="evenout",
    )(o2, z2, c2, x2, gn, w, post)


def _ffn_body(*refs, tm, seq8):
    if seq8:
        (x_ref, pg_ref, wg_ref, wv_ref, dwg_ref, dwv_ref, bg_ref, bv_ref, wd_ref, post_ref,
         p1g_ref, p1v_ref, p2g_ref, p2v_ref, xo_ref, tg_ref, tv_ref, h_sc, acc) = refs
    else:
        (x_ref, pg_ref, wg_ref, wv_ref, dwg_ref, dwv_ref, bg_ref, bv_ref, wd_ref, post_ref,
         xo_ref, tg_ref, tv_ref, h_sc, acc, car) = refs
    i = pl.program_id(1)
    c = pl.program_id(2)

    @pl.when(c == 0)
    def _():
        h_sc[...] = _rms(x_ref[...], pg_ref[...]).astype(BF16)
        acc[...] = jnp.zeros((tm, D_MODEL), F32)

    h = h_sc[...]
    ug = jnp.dot(h, wg_ref[...], preferred_element_type=F32)
    uv = jnp.dot(h, wv_ref[...], preferred_element_type=F32)
    row = lax.broadcasted_iota(jnp.int32, (tm, FFN_CHUNK), 0)
    if seq8:
        t = row & 7
        prev_g = (p1g_ref[...], p2g_ref[...])
        prev_v = (p1v_ref[...], p2v_ref[...])
    else:
        t = row

        @pl.when(i == 0)
        def _():
            car[c] = jnp.zeros((2, 8, FFN_CHUNK), F32)

        cg = car[c, 0]
        cv = car[c, 1]
        prev_g = (cg[7:8, :], jnp.where(row == 0, cg[6:7, :], cg[7:8, :]))
        prev_v = (cv[7:8, :], jnp.where(row == 0, cv[6:7, :], cv[7:8, :]))
        car[c, 0] = ug[tm - 8:tm, :]
        car[c, 1] = uv[tm - 8:tm, :]

    def conv(u, dw_ref, b_ref, prev):
        u1 = jnp.where(t >= 1, pltpu.roll(u, 1, axis=0), prev[0])
        u2 = jnp.where(t >= 2, pltpu.roll(u, 2, axis=0), prev[1])
        return dw_ref[0:1, :] * u2 + dw_ref[1:2, :] * u1 + dw_ref[2:3, :] * u + b_ref[...]

    g = conv(ug, dwg_ref, bg_ref, prev_g)
    val = conv(uv, dwv_ref, bv_ref, prev_v)
    act = (_gelu(g) * val).astype(BF16)
    acc[...] += jnp.dot(act, wd_ref[...], preferred_element_type=F32)
    if seq8:
        tg_ref[...] = ug
        tv_ref[...] = uv
    else:
        tg_ref[0] = ug[tm - 8:tm, :]
        tv_ref[0] = uv[tm - 8:tm, :]

    @pl.when(c == FFN_NCHUNK - 1)
    def _():
        xo_ref[...] = x_ref[...] + _rms(acc[...], post_ref[...])


def _ffn(x2, pg, wup, dw8, dwb, wdn, post, nb, tm, prev=None):
    rows = x2.shape[0]
    nt = rows // (nb * tm)
    seq8 = prev is not None
    nc = FFN_NCHUNK
    xmap = lambda b, i, c: (b * nt + i, 0)
    in_specs = [pl.BlockSpec((tm, D_MODEL), xmap),
                pl.BlockSpec((1, D_MODEL), lambda b, i, c: (0, 0)),
                pl.BlockSpec((D_MODEL, FFN_CHUNK), lambda b, i, c: (0, c)),
                pl.BlockSpec((D_MODEL, FFN_CHUNK), lambda b, i, c: (0, nc + c)),
                pl.BlockSpec((8, FFN_CHUNK), lambda b, i, c: (0, c)),
                pl.BlockSpec((8, FFN_CHUNK), lambda b, i, c: (0, nc + c)),
                pl.BlockSpec((1, FFN_CHUNK), lambda b, i, c: (0, c)),
                pl.BlockSpec((1, FFN_CHUNK), lambda b, i, c: (0, nc + c)),
                pl.BlockSpec((FFN_CHUNK, D_MODEL), lambda b, i, c: (c, 0)),
                pl.BlockSpec((1, D_MODEL), lambda b, i, c: (0, 0))]
    args = [x2, pg, wup, wup, dw8, dw8, dwb, dwb, wdn, post]
    scratch = [pltpu.VMEM((tm, D_MODEL), BF16), pltpu.VMEM((tm, D_MODEL), F32)]
    if seq8:
        p1, p2 = prev
        pm_g = lambda b, i, c: (b * nt + i, c)
        pm_v = lambda b, i, c: (b * nt + i, nc + c)
        in_specs += [pl.BlockSpec((tm, FFN_CHUNK), pm_g), pl.BlockSpec((tm, FFN_CHUNK), pm_v),
                     pl.BlockSpec((tm, FFN_CHUNK), pm_g), pl.BlockSpec((tm, FFN_CHUNK), pm_v)]
        args += [p1, p1, p2, p2]
        tail_specs = [pl.BlockSpec((tm, FFN_CHUNK), lambda b, i, c: (b * nt + i, c))] * 2
        tail_shape = [jax.ShapeDtypeStruct((rows, D_FF), F32)] * 2
    else:
        tail_specs = [pl.BlockSpec((1, 8, FFN_CHUNK), lambda b, i, c: (b * nt + i, 0, c))] * 2
        tail_shape = [jax.ShapeDtypeStruct((nb * nt, 8, D_FF), F32)] * 2
        scratch.append(pltpu.VMEM((nc, 2, 8, FFN_CHUNK), F32))
    body = functools.partial(_ffn_body, tm=tm, seq8=seq8)
    return pl.pallas_call(
        body,
        grid=(nb, nt, nc),
        in_specs=in_specs,
        out_specs=[pl.BlockSpec((tm, D_MODEL), xmap)] + tail_specs,
        out_shape=[jax.ShapeDtypeStruct((rows, D_MODEL), F32)] + tail_shape,
        scratch_shapes=scratch,
        compiler_params=_params(3),
        name="ffn_seq8" if seq8 else "ffn",
    )(*args)


def _attn_p_body(q0, k0, v0, q1, k1, v1, q2, k2, v2, bias_ref, o_ref,
                 qd, kd, vd, od, ld, og, lg, *, L):
    lane = lax.broadcasted_iota(jnp.int32, (C_QBLOCK, 2 * C_DH), 1)
    first = lane < C_DH
    col = lax.broadcasted_iota(jnp.int32, (C_QBLOCK, 2 * C_QBLOCK), 1)
    groups = ((q0, k0, v0), (q1, k1, v1), (q2, k2, v2))
    for g, (qr, kr, vr) in enumerate(groups):
        d = C_PATTERNS[g][1]
        nsub = L // d
        nblk = nsub // C_QBLOCK
        if d == 1:
            qs, ks, vs = qr.at[0], kr.at[0], vr.at[0]
            os_, ls_ = og.at[g], lg.at[g]
        else:
            for r in range(d):
                dst = slice(r * nsub, (r + 1) * nsub)
                qd[dst, :] = qr[0, pl.ds(r, nsub, stride=d), :]
                kd[dst, :] = kr[0, pl.ds(r, nsub, stride=d), :]
                vd[dst, :] = vr[0, pl.ds(r, nsub, stride=d), :]
            qs, ks, vs = qd, kd, vd
            os_, ls_ = od, ld

        def blk(idx, carry, qs=qs, ks=ks, vs=vs, os_=os_, ls_=ls_, nblk=nblk, g=g):
            rows = pl.ds(pl.multiple_of(idx * C_QBLOCK, C_QBLOCK), C_QBLOCK)
            qb = qs[rows, :] * (C_DH ** -0.5)
            kc = ks[rows, :]
            vc = vs[rows, :]
            if nblk > 1:
                m = idx & (nblk - 1)
                prow = pl.ds(pl.multiple_of(jnp.maximum(idx - 1, 0) * C_QBLOCK, C_QBLOCK), C_QBLOCK)
                kcat = jnp.concatenate([ks[prow, :], kc], axis=0).astype(BF16)
                vcat = jnp.concatenate([vs[prow, :], vc], axis=0).astype(BF16)
                dead = col < jnp.where(m == 0, C_QBLOCK, 0)
            else:
                kcat = kc.astype(BF16)
                vcat = vc.astype(BF16)
            outs, lses = [], []
            for hh in range(2):
                keep = first if hh == 0 else jnp.logical_not(first)
                qm = jnp.where(keep, qb, 0.0).astype(BF16)
                s = lax.dot_general(qm, kcat, NT_DIMS, preferred_element_type=F32)
                if nblk > 1:
                    s = jnp.where(dead, NEG, s + bias_ref[g, hh])
                else:
                    s = s + bias_ref[g, hh, :, C_QBLOCK:2 * C_QBLOCK]
                mx = jnp.max(s, axis=-1, keepdims=True)
                p = jnp.exp(s - mx)
                l = jnp.sum(p, axis=-1, keepdims=True)
                r = jnp.dot(p.astype(BF16), vcat, preferred_element_type=F32)
                outs.append(r / l)
                lses.append(mx + jnp.log(l))
            os_[rows, :] = jnp.where(first, outs[0], outs[1])
            ls_[rows, :] = jnp.where(first, lses[0], lses[1])
            return carry

        lax.fori_loop(0, d * nblk, blk, 0)
        if d != 1:
            for r in range(d):
                src = slice(r * nsub, (r + 1) * nsub)
                og[g, pl.ds(r, nsub, stride=d), :] = od[src, :]
                lg[g, pl.ds(r, nsub, stride=d), :] = ld[src, :]
    cr = 256
    for cidx in range(L // cr):
        rows = slice(cidx * cr, (cidx + 1) * cr)
        l0, l1, l2 = lg[0, rows, :], lg[1, rows, :], lg[2, rows, :]
        mx = jnp.maximum(jnp.maximum(l0, l1), l2)
        w0, w1, w2 = jnp.exp(l0 - mx), jnp.exp(l1 - mx), jnp.exp(l2 - mx)
        num = w0 * og[0, rows, :] + w1 * og[1, rows, :] + w2 * og[2, rows, :]
        o_ref[0, rows, :] = num / (w0 + w1 + w2)


def _attn_prompt(z3, biasmat):
    nb, L, _ = z3.shape
    lw = 2 * C_DH
    in_specs = []
    for g in range(3):
        for part in range(3):
            base = part * 6 + g * 2
            in_specs.append(pl.BlockSpec((1, L, lw), lambda b, p, base=base: (b, 0, base + p)))
    in_specs.append(pl.BlockSpec((3, 2, C_QBLOCK, 2 * C_QBLOCK), lambda b, p: (0, p, 0, 0)))
    body = functools.partial(_attn_p_body, L=L)
    return pl.pallas_call(
        body,
        grid=(nb, 2),
        in_specs=in_specs,
        out_specs=pl.BlockSpec((1, L, lw), lambda b, p: (b, 0, p)),
        out_shape=jax.ShapeDtypeStruct((nb, L, C_WIDTH), F32),
        scratch_shapes=[pltpu.VMEM((L, lw), F32)] * 5 + [pltpu.VMEM((3, L, lw), F32)] * 2,
        compiler_params=_params(2),
        name="attn_prompt",
    )(*([z3] * 9), biasmat)


def _attn_s_body(z_ref, c0, c1, c2, ma0, ma1, ma2, mb_ref, o_ref, n0, n1, n2):
    nq = 8
    z = z_ref[0]
    hm = _head_mask(C_HEADS * nq, C_WIDTH, nq, C_DH)
    outs, lses = [], []
    for g, (cref, mref, nref) in enumerate(((c0, ma0, n0), (c1, ma1, n1), (c2, ma2, n2))):
        W = C_PATTERNS[g][0]
        q = z[:, g * C_WIDTH:(g + 1) * C_WIDTH] * (C_DH ** -0.5)
        kn = z[:, (3 + g) * C_WIDTH:(4 + g) * C_WIDTH]
        vn = z[:, (6 + g) * C_WIDTH:(7 + g) * C_WIDTH]
        nref[0, 0:W - nq, :] = cref[0, nq:W, :]
        nref[0, W - nq:W, :] = jnp.concatenate([kn, vn], axis=1)
        qbd = jnp.where(hm, jnp.concatenate([q] * C_HEADS, axis=0), 0.0).astype(BF16)
        ka = nref[0, :, 0:C_WIDTH].astype(BF16)
        va = nref[0, :, C_WIDTH:2 * C_WIDTH].astype(BF16)
        kb = cref[0, 0:C_QBLOCK, 0:C_WIDTH].astype(BF16)
        vb = cref[0, 0:C_QBLOCK, C_WIDTH:2 * C_WIDTH].astype(BF16)
        sa = lax.dot_general(qbd, ka, NT_DIMS, preferred_element_type=F32) + mref[...]
        sb = lax.dot_general(qbd, kb, NT_DIMS, preferred_element_type=F32) + mb_ref[g]
        mx = jnp.maximum(jnp.max(sa, axis=-1, keepdims=True), jnp.max(sb, axis=-1, keepdims=True))
        pa = jnp.exp(sa - mx)
        pb = jnp.exp(sb - mx)
        l = jnp.sum(pa, axis=-1, keepdims=True) + jnp.sum(pb, axis=-1, keepdims=True)
        r = (jnp.dot(pa.astype(BF16), va, preferred_element_type=F32)
             + jnp.dot(pb.astype(BF16), vb, preferred_element_type=F32))
        outs.append(r / l)
        lses.append(mx + jnp.log(l))
    mx = jnp.maximum(jnp.maximum(lses[0], lses[1]), lses[2])
    ws = [jnp.exp(ls - mx) for ls in lses]
    o32 = (ws[0] * outs[0] + ws[1] * outs[1] + ws[2] * outs[2]) / (ws[0] + ws[1] + ws[2])
    o32 = jnp.where(hm, o32, 0.0)
    o_ref[0] = o32[0:nq] + o32[nq:2 * nq] + o32[2 * nq:3 * nq] + o32[3 * nq:4 * nq]


def _attn_sample(z3, caches, mas, mb):
    nb, nq, _ = z3.shape
    kvw = 2 * C_WIDTH
    in_specs = [pl.BlockSpec((1, nq, Z_WIDTH), lambda b: (b, 0, 0))]
    out_specs = [pl.BlockSpec((1, nq, C_WIDTH), lambda b: (b, 0, 0))]
    out_shape = [jax.ShapeDtypeStruct((nb, nq, C_WIDTH), F32)]
    for W, _ in C_PATTERNS:
        in_specs.append(pl.BlockSpec((1, W, kvw), lambda b: (b, 0, 0)))
        out_specs.append(pl.BlockSpec((1, W, kvw), lambda b: (b, 0, 0)))
        out_shape.append(jax.ShapeDtypeStruct((nb, W, kvw), F32))
    for W, _ in C_PATTERNS:
        in_specs.append(pl.BlockSpec((C_HEADS * nq, W), lambda b: (0, 0)))
    in_specs.append(pl.BlockSpec((3, C_HEADS * nq, C_QBLOCK), lambda b: (0, 0, 0)))
    return pl.pallas_call(
        _attn_s_body,
        grid=(nb,),
        in_specs=in_specs,
        out_specs=out_specs,
        out_shape=out_shape,
        compiler_params=_params(1),
        name="attn_sample",
    )(z3, *caches, *mas, mb)


def _oddout_p_body(oc_ref, du_ref, dv_ref, lng_ref, lnb_ref, ws_ref, sb_ref, x_ref, w_ref,
                   post_ref, xo_ref, *, tm):
    u = _gelu(du_ref[0])
    vn = _layernorm(_gelu(dv_ref[0]), lng_ref[...], lnb_ref[...])
    lane_g = lax.broadcasted_iota(jnp.int32, (D_CHUNK, D_WIDTH), 1) >> int(math.log2(D_DH))
    parts = []
    for cidx in range(tm // D_CHUNK):
        vc = vn[cidx * D_CHUNK:(cidx + 1) * D_CHUNK, :]
        m = sb_ref[...]
        for g in range(D_WIDTH // D_DH):
            vm = jnp.where(lane_g == g, vc, 0.0).astype(BF16)
            m = m + jnp.dot(ws_ref[g], vm, preferred_element_type=F32)
        parts.append(m)
    od = u * jnp.concatenate(parts, axis=0)
    y = (jnp.dot(oc_ref[0].astype(BF16), w_ref[0:C_WIDTH, :], preferred_element_type=F32)
         + jnp.dot(od.astype(BF16), w_ref[C_WIDTH:C_WIDTH + D_WIDTH, :],
                   preferred_element_type=F32))
    xo_ref[0] = x_ref[0] + _rms(y, post_ref[...])


def _oddout_prompt(oc3, z3, x3, lng, lnb, wsm, sbe, w, post, tm):
    nb, L, _ = x3.shape
    body = functools.partial(_oddout_p_body, tm=tm)
    return pl.pallas_call(
        body,
        grid=(nb, L // tm),
        in_specs=[pl.BlockSpec((1, tm, C_WIDTH), lambda b, j: (b, j, 0)),
                  pl.BlockSpec((1, tm, D_WIDTH), lambda b, j: (b, j, 9)),
                  pl.BlockSpec((1, tm, D_WIDTH), lambda b, j: (b, j, 10)),
                  pl.BlockSpec((1, D_WIDTH), lambda b, j: (0, 0)),
                  pl.BlockSpec((1, D_WIDTH), lambda b, j: (0, 0)),
                  pl.BlockSpec((4, D_CHUNK, D_CHUNK), lambda b, j: (0, 0, 0)),
                  pl.BlockSpec((D_CHUNK, D_WIDTH), lambda b, j: (0, 0)),
                  pl.BlockSpec((1, tm, D_MODEL), lambda b, j: (b, j, 0)),
                  pl.BlockSpec((C_WIDTH + D_WIDTH, D_MODEL), lambda b, j: (0, 0)),
                  pl.BlockSpec((1, D_MODEL), lambda b, j: (0, 0))],
        out_specs=pl.BlockSpec((1, tm, D_MODEL), lambda b, j: (b, j, 0)),
        out_shape=jax.ShapeDtypeStruct((nb, L, D_MODEL), F32),
        compiler_params=_params(2),
        name="oddout_prompt",
    )(oc3, z3, z3, lng, lnb, wsm, sbe, x3, w, post)


def _oddout_s_body(oc_ref, du_ref, dv_ref, lng_ref, lnb_ref, we_ref, sb_ref, x_ref, w_ref,
                   post_ref, xo_ref, vn_ref):
    nb, nq, _ = du_ref.shape
    u = _gelu(du_ref[...])
    vn = _layernorm(_gelu(dv_ref[...]), lng_ref[...], lnb_ref[...])
    vn_ref[...] = vn
    m = jnp.zeros((nb, nq, D_WIDTH), F32) + sb_ref[...]
    for s in range(nq):
        m = m + we_ref[s] * vn[:, s:s + 1, :]
    od = (u * m).reshape(nb * nq, D_WIDTH)
    y = (jnp.dot(oc_ref[...].astype(BF16), w_ref[0:C_WIDTH, :], preferred_element_type=F32)
         + jnp.dot(od.astype(BF16), w_ref[C_WIDTH:C_WIDTH + D_WIDTH, :],
                   preferred_element_type=F32))
    xo_ref[...] = x_ref[...] + _rms(y, post_ref[...])


def _oddout_sample(oc2, z3, x2, lng, lnb, wexp, sbe, w, post):
    nb, nq, _ = z3.shape
    rows = nb * nq
    return pl.pallas_call(
        _oddout_s_body,
        grid=(1,),
        in_specs=[pl.BlockSpec((rows, C_WIDTH), lambda i: (0, 0)),
                  pl.BlockSpec((nb, nq, D_WIDTH), lambda i: (0, 0, 9)),
                  pl.BlockSpec((nb, nq, D_WIDTH), lambda i: (0, 0, 10)),
                  pl.BlockSpec((1, D_WIDTH), lambda i: (0, 0)),
                  pl.BlockSpec((1, D_WIDTH), lambda i: (0, 0)),
                  pl.BlockSpec((nq, nq, D_WIDTH), lambda i: (0, 0, 0)),
                  pl.BlockSpec((nq, D_WIDTH), lambda i: (0, 0)),
                  pl.BlockSpec((rows, D_MODEL), lambda i: (0, 0)),
                  pl.BlockSpec((C_WIDTH + D_WIDTH, D_MODEL), lambda i: (0, 0)),
                  pl.BlockSpec((1, D_MODEL), lambda i: (0, 0))],
        out_specs=[pl.BlockSpec((rows, D_MODEL), lambda i: (0, 0)),
                   pl.BlockSpec((nb, nq, D_WIDTH), lambda i: (0, 0, 0))],
        out_shape=[jax.ShapeDtypeStruct((rows, D_MODEL), F32),
                   jax.ShapeDtypeStruct((nb, nq, D_WIDTH), F32)],
        compiler_params=_params(1),
        name="oddout_sample",
    )(oc2, z3, z3, lng, lnb, wexp, sbe, x2, w, post)


def _t5_bucket(dist):
    max_exact = N_BUCKETS // 2
    d32 = jnp.maximum(dist, 1).astype(F32)
    large = max_exact + (jnp.log(d32 / max_exact) / math.log(MAX_DIST / max_exact)
                         * (N_BUCKETS - max_exact)).astype(jnp.int32)
    large = jnp.minimum(large, N_BUCKETS - 1)
    return jnp.where(dist < max_exact, dist, large)


def _bias_table(rel_bias, dist, valid, g):
    bucket = _t5_bucket(jnp.asarray(np.maximum(dist, 0).astype(np.int32)))
    vals = rel_bias[bucket][..., g * C_HEADS:(g + 1) * C_HEADS]
    vals = jnp.moveaxis(vals, -1, 0)
    return jnp.where(jnp.asarray(valid)[None], vals, NEG).astype(F32)


def _prompt_bias(rel_bias):
    qi = np.arange(C_QBLOCK)[:, None]
    cc = np.arange(2 * C_QBLOCK)[None, :]
    steps = C_QBLOCK + qi - cc
    mats = []
    for g, (window, dil) in enumerate(C_PATTERNS):
        valid = (steps >= 0) & (steps <= window // dil)
        mats.append(_bias_table(rel_bias, steps * dil, valid, g))
    return jnp.stack(mats, axis=0)


def _sample_bias(rel_bias, nq):
    mas, mbs = [], []
    i = np.arange(nq)[:, None]
    for g, (window, dil) in enumerate(C_PATTERNS):
        a_new = np.arange(window)[None, :] + nq
        dist = window + i - a_new
        valid = (dist >= 0) & (dist <= window) & (dist % dil == 0)
        mas.append(_bias_table(rel_bias, dist, valid, g).reshape(C_HEADS * nq, window))
        a_old = np.arange(C_QBLOCK)[None, :]
        dist = window + i - a_old
        valid = (a_old < nq) & (dist <= window) & (dist % dil == 0)
        mbs.append(_bias_table(rel_bias, dist, valid, g).reshape(C_HEADS * nq, C_QBLOCK))
    return mas, jnp.stack(mbs, axis=0)


def _row(v):
    return v.reshape(1, -1).astype(F32)


def kernel(x_prompt, x_sample, state_gla, state_conv_b, cache_c_w128, cache_c_w512, cache_c_w2048,
           state_ffn_conv, norm_pre_mix, norm_post_mix, norm_pre_ffn, norm_post_ffn, w_in_even,
           w_gate2, b_gate, gla_norm, conv_b_w, conv_b_b, ln_b_g, ln_b_b, w_out_even, w_in_odd,
           rel_bias, sgu_ln_g, sgu_ln_b, sgu_w, sgu_b, w_out_odd, w_up, ffn_dw_w, ffn_dw_b, w_down):
    nbp, lp, d = x_prompt.shape
    nbs, ls, _ = x_sample.shape

    we = w_in_even[0]
    split = 2 * A_QK + A_WIDTH
    w_e = jnp.concatenate(
        [we[:, :split], we[:, split + A_GATE_RANK:],
         we[:, split:split + A_GATE_RANK],
         jnp.zeros((d, A_QK - A_GATE_RANK), F32)], axis=1).astype(BF16)
    wg2p = jnp.zeros((A_QK, A_QK), F32).at[:A_GATE_RANK].set(w_gate2[0]).astype(BF16)
    segb = jnp.asarray(np.kron(np.eye(A_HEADS), np.ones((A_DK, A_DV))), BF16)
    cw = jnp.zeros((CONV_HDR, B_WIDTH), F32).at[:B_CONV].set(conv_b_w[0])
    w_oe = w_out_even[0].astype(BF16)
    w_o = w_in_odd[0].astype(BF16)
    w_oo = w_out_odd[0].astype(BF16)
    w_up_b = w_up.astype(BF16)
    w_dn_b = w_down.astype(BF16)
    dw8 = jnp.zeros((2, 8, 2 * D_FF), F32).at[:, :3].set(ffn_dw_w)
    tril = jnp.tril(jnp.ones((D_CHUNK, D_CHUNK), F32))
    sgu_wm = sgu_w[0] * tril
    bias_p = _prompt_bias(rel_bias)
    mas, mbs = _sample_bias(rel_bias, ls)

    def ffn_layer(x2, layer, nb, tm, prev):
        return _ffn(x2, _row(norm_pre_ffn[layer]), w_up_b[layer], dw8[layer], _row(ffn_dw_b[layer]),
                    w_dn_b[layer], _row(norm_post_ffn[layer]), nb, tm, prev)

    def even_layer(x2, nb, L, s0t, hdr, conv_tm):
        z = _inproj(x2, _row(norm_pre_mix[0]), w_e, 512)
        z3 = z.reshape(nb, L, Z_WIDTH)
        o, st = _gla(z3, wg2p, _row(b_gate[0]), segb, s0t)
        c, ut = _convmod(z3, hdr, cw, _row(conv_b_b[0]), _row(ln_b_g[0]), _row(ln_b_b[0]), conv_tm)
        x1 = _evenout(o.reshape(nb * L, A_WIDTH), z, c.reshape(nb * L, B_WIDTH), x2,
                      _row(gla_norm[0]), w_oe, _row(norm_post_mix[0]), 512)
        s_new = st.reshape(nb, A_DV, A_HEADS, A_DK).transpose(0, 2, 3, 1)
        return x1, s_new, ut

    xp2 = x_prompt.reshape(nbp * lp, d)
    x1, p_gla, ut = even_layer(xp2, nbp, lp, jnp.zeros((nbp, A_DV, A_QK), F32),
                               jnp.zeros((nbp, CONV_HDR, B_WIDTH), F32), 512)
    p_conv_b = ut[:, CONV_HDR - (B_CONV - 1):]
    x2, tg0, tv0 = ffn_layer(x1, 0, nbp, 1024, None)
    z = _inproj(x2, _row(norm_pre_mix[1]), w_o, 512)
    z3 = z.reshape(nbp, lp, Z_WIDTH)
    oc = _attn_prompt(z3, bias_p)
    wsm = sgu_wm.astype(BF16)
    sbe = jnp.repeat(sgu_b[0].T, D_DH, axis=-1)
    x3 = _oddout_prompt(oc, z3, x2.reshape(nbp, lp, d), _row(sgu_ln_g[0]), _row(sgu_ln_b[0]),
                        wsm, sbe, w_oo, _row(norm_post_mix[1]), 512)
    x4, tg1, tv1 = ffn_layer(x3.reshape(nbp * lp, d), 1, nbp, 1024, None)
    y_prompt = x4.reshape(nbp, lp, d)
    p_kv = []
    for g, (window, _) in enumerate(C_PATTERNS):
        wl = min(window, lp)
        kk = z3[:, lp - wl:, (3 + g) * C_WIDTH:(4 + g) * C_WIDTH]
        vv = z3[:, lp - wl:, (6 + g) * C_WIDTH:(7 + g) * C_WIDTH]
        p_kv.append(jnp.concatenate([kk, vv], axis=-1).reshape(1, nbp, wl, 2, C_HEADS, C_DH))
    def last_two(t):
        return t.reshape(nbp, -1, 8, D_FF)[:, -1, 6:8]

    p_ffn = jnp.stack([jnp.concatenate([last_two(tg0), last_two(tv0)], axis=-1),
                       jnp.concatenate([last_two(tg1), last_two(tv1)], axis=-1)], axis=0)

    xs2 = x_sample.reshape(nbs * ls, d)
    s0t = state_gla[0].transpose(0, 3, 1, 2).reshape(nbs, A_DV, A_QK)
    hdr = jnp.pad(state_conv_b[0], ((0, 0), (CONV_HDR - (B_CONV - 1), 0), (0, 0)))
    y1, s_gla, us = even_layer(xs2, nbs, ls, s0t, hdr, ls)
    s_conv_b = jnp.concatenate([state_conv_b[0][:, ls:], us], axis=1)

    def prev_rows(st):
        p1 = jnp.zeros((nbs, ls, 2 * D_FF), F32).at[:, 0].set(st[:, 1])
        p2 = jnp.zeros((nbs, ls, 2 * D_FF), F32).at[:, 0].set(st[:, 0]).at[:, 1].set(st[:, 1])
        return p1.reshape(nbs * ls, 2 * D_FF), p2.reshape(nbs * ls, 2 * D_FF)

    def tail_rows(tg, tv):
        return jnp.concatenate([tg.reshape(nbs, ls, D_FF)[:, ls - 2:],
                                tv.reshape(nbs, ls, D_FF)[:, ls - 2:]], axis=-1)

    y2, sg0, sv0 = ffn_layer(y1, 0, 1, nbs * ls, prev_rows(state_ffn_conv[0]))
    zs = _inproj(y2, _row(norm_pre_mix[1]), w_o, 512)
    zs3 = zs.reshape(nbs, ls, Z_WIDTH)
    caches = [c[0].reshape(nbs, c.shape[2], 2 * C_WIDTH)
              for c in (cache_c_w128, cache_c_w512, cache_c_w2048)]
    ocs, n0, n1, n2 = _attn_sample(zs3, caches, mas, mbs)
    wexp = jnp.repeat(sgu_wm[:, :ls, :ls].transpose(2, 1, 0), D_DH, axis=-1)
    sbes = jnp.repeat(sgu_b[0][:, :ls].T, D_DH, axis=-1)
    y3, s_sgu_v = _oddout_sample(ocs.reshape(nbs * ls, C_WIDTH), zs3, y2, _row(sgu_ln_g[0]),
                                 _row(sgu_ln_b[0]), wexp, sbes, w_oo, _row(norm_post_mix[1]))
    y4, sg1, sv1 = ffn_layer(y3, 1, 1, nbs * ls, prev_rows(state_ffn_conv[1]))
    y_sample = y4.reshape(nbs, ls, d)
    s_kv = [n.reshape(1, nbs, n.shape[1], 2, C_HEADS, C_DH) for n in (n0, n1, n2)]
    s_ffn = jnp.stack([tail_rows(sg0, sv0), tail_rows(sg1, sv1)], axis=0)

    return (y_prompt, y_sample, p_gla[None], p_conv_b[None], p_kv[0], p_kv[1], p_kv[2], p_ffn,
            s_gla[None], s_conv_b[None], s_kv[0], s_kv[1], s_kv[2], s_sgu_v[None], s_ffn)
```

```python
import functools
import math

import numpy as np
import jax
import jax.numpy as jnp
from jax import lax
from jax.experimental import pallas as pl
from jax.experimental.pallas import tpu as pltpu

F32 = jnp.float32
BF16 = jnp.bfloat16

D_MODEL = 1024
EPS = 1e-6
NEG = -1e30

A_HEADS = 4
A_DK = 64
A_DV = 128
A_QK = A_HEADS * A_DK
A_WIDTH = A_HEADS * A_DV
A_GATE_RANK = 16
A_GATE_NORM = 16.0
GLA_STEP = 16
B_WIDTH = 512
B_CONV = 31
CONV_HDR = 32
C_PATTERNS = ((128, 1), (512, 4), (2048, 16))
C_HEADS = 4
C_DH = 64
C_WIDTH = C_HEADS * C_DH
C_QBLOCK = 128
N_BUCKETS = 32
MAX_DIST = 2048
D_WIDTH = 256
D_DH = 64
D_CHUNK = 128
D_FF = 2816
FFN_CHUNK = 256
FFN_NCHUNK = D_FF // FFN_CHUNK
Z_WIDTH = 2816

VMEM_LIMIT_BYTES = 56 * 1024 * 1024

NT_DIMS = (((1,), (1,)), ((), ()))
TN_DIMS = (((0,), (0,)), ((), ()))


def _params(n_axes):
    return pltpu.CompilerParams(dimension_semantics=("arbitrary",) * n_axes,
                                vmem_limit_bytes=VMEM_LIMIT_BYTES)


def _rms(x, g):
    return x * lax.rsqrt(jnp.mean(x * x, axis=-1, keepdims=True) + EPS) * g


def _layernorm(x, g, b):
    mu = jnp.mean(x, axis=-1, keepdims=True)
    xc = x - mu
    var = jnp.mean(xc * xc, axis=-1, keepdims=True)
    return xc * lax.rsqrt(var + EPS) * g + b


def _sigmoid(x):
    return 1.0 / (1.0 + jnp.exp(-x))


def _head_mask(rows, lanes, rows_per_head, lanes_per_head):
    r = lax.broadcasted_iota(jnp.int32, (rows, lanes), 0) >> int(math.log2(rows_per_head))
    c = lax.broadcasted_iota(jnp.int32, (rows, lanes), 1) >> int(math.log2(lanes_per_head))
    return r == c


def _gelu(x):
    c = math.sqrt(2.0 / math.pi)
    return 0.5 * x * (1.0 + jnp.tanh(c * (x + 0.044715 * (x * x * x))))


def _inproj_body(x_ref, g_ref, w_ref, o_ref):
    h = _rms(x_ref[...], g_ref[...])
    o_ref[...] = jnp.dot(h.astype(BF16), w_ref[...], preferred_element_type=F32)


def _inproj(x2, g, w, tm):
    rows, d = x2.shape
    n = w.shape[1]
    return pl.pallas_call(
        _inproj_body,
        grid=(rows // tm,),
        in_specs=[pl.BlockSpec((tm, d), lambda i: (i, 0)),
                  pl.BlockSpec((1, d), lambda i: (0, 0)),
                  pl.BlockSpec((d, n), lambda i: (0, 0))],
        out_specs=pl.BlockSpec((tm, n), lambda i: (i, 0)),
        out_shape=jax.ShapeDtypeStruct((rows, n), F32),
        compiler_params=_params(1),
        name="inproj",
    )(x2, g, w)


def _gla_body(q_ref, k_ref, v_ref, glr_ref, wg_ref, bg_ref, segb_ref, s0_ref,
              o_ref, st_ref, b_sc, *, L, SB):
    xg = jnp.dot(glr_ref[0].astype(BF16), wg_ref[...], preferred_element_type=F32) + bg_ref[...]
    la = (jnp.minimum(xg, 0.0) - jnp.log(1.0 + jnp.exp(-jnp.abs(xg)))) * (1.0 / A_GATE_NORM)
    tl = lax.broadcasted_iota(jnp.int32, (L, A_QK), 0) & (SB - 1)
    sh = 1
    while sh < SB:
        la = la + jnp.where(tl >= sh, pltpu.roll(la, sh, axis=0), 0.0)
        sh *= 2
    b_sc[...] = la
    st_ref[0] = s0_ref[0]

    hm = _head_mask(A_HEADS * SB, A_QK, SB, A_DK)
    trow = lax.broadcasted_iota(jnp.int32, (SB, A_QK), 0)

    def step(i, carry):
        rows = pl.ds(pl.multiple_of(i * SB, SB), SB)
        q = q_ref[0, rows, :] * (A_DK ** -0.5)
        k = k_ref[0, rows, :]
        v = v_ref[0, rows, :]
        b = b_sc[rows, :]
        bl = b[SB - 1:SB, :]
        qt = q * jnp.exp(b)
        kh = k * jnp.exp(bl - b)
        st = st_ref[0]
        qbd = jnp.where(hm, jnp.concatenate([qt] * A_HEADS, axis=0), 0.0).astype(BF16)
        o_int = lax.dot_general(qbd, st.astype(BF16), NT_DIMS, preferred_element_type=F32)
        kbd = jnp.where(hm, jnp.concatenate([kh] * A_HEADS, axis=0), 0.0).astype(BF16)
        vst = jnp.concatenate([v[:, h * A_DV:(h + 1) * A_DV] for h in range(A_HEADS)],
                              axis=0).astype(BF16)
        upd = lax.dot_general(vst, kbd, TN_DIMS, preferred_element_type=F32)
        st_ref[0] = st * jnp.exp(bl) + upd
        ps = []
        for s in range(SB):
            e = jnp.exp(jnp.where(trow >= s, b - b[s:s + 1, :], NEG))
            ps.append(q * e * k[s:s + 1, :])
        pall = jnp.concatenate(ps, axis=0).astype(BF16)
        r = jnp.dot(pall, segb_ref[...], preferred_element_type=F32)
        od = r[0:SB] * v[0:1, :]
        for s in range(1, SB):
            od = od + r[s * SB:(s + 1) * SB] * v[s:s + 1, :]
        o = jnp.concatenate([o_int[h * SB:(h + 1) * SB] for h in range(A_HEADS)], axis=1) + od
        o_ref[0, rows, :] = o
        return carry

    lax.fori_loop(0, L // SB, step, 0)


def _gla(z3, wg2p, bgate, segb, s0t):
    nb, L, _ = z3.shape
    SB = GLA_STEP if L % GLA_STEP == 0 else L
    body = functools.partial(_gla_body, L=L, SB=SB)
    return pl.pallas_call(
        body,
        grid=(nb,),
        in_specs=[pl.BlockSpec((1, L, A_QK), lambda b: (b, 0, 0)),
                  pl.BlockSpec((1, L, A_QK), lambda b: (b, 0, 1)),
                  pl.BlockSpec((1, L, A_WIDTH), lambda b: (b, 0, 1)),
                  pl.BlockSpec((1, L, A_QK), lambda b: (b, 0, 10)),
                  pl.BlockSpec((A_QK, A_QK), lambda b: (0, 0)),
                  pl.BlockSpec((1, A_QK), lambda b: (0, 0)),
                  pl.BlockSpec((A_QK, A_WIDTH), lambda b: (0, 0)),
                  pl.BlockSpec((1, A_DV, A_QK), lambda b: (b, 0, 0))],
        out_specs=[pl.BlockSpec((1, L, A_WIDTH), lambda b: (b, 0, 0)),
                   pl.BlockSpec((1, A_DV, A_QK), lambda b: (b, 0, 0))],
        out_shape=[jax.ShapeDtypeStruct((nb, L, A_WIDTH), F32),
                   jax.ShapeDtypeStruct((nb, A_DV, A_QK), F32)],
        scratch_shapes=[pltpu.VMEM((L, A_QK), F32)],
        compiler_params=_params(1),
        name="gla",
    )(z3, z3, z3, z3, wg2p, bgate, segb, s0t)


def _conv_body(ga_ref, gg_ref, hdr_ref, cw_ref, cb_ref, lng_ref, lnb_ref,
               c_ref, ut_ref, xp, *, tm, tail):
    j = pl.program_id(1)

    @pl.when(j == 0)
    def _():
        xp[0:CONV_HDR, :] = hdr_ref[0]

    u = ga_ref[0] * _sigmoid(gg_ref[0])
    xp[CONV_HDR:CONV_HDR + tm, :] = u
    acc = jnp.zeros((tm, B_WIDTH), F32) + cb_ref[...]
    off = CONV_HDR - (B_CONV - 1)
    for kk in range(B_CONV):
        acc = acc + cw_ref[kk:kk + 1, :] * xp[off + kk:off + kk + tm, :]
    y = _layernorm(acc, lng_ref[...], lnb_ref[...])
    c_ref[0] = y * _sigmoid(y)
    ut_ref[0] = u[tm - tail:tm, :]
    nxt = xp[tm:tm + CONV_HDR, :]
    xp[0:CONV_HDR, :] = nxt


def _convmod(z3, hdr, cw, cb, lng, lnb, tm):
    nb, L, _ = z3.shape
    tail = min(CONV_HDR, tm)
    body = functools.partial(_conv_body, tm=tm, tail=tail)
    return pl.pallas_call(
        body,
        grid=(nb, L // tm),
        in_specs=[pl.BlockSpec((1, tm, B_WIDTH), lambda b, j: (b, j, 3)),
                  pl.BlockSpec((1, tm, B_WIDTH), lambda b, j: (b, j, 4)),
                  pl.BlockSpec((1, CONV_HDR, B_WIDTH), lambda b, j: (b, 0, 0)),
                  pl.BlockSpec((CONV_HDR, B_WIDTH), lambda b, j: (0, 0)),
                  pl.BlockSpec((1, B_WIDTH), lambda b, j: (0, 0)),
                  pl.BlockSpec((1, B_WIDTH), lambda b, j: (0, 0)),
                  pl.BlockSpec((1, B_WIDTH), lambda b, j: (0, 0))],
        out_specs=[pl.BlockSpec((1, tm, B_WIDTH), lambda b, j: (b, j, 0)),
                   pl.BlockSpec((1, tail, B_WIDTH), lambda b, j: (b, 0, 0))],
        out_shape=[jax.ShapeDtypeStruct((nb, L, B_WIDTH), F32),
                   jax.ShapeDtypeStruct((nb, tail, B_WIDTH), F32)],
        scratch_shapes=[pltpu.VMEM((CONV_HDR + tm, B_WIDTH), F32)],
        compiler_params=_params(2),
        name="convmod",
    )(z3, z3, hdr, cw, cb, lng, lnb)


def _evenout_body(o_ref, r_ref, c_ref, x_ref, gn_ref, w_ref, post_ref, xo_ref):
    o = o_ref[...]
    gn = gn_ref[...]
    oa = jnp.concatenate([_rms(o[:, h * A_DV:(h + 1) * A_DV], gn) for h in range(A_HEADS)], axis=1)
    r = r_ref[...]
    oa = oa * (r * _sigmoid(r))
    y = (jnp.dot(oa.astype(BF16), w_ref[0:A_WIDTH, :], preferred_element_type=F32)
         + jnp.dot(c_ref[...].astype(BF16), w_ref[A_WIDTH:A_WIDTH + B_WIDTH, :],
                   preferred_element_type=F32))
    xo_ref[...] = x_ref[...] + _rms(y, post_ref[...])


def _evenout(o2, z2, c2, x2, gn, w, post, tm):
    rows = x2.shape[0]
    return pl.pallas_call(
        _evenout_body,
        grid=(rows // tm,),
        in_specs=[pl.BlockSpec((tm, A_WIDTH), lambda i: (i, 0)),
                  pl.BlockSpec((tm, A_WIDTH), lambda i: (i, 2)),
                  pl.BlockSpec((tm, B_WIDTH), lambda i: (i, 0)),
                  pl.BlockSpec((tm, D_MODEL), lambda i: (i, 0)),
                  pl.BlockSpec((1, A_DV), lambda i: (0, 0)),
                  pl.BlockSpec((A_WIDTH + B_WIDTH, D_MODEL), lambda i: (0, 0)),
                  pl.BlockSpec((1, D_MODEL), lambda i: (0, 0))],
        out_specs=pl.BlockSpec((tm, D_MODEL), lambda i: (i, 0)),
        out_shape=jax.ShapeDtypeStruct((rows, D_MODEL), F32),
        compiler_params=_params(1),
        name="evenout",
    )(o2, z2, c2, x2, gn, w, post)


def _dwconv3(u, t, prev1, prev2, dw, b):
    u1 = jnp.where(t >= 1, pltpu.roll(u, 1, axis=0), prev1)
    u2 = jnp.where(t >= 2, pltpu.roll(u, 2, axis=0), prev2)
    return dw[0:1, :] * u2 + dw[1:2, :] * u1 + dw[2:3, :] * u + b


def _ffn_s_body(x_ref, pg_ref, wg_ref, wv_ref, dwg_ref, dwv_ref, bg_ref, bv_ref, wd_ref, post_ref,
                sg_ref, sv_ref, xo_ref, tg_ref, tv_ref, h_sc, acc, *, tm):
    c = pl.program_id(0)

    @pl.when(c == 0)
    def _():
        h_sc[...] = _rms(x_ref[...], pg_ref[...]).astype(BF16)
        acc[...] = jnp.zeros((tm, D_MODEL), F32)

    h = h_sc[...]
    ug = jnp.dot(h, wg_ref[...], preferred_element_type=F32)
    uv = jnp.dot(h, wv_ref[...], preferred_element_type=F32)
    t = lax.broadcasted_iota(jnp.int32, (tm, FFN_CHUNK), 0) & 7

    def conv(u, s_ref, dw_ref, b_ref):
        s = s_ref[...]
        return _dwconv3(u, t, pltpu.roll(s, tm - 1, axis=0), s, dw_ref[...], b_ref[...])

    g = conv(ug, sg_ref, dwg_ref, bg_ref)
    val = conv(uv, sv_ref, dwv_ref, bv_ref)
    act = (_gelu(g) * val).astype(BF16)
    acc[...] += jnp.dot(act, wd_ref[...], preferred_element_type=F32)
    tg_ref[...] = ug
    tv_ref[...] = uv

    @pl.when(c == FFN_NCHUNK - 1)
    def _():
        xo_ref[...] = x_ref[...] + _rms(acc[...], post_ref[...])


def _ffn_sample(x2, pg, wup, dw8, dwb, wdn, post, st8):
    tm = x2.shape[0]
    nc = FFN_NCHUNK
    g_map = lambda c: (0, c)
    v_map = lambda c: (0, nc + c)
    fix = lambda c: (0, 0)
    body = functools.partial(_ffn_s_body, tm=tm)
    return pl.pallas_call(
        body,
        grid=(nc,),
        in_specs=[pl.BlockSpec((tm, D_MODEL), fix),
                  pl.BlockSpec((1, D_MODEL), fix),
                  pl.BlockSpec((D_MODEL, FFN_CHUNK), g_map),
                  pl.BlockSpec((D_MODEL, FFN_CHUNK), v_map),
                  pl.BlockSpec((8, FFN_CHUNK), g_map),
                  pl.BlockSpec((8, FFN_CHUNK), v_map),
                  pl.BlockSpec((1, FFN_CHUNK), g_map),
                  pl.BlockSpec((1, FFN_CHUNK), v_map),
                  pl.BlockSpec((FFN_CHUNK, D_MODEL), lambda c: (c, 0)),
                  pl.BlockSpec((1, D_MODEL), fix),
                  pl.BlockSpec((tm, FFN_CHUNK), g_map),
                  pl.BlockSpec((tm, FFN_CHUNK), v_map)],
        out_specs=[pl.BlockSpec((tm, D_MODEL), fix),
                   pl.BlockSpec((tm, FFN_CHUNK), g_map),
                   pl.BlockSpec((tm, FFN_CHUNK), g_map)],
        out_shape=[jax.ShapeDtypeStruct((tm, D_MODEL), F32),
                   jax.ShapeDtypeStruct((tm, D_FF), F32),
                   jax.ShapeDtypeStruct((tm, D_FF), F32)],
        scratch_shapes=[pltpu.VMEM((tm, D_MODEL), BF16), pltpu.VMEM((tm, D_MODEL), F32)],
        compiler_params=_params(1),
        name="ffn_sample",
    )(x2, pg, wup, wup, dw8, dw8, dwb, dwb, wdn, post, st8, st8)


def _ffn_p_body(x_ref, pg_ref, wup_ref, dw_ref, b_ref, wdn_ref, post_ref,
                xo_ref, tail_ref, car, act_sc, *, tm):
    i = pl.program_id(1)

    @pl.when(i == 0)
    def _():
        car[...] = jnp.zeros((8, 2 * D_FF), F32)

    x = x_ref[...]
    h = _rms(x, pg_ref[...]).astype(BF16)
    row = lax.broadcasted_iota(jnp.int32, (tm, FFN_CHUNK), 0)

    def half(cols):
        u = jnp.dot(h, wup_ref[:, cols], preferred_element_type=F32)
        cr = car[:, cols]
        prev2 = jnp.where(row == 0, cr[6:7, :], cr[7:8, :])
        out = _dwconv3(u, row, cr[7:8, :], prev2, dw_ref[:, cols], b_ref[:, cols])
        last = u[tm - 8:tm, :]
        car[:, cols] = last
        tail_ref[0, :, cols] = last
        return out

    for c in range(FFN_NCHUNK):
        g = half(slice(c * FFN_CHUNK, (c + 1) * FFN_CHUNK))
        val = half(slice(D_FF + c * FFN_CHUNK, D_FF + (c + 1) * FFN_CHUNK))
        act_sc[:, c * FFN_CHUNK:(c + 1) * FFN_CHUNK] = (_gelu(g) * val).astype(BF16)
    f = jnp.dot(act_sc[...], wdn_ref[...], preferred_element_type=F32)
    xo_ref[...] = x + _rms(f, post_ref[...])


def _ffn_prompt(x2, pg, wup, dw8, dwb, wdn, post, nb, tm):
    rows = x2.shape[0]
    nt = rows // (nb * tm)
    xmap = lambda b, i: (b * nt + i, 0)
    fix = lambda b, i: (0, 0)
    once = pl.Buffered(1)
    body = functools.partial(_ffn_p_body, tm=tm)
    return pl.pallas_call(
        body,
        grid=(nb, nt),
        in_specs=[pl.BlockSpec((tm, D_MODEL), xmap),
                  pl.BlockSpec((1, D_MODEL), fix),
                  pl.BlockSpec((D_MODEL, 2 * D_FF), fix, pipeline_mode=once),
                  pl.BlockSpec((8, 2 * D_FF), fix),
                  pl.BlockSpec((1, 2 * D_FF), fix),
                  pl.BlockSpec((D_FF, D_MODEL), fix, pipeline_mode=once),
                  pl.BlockSpec((1, D_MODEL), fix)],
        out_specs=[pl.BlockSpec((tm, D_MODEL), xmap),
                   pl.BlockSpec((1, 8, 2 * D_FF), lambda b, i: (b * nt + i, 0, 0))],
        out_shape=[jax.ShapeDtypeStruct((rows, D_MODEL), F32),
                   jax.ShapeDtypeStruct((nb * nt, 8, 2 * D_FF), F32)],
        scratch_shapes=[pltpu.VMEM((8, 2 * D_FF), F32), pltpu.VMEM((tm, D_FF), BF16)],
        compiler_params=_params(2),
        name="ffn_prompt",
    )(x2, pg, wup, dw8, dwb, wdn, post)


def _attn_p_body(q0, k0, v0, q1, k1, v1, q2, k2, v2, bias_ref, o_ref,
                 qd, kd, vd, od, ld, og, lg, *, L):
    lane = lax.broadcasted_iota(jnp.int32, (C_QBLOCK, 2 * C_DH), 1)
    first = lane < C_DH
    col = lax.broadcasted_iota(jnp.int32, (C_QBLOCK, 2 * C_QBLOCK), 1)
    groups = ((q0, k0, v0), (q1, k1, v1), (q2, k2, v2))
    for g, (qr, kr, vr) in enumerate(groups):
        d = C_PATTERNS[g][1]
        nsub = L // d
        nblk = nsub // C_QBLOCK
        if d == 1:
            qs, ks, vs = qr.at[0], kr.at[0], vr.at[0]
            os_, ls_ = og.at[g], lg.at[g]
        else:
            for r in range(d):
                dst = slice(r * nsub, (r + 1) * nsub)
                qd[dst, :] = qr[0, pl.ds(r, nsub, stride=d), :]
                kd[dst, :] = kr[0, pl.ds(r, nsub, stride=d), :]
                vd[dst, :] = vr[0, pl.ds(r, nsub, stride=d), :]
            qs, ks, vs = qd, kd, vd
            os_, ls_ = od, ld

        def blk(idx, carry, qs=qs, ks=ks, vs=vs, os_=os_, ls_=ls_, nblk=nblk, g=g):
            rows = pl.ds(pl.multiple_of(idx * C_QBLOCK, C_QBLOCK), C_QBLOCK)
            qb = qs[rows, :] * (C_DH ** -0.5)
            kc = ks[rows, :]
            vc = vs[rows, :]
            if nblk > 1:
                m = idx & (nblk - 1)
                prow = pl.ds(pl.multiple_of(jnp.maximum(idx - 1, 0) * C_QBLOCK, C_QBLOCK), C_QBLOCK)
                kcat = jnp.concatenate([ks[prow, :], kc], axis=0).astype(BF16)
                vcat = jnp.concatenate([vs[prow, :], vc], axis=0).astype(BF16)
                dead = col < jnp.where(m == 0, C_QBLOCK, 0)
            else:
                kcat = kc.astype(BF16)
                vcat = vc.astype(BF16)
            outs, lses = [], []
            for hh in range(2):
                keep = first if hh == 0 else jnp.logical_not(first)
                qm = jnp.where(keep, qb, 0.0).astype(BF16)
                s = lax.dot_general(qm, kcat, NT_DIMS, preferred_element_type=F32)
                if nblk > 1:
                    s = jnp.where(dead, NEG, s + bias_ref[g, hh])
                else:
                    s = s + bias_ref[g, hh, :, C_QBLOCK:2 * C_QBLOCK]
                mx = jnp.max(s, axis=-1, keepdims=True)
                p = jnp.exp(s - mx)
                l = jnp.sum(p, axis=-1, keepdims=True)
                r = jnp.dot(p.astype(BF16), vcat, preferred_element_type=F32)
                outs.append(r / l)
                lses.append(mx + jnp.log(l))
            os_[rows, :] = jnp.where(first, outs[0], outs[1])
            ls_[rows, :] = jnp.where(first, lses[0], lses[1])
            return carry

        lax.fori_loop(0, d * nblk, blk, 0)
        if d != 1:
            for r in range(d):
                src = slice(r * nsub, (r + 1) * nsub)
                og[g, pl.ds(r, nsub, stride=d), :] = od[src, :]
                lg[g, pl.ds(r, nsub, stride=d), :] = ld[src, :]
    cr = 256
    for cidx in range(L // cr):
        rows = slice(cidx * cr, (cidx + 1) * cr)
        l0, l1, l2 = lg[0, rows, :], lg[1, rows, :], lg[2, rows, :]
        mx = jnp.maximum(jnp.maximum(l0, l1), l2)
        w0, w1, w2 = jnp.exp(l0 - mx), jnp.exp(l1 - mx), jnp.exp(l2 - mx)
        num = w0 * og[0, rows, :] + w1 * og[1, rows, :] + w2 * og[2, rows, :]
        o_ref[0, rows, :] = num / (w0 + w1 + w2)


def _attn_prompt(z3, biasmat):
    nb, L, _ = z3.shape
    lw = 2 * C_DH
    in_specs = []
    for g in range(3):
        for part in range(3):
            base = part * 6 + g * 2
            in_specs.append(pl.BlockSpec((1, L, lw), lambda b, p, base=base: (b, 0, base + p)))
    in_specs.append(pl.BlockSpec((3, 2, C_QBLOCK, 2 * C_QBLOCK), lambda b, p: (0, p, 0, 0)))
    body = functools.partial(_attn_p_body, L=L)
    return pl.pallas_call(
        body,
        grid=(nb, 2),
        in_specs=in_specs,
        out_specs=pl.BlockSpec((1, L, lw), lambda b, p: (b, 0, p)),
        out_shape=jax.ShapeDtypeStruct((nb, L, C_WIDTH), F32),
        scratch_shapes=[pltpu.VMEM((L, lw), F32)] * 5 + [pltpu.VMEM((3, L, lw), F32)] * 2,
        compiler_params=_params(2),
        name="attn_prompt",
    )(*([z3] * 9), biasmat)


def _attn_s_body(z_ref, c0, c1, c2, ma0, ma1, ma2, mb_ref, o_ref, n0, n1, n2):
    nq = 8
    z = z_ref[0]
    hm = _head_mask(C_HEADS * nq, C_WIDTH, nq, C_DH)
    lane = lax.broadcasted_iota(jnp.int32, (2 * C_WIDTH, C_QBLOCK), 1)
    outs, lses = [], []
    for g, (cref, mref, nref) in enumerate(((c0, ma0, n0), (c1, ma1, n1), (c2, ma2, n2))):
        W = C_PATTERNS[g][0]
        q = z[:, g * C_WIDTH:(g + 1) * C_WIDTH] * (C_DH ** -0.5)
        kn = z[:, (3 + g) * C_WIDTH:(4 + g) * C_WIDTH]
        vn = z[:, (6 + g) * C_WIDTH:(7 + g) * C_WIDTH]
        new = jnp.concatenate([jnp.zeros((C_QBLOCK - nq, 2 * C_WIDTH), F32),
                               jnp.concatenate([kn, vn], axis=1)], axis=0)
        new_t = new.T
        sh = pltpu.roll(cref[0], W - nq, axis=1)
        if W > C_QBLOCK:
            nref[0, :, 0:W - C_QBLOCK] = sh[:, 0:W - C_QBLOCK]
        nref[0, :, W - C_QBLOCK:W] = jnp.where(lane >= C_QBLOCK - nq, new_t,
                                               sh[:, W - C_QBLOCK:W])
        qbd = jnp.where(hm, jnp.concatenate([q] * C_HEADS, axis=0), 0.0).astype(BF16)
        ka = nref[0, 0:C_WIDTH, :].astype(BF16)
        va = nref[0, C_WIDTH:2 * C_WIDTH, :].astype(BF16)
        kb = cref[0, 0:C_WIDTH, 0:C_QBLOCK].astype(BF16)
        vb = cref[0, C_WIDTH:2 * C_WIDTH, 0:C_QBLOCK].astype(BF16)
        sa = jnp.dot(qbd, ka, preferred_element_type=F32) + mref[...]
        sb = jnp.dot(qbd, kb, preferred_element_type=F32) + mb_ref[g]
        mx = jnp.maximum(jnp.max(sa, axis=-1, keepdims=True), jnp.max(sb, axis=-1, keepdims=True))
        pa = jnp.exp(sa - mx)
        pb = jnp.exp(sb - mx)
        l = jnp.sum(pa, axis=-1, keepdims=True) + jnp.sum(pb, axis=-1, keepdims=True)
        r = (lax.dot_general(pa.astype(BF16), va, NT_DIMS, preferred_element_type=F32)
             + lax.dot_general(pb.astype(BF16), vb, NT_DIMS, preferred_element_type=F32))
        outs.append(r / l)
        lses.append(mx + jnp.log(l))
    mx = jnp.maximum(jnp.maximum(lses[0], lses[1]), lses[2])
    ws = [jnp.exp(ls - mx) for ls in lses]
    o32 = (ws[0] * outs[0] + ws[1] * outs[1] + ws[2] * outs[2]) / (ws[0] + ws[1] + ws[2])
    o32 = jnp.where(hm, o32, 0.0)
    o_ref[0] = o32[0:nq] + o32[nq:2 * nq] + o32[2 * nq:3 * nq] + o32[3 * nq:4 * nq]


def _attn_sample(z3, caches, mas, mb):
    nb, nq, _ = z3.shape
    kvw = 2 * C_WIDTH
    in_specs = [pl.BlockSpec((1, nq, Z_WIDTH), lambda b: (b, 0, 0))]
    out_specs = [pl.BlockSpec((1, nq, C_WIDTH), lambda b: (b, 0, 0))]
    out_shape = [jax.ShapeDtypeStruct((nb, nq, C_WIDTH), F32)]
    for W, _ in C_PATTERNS:
        in_specs.append(pl.BlockSpec((1, kvw, W), lambda b: (b, 0, 0)))
        out_specs.append(pl.BlockSpec((1, kvw, W), lambda b: (b, 0, 0)))
        out_shape.append(jax.ShapeDtypeStruct((nb, kvw, W), F32))
    for W, _ in C_PATTERNS:
        in_specs.append(pl.BlockSpec((C_HEADS * nq, W), lambda b: (0, 0)))
    in_specs.append(pl.BlockSpec((3, C_HEADS * nq, C_QBLOCK), lambda b: (0, 0, 0)))
    return pl.pallas_call(
        _attn_s_body,
        grid=(nb,),
        in_specs=in_specs,
        out_specs=out_specs,
        out_shape=out_shape,
        compiler_params=_params(1),
        name="attn_sample",
    )(z3, *caches, *mas, mb)


def _oddout_p_body(oc_ref, du_ref, dv_ref, lng_ref, lnb_ref, ws_ref, sb_ref, x_ref, w_ref,
                   post_ref, xo_ref, *, tm):
    u = _gelu(du_ref[0])
    vn = _layernorm(_gelu(dv_ref[0]), lng_ref[...], lnb_ref[...])
    lane_g = lax.broadcasted_iota(jnp.int32, (D_CHUNK, D_WIDTH), 1) >> int(math.log2(D_DH))
    parts = []
    for cidx in range(tm // D_CHUNK):
        vc = vn[cidx * D_CHUNK:(cidx + 1) * D_CHUNK, :]
        m = sb_ref[...]
        for g in range(D_WIDTH // D_DH):
            vm = jnp.where(lane_g == g, vc, 0.0).astype(BF16)
            m = m + jnp.dot(ws_ref[g], vm, preferred_element_type=F32)
        parts.append(m)
    od = u * jnp.concatenate(parts, axis=0)
    y = (jnp.dot(oc_ref[0].astype(BF16), w_ref[0:C_WIDTH, :], preferred_element_type=F32)
         + jnp.dot(od.astype(BF16), w_ref[C_WIDTH:C_WIDTH + D_WIDTH, :],
                   preferred_element_type=F32))
    xo_ref[0] = x_ref[0] + _rms(y, post_ref[...])


def _oddout_prompt(oc3, z3, x3, lng, lnb, wsm, sbe, w, post, tm):
    nb, L, _ = x3.shape
    body = functools.partial(_oddout_p_body, tm=tm)
    return pl.pallas_call(
        body,
        grid=(nb, L // tm),
        in_specs=[pl.BlockSpec((1, tm, C_WIDTH), lambda b, j: (b, j, 0)),
                  pl.BlockSpec((1, tm, D_WIDTH), lambda b, j: (b, j, 9)),
                  pl.BlockSpec((1, tm, D_WIDTH), lambda b, j: (b, j, 10)),
                  pl.BlockSpec((1, D_WIDTH), lambda b, j: (0, 0)),
                  pl.BlockSpec((1, D_WIDTH), lambda b, j: (0, 0)),
                  pl.BlockSpec((4, D_CHUNK, D_CHUNK), lambda b, j: (0, 0, 0)),
                  pl.BlockSpec((D_CHUNK, D_WIDTH), lambda b, j: (0, 0)),
                  pl.BlockSpec((1, tm, D_MODEL), lambda b, j: (b, j, 0)),
                  pl.BlockSpec((C_WIDTH + D_WIDTH, D_MODEL), lambda b, j: (0, 0)),
                  pl.BlockSpec((1, D_MODEL), lambda b, j: (0, 0))],
        out_specs=pl.BlockSpec((1, tm, D_MODEL), lambda b, j: (b, j, 0)),
        out_shape=jax.ShapeDtypeStruct((nb, L, D_MODEL), F32),
        compiler_params=_params(2),
        name="oddout_prompt",
    )(oc3, z3, z3, lng, lnb, wsm, sbe, x3, w, post)


def _oddout_s_body(oc_ref, du_ref, dv_ref, lng_ref, lnb_ref, we_ref, sb_ref, x_ref, w_ref,
                   post_ref, xo_ref, vn_ref):
    nb, nq, _ = du_ref.shape
    u = _gelu(du_ref[...])
    vn = _layernorm(_gelu(dv_ref[...]), lng_ref[...], lnb_ref[...])
    vn_ref[...] = vn
    m = jnp.zeros((nb, nq, D_WIDTH), F32) + sb_ref[...]
    for s in range(nq):
        m = m + we_ref[s] * vn[:, s:s + 1, :]
    od = (u * m).reshape(nb * nq, D_WIDTH)
    y = (jnp.dot(oc_ref[...].astype(BF16), w_ref[0:C_WIDTH, :], preferred_element_type=F32)
         + jnp.dot(od.astype(BF16), w_ref[C_WIDTH:C_WIDTH + D_WIDTH, :],
                   preferred_element_type=F32))
    xo_ref[...] = x_ref[...] + _rms(y, post_ref[...])


def _oddout_sample(oc2, z3, x2, lng, lnb, wexp, sbe, w, post):
    nb, nq, _ = z3.shape
    rows = nb * nq
    return pl.pallas_call(
        _oddout_s_body,
        grid=(1,),
        in_specs=[pl.BlockSpec((rows, C_WIDTH), lambda i: (0, 0)),
                  pl.BlockSpec((nb, nq, D_WIDTH), lambda i: (0, 0, 9)),
                  pl.BlockSpec((nb, nq, D_WIDTH), lambda i: (0, 0, 10)),
                  pl.BlockSpec((1, D_WIDTH), lambda i: (0, 0)),
                  pl.BlockSpec((1, D_WIDTH), lambda i: (0, 0)),
                  pl.BlockSpec((nq, nq, D_WIDTH), lambda i: (0, 0, 0)),
                  pl.BlockSpec((nq, D_WIDTH), lambda i: (0, 0)),
                  pl.BlockSpec((rows, D_MODEL), lambda i: (0, 0)),
                  pl.BlockSpec((C_WIDTH + D_WIDTH, D_MODEL), lambda i: (0, 0)),
                  pl.BlockSpec((1, D_MODEL), lambda i: (0, 0))],
        out_specs=[pl.BlockSpec((rows, D_MODEL), lambda i: (0, 0)),
                   pl.BlockSpec((nb, nq, D_WIDTH), lambda i: (0, 0, 0))],
        out_shape=[jax.ShapeDtypeStruct((rows, D_MODEL), F32),
                   jax.ShapeDtypeStruct((nb, nq, D_WIDTH), F32)],
        compiler_params=_params(1),
        name="oddout_sample",
    )(oc2, z3, z3, lng, lnb, wexp, sbe, x2, w, post)


def _t5_bucket(dist):
    max_exact = N_BUCKETS // 2
    d32 = jnp.maximum(dist, 1).astype(F32)
    large = max_exact + (jnp.log(d32 / max_exact) / math.log(MAX_DIST / max_exact)
                         * (N_BUCKETS - max_exact)).astype(jnp.int32)
    large = jnp.minimum(large, N_BUCKETS - 1)
    return jnp.where(dist < max_exact, dist, large)


def _step_bias(rel_bias, g):
    window, dil = C_PATTERNS[g]
    j = jnp.arange(window // dil + 1, dtype=jnp.int32)
    return rel_bias[_t5_bucket(dil * j), g * C_HEADS:(g + 1) * C_HEADS].T.astype(F32)


def _skew(v, rows):
    n = v.shape[-1]
    lead = v.shape[:-1]
    t = jnp.broadcast_to(v[..., None, :], lead + (rows, n)).reshape(lead + (rows * n,))
    return t[..., :rows * (n - 1)].reshape(lead + (rows, n - 1))


def _prompt_bias(rel_bias):
    mats = []
    for g in range(len(C_PATTERNS)):
        bj = _step_bias(rel_bias, g)
        v = jnp.concatenate([bj[:, ::-1], jnp.full((C_HEADS, C_QBLOCK), NEG, F32)], axis=1)
        mats.append(_skew(v, C_QBLOCK))
    return jnp.stack(mats, axis=0)


def _sample_bias(rel_bias, nq):
    mas, mbs = [], []
    i = np.arange(nq)[:, None]
    a = np.arange(nq)[None, :]
    for g, (window, dil) in enumerate(C_PATTERNS):
        bj = _step_bias(rel_bias, g)
        on_grid = jnp.asarray(np.arange(window + 1) % dil == 0)
        bd = jnp.where(on_grid[None], jnp.repeat(bj, dil, axis=1)[:, :window + 1], NEG)
        e = jnp.concatenate([bd[:, :window][:, ::-1], jnp.full((C_HEADS, nq), NEG, F32)], axis=1)
        ma = _skew(e, nq)[:, :, nq - 1:nq - 1 + window]
        mas.append(ma.reshape(C_HEADS * nq, window))
        dist = window + i - a
        ok = jnp.asarray((a >= i) & (dist % dil == 0))
        mb = jnp.where(ok[None], bd[:, np.minimum(dist, window)], NEG)
        mb = jnp.pad(mb, ((0, 0), (0, 0), (0, C_QBLOCK - nq)), constant_values=NEG)
        mbs.append(mb.reshape(C_HEADS * nq, C_QBLOCK))
    return mas, jnp.stack(mbs, axis=0)


def _row(v):
    return v.reshape(1, -1).astype(F32)


def kernel(x_prompt, x_sample, state_gla, state_conv_b, cache_c_w128, cache_c_w512, cache_c_w2048,
           state_ffn_conv, norm_pre_mix, norm_post_mix, norm_pre_ffn, norm_post_ffn, w_in_even,
           w_gate2, b_gate, gla_norm, conv_b_w, conv_b_b, ln_b_g, ln_b_b, w_out_even, w_in_odd,
           rel_bias, sgu_ln_g, sgu_ln_b, sgu_w, sgu_b, w_out_odd, w_up, ffn_dw_w, ffn_dw_b, w_down):
    nbp, lp, d = x_prompt.shape
    nbs, ls, _ = x_sample.shape

    we = w_in_even[0]
    split = 2 * A_QK + A_WIDTH
    w_e = jnp.concatenate(
        [we[:, :split], we[:, split + A_GATE_RANK:],
         we[:, split:split + A_GATE_RANK],
         jnp.zeros((d, A_QK - A_GATE_RANK), F32)], axis=1).astype(BF16)
    wg2p = jnp.zeros((A_QK, A_QK), F32).at[:A_GATE_RANK].set(w_gate2[0]).astype(BF16)
    segb = jnp.asarray(np.kron(np.eye(A_HEADS), np.ones((A_DK, A_DV))), BF16)
    cw = jnp.zeros((CONV_HDR, B_WIDTH), F32).at[:B_CONV].set(conv_b_w[0])
    w_oe = w_out_even[0].astype(BF16)
    w_o = w_in_odd[0].astype(BF16)
    w_oo = w_out_odd[0].astype(BF16)
    w_up_b = w_up.astype(BF16)
    w_dn_b = w_down.astype(BF16)
    dw8 = jnp.zeros((2, 8, 2 * D_FF), F32).at[:, :3].set(ffn_dw_w)
    tril = jnp.tril(jnp.ones((D_CHUNK, D_CHUNK), F32))
    sgu_wm = sgu_w[0] * tril
    bias_p = _prompt_bias(rel_bias)
    mas, mbs = _sample_bias(rel_bias, ls)

    def ffn_args(x2, layer):
        return (x2, _row(norm_pre_ffn[layer]), w_up_b[layer], dw8[layer], _row(ffn_dw_b[layer]),
                w_dn_b[layer], _row(norm_post_ffn[layer]))

    def even_layer(x2, nb, L, s0t, hdr, conv_tm):
        z = _inproj(x2, _row(norm_pre_mix[0]), w_e, 512)
        z3 = z.reshape(nb, L, Z_WIDTH)
        o, st = _gla(z3, wg2p, _row(b_gate[0]), segb, s0t)
        c, ut = _convmod(z3, hdr, cw, _row(conv_b_b[0]), _row(ln_b_g[0]), _row(ln_b_b[0]), conv_tm)
        x1 = _evenout(o.reshape(nb * L, A_WIDTH), z, c.reshape(nb * L, B_WIDTH), x2,
                      _row(gla_norm[0]), w_oe, _row(norm_post_mix[0]), 512)
        s_new = st.reshape(nb, A_DV, A_HEADS, A_DK).transpose(0, 2, 3, 1)
        return x1, s_new, ut

    xp2 = x_prompt.reshape(nbp * lp, d)
    x1, p_gla, ut = even_layer(xp2, nbp, lp, jnp.zeros((nbp, A_DV, A_QK), F32),
                               jnp.zeros((nbp, CONV_HDR, B_WIDTH), F32), 512)
    p_conv_b = ut[:, CONV_HDR - (B_CONV - 1):]
    x2, tail0 = _ffn_prompt(*ffn_args(x1, 0), nbp, 512)
    z = _inproj(x2, _row(norm_pre_mix[1]), w_o, 512)
    z3 = z.reshape(nbp, lp, Z_WIDTH)
    oc = _attn_prompt(z3, bias_p)
    wsm = sgu_wm.astype(BF16)
    sbe = jnp.repeat(sgu_b[0].T, D_DH, axis=-1)
    x3 = _oddout_prompt(oc, z3, x2.reshape(nbp, lp, d), _row(sgu_ln_g[0]), _row(sgu_ln_b[0]),
                        wsm, sbe, w_oo, _row(norm_post_mix[1]), 512)
    x4, tail1 = _ffn_prompt(*ffn_args(x3.reshape(nbp * lp, d), 1), nbp, 512)
    y_prompt = x4.reshape(nbp, lp, d)
    p_kv = []
    for g, (window, _) in enumerate(C_PATTERNS):
        wl = min(window, lp)
        kk = z3[:, lp - wl:, (3 + g) * C_WIDTH:(4 + g) * C_WIDTH]
        vv = z3[:, lp - wl:, (6 + g) * C_WIDTH:(7 + g) * C_WIDTH]
        p_kv.append(jnp.concatenate([kk, vv], axis=-1).reshape(1, nbp, wl, 2, C_HEADS, C_DH))
    p_ffn = jnp.stack([t.reshape(nbp, -1, 8, 2 * D_FF)[:, -1, 6:8] for t in (tail0, tail1)], axis=0)

    xs2 = x_sample.reshape(nbs * ls, d)
    s0t = state_gla[0].transpose(0, 3, 1, 2).reshape(nbs, A_DV, A_QK)
    hdr = jnp.pad(state_conv_b[0], ((0, 0), (CONV_HDR - (B_CONV - 1), 0), (0, 0)))
    y1, s_gla, us = even_layer(xs2, nbs, ls, s0t, hdr, ls)
    s_conv_b = jnp.concatenate([state_conv_b[0][:, ls:], us], axis=1)

    def prev_rows(st):
        return jnp.pad(st, ((0, 0), (0, ls - st.shape[1]), (0, 0))).reshape(nbs * ls, 2 * D_FF)

    def tail_rows(tg, tv):
        return jnp.concatenate([tg.reshape(nbs, ls, D_FF)[:, ls - 2:],
                                tv.reshape(nbs, ls, D_FF)[:, ls - 2:]], axis=-1)

    y2, sg0, sv0 = _ffn_sample(*ffn_args(y1, 0), prev_rows(state_ffn_conv[0]))
    zs = _inproj(y2, _row(norm_pre_mix[1]), w_o, 512)
    zs3 = zs.reshape(nbs, ls, Z_WIDTH)
    caches = [c[0].transpose(0, 2, 3, 4, 1).reshape(nbs, 2 * C_WIDTH, c.shape[2])
              for c in (cache_c_w128, cache_c_w512, cache_c_w2048)]
    ocs, n0, n1, n2 = _attn_sample(zs3, caches, mas, mbs)
    wexp = jnp.repeat(sgu_wm[:, :ls, :ls].transpose(2, 1, 0), D_DH, axis=-1)
    sbes = jnp.repeat(sgu_b[0][:, :ls].T, D_DH, axis=-1)
    y3, s_sgu_v = _oddout_sample(ocs.reshape(nbs * ls, C_WIDTH), zs3, y2, _row(sgu_ln_g[0]),
                                 _row(sgu_ln_b[0]), wexp, sbes, w_oo, _row(norm_post_mix[1]))
    y4, sg1, sv1 = _ffn_sample(*ffn_args(y3, 1), prev_rows(state_ffn_conv[1]))
    y_sample = y4.reshape(nbs, ls, d)
    s_kv = [n.reshape(nbs, 2, C_HEADS, C_DH, n.shape[2]).transpose(0, 4, 1, 2, 3)[None]
            for n in (n0, n1, n2)]
    s_ffn = jnp.stack([tail_rows(sg0, sv0), tail_rows(sg1, sv1)], axis=0)

    return (y_prompt, y_sample, p_gla[None], p_conv_b[None], p_kv[0], p_kv[1], p_kv[2], p_ffn,
            s_gla[None], s_conv_b[None], s_kv[0], s_kv[1], s_kv[2], s_sgu_v[None], s_ffn)
```

```python
import functools
import math

import numpy as np
import jax
import jax.numpy as jnp
from jax import lax
from jax.experimental import pallas as pl
from jax.experimental.pallas import tpu as pltpu

F32 = jnp.float32
BF16 = jnp.bfloat16

D_MODEL = 1024
EPS = 1e-6
NEG = -1e30

A_HEADS = 4
A_DK = 64
A_DV = 128
A_QK = A_HEADS * A_DK
A_WIDTH = A_HEADS * A_DV
A_GATE_RANK = 16
A_GATE_NORM = 16.0
GLA_STEP = 16
B_WIDTH = 512
B_CONV = 31
CONV_HDR = 32
C_PATTERNS = ((128, 1), (512, 4), (2048, 16))
C_HEADS = 4
C_DH = 64
C_WIDTH = C_HEADS * C_DH
C_QBLOCK = 128
ATTN_UNROLL = 4
N_BUCKETS = 32
MAX_DIST = 2048
D_WIDTH = 256
D_DH = 64
D_CHUNK = 128
D_FF = 2816
FFN_CHUNK = 256
FFN_NCHUNK = D_FF // FFN_CHUNK
Z_WIDTH = 2816

VMEM_LIMIT_BYTES = 56 * 1024 * 1024

NT_DIMS = (((1,), (1,)), ((), ()))
TN_DIMS = (((0,), (0,)), ((), ()))


def _params(n_axes):
    return pltpu.CompilerParams(dimension_semantics=("arbitrary",) * n_axes,
                                vmem_limit_bytes=VMEM_LIMIT_BYTES)


def _rms(x, g):
    return x * lax.rsqrt(jnp.mean(x * x, axis=-1, keepdims=True) + EPS) * g


def _layernorm(x, g, b):
    mu = jnp.mean(x, axis=-1, keepdims=True)
    xc = x - mu
    var = jnp.mean(xc * xc, axis=-1, keepdims=True)
    return xc * lax.rsqrt(var + EPS) * g + b


def _sigmoid(x):
    return 1.0 / (1.0 + jnp.exp(-x))


def _head_mask(rows, lanes, rows_per_head, lanes_per_head):
    r = lax.broadcasted_iota(jnp.int32, (rows, lanes), 0) >> int(math.log2(rows_per_head))
    c = lax.broadcasted_iota(jnp.int32, (rows, lanes), 1) >> int(math.log2(lanes_per_head))
    return r == c


def _gelu(x):
    c = math.sqrt(2.0 / math.pi)
    return 0.5 * x * (1.0 + jnp.tanh(c * (x + 0.044715 * (x * x * x))))


def _inproj_body(x_ref, g_ref, w_ref, o_ref):
    h = _rms(x_ref[...], g_ref[...])
    o_ref[...] = jnp.dot(h.astype(BF16), w_ref[...], preferred_element_type=F32)


def _inproj(x2, g, w, tm):
    rows, d = x2.shape
    n = w.shape[1]
    return pl.pallas_call(
        _inproj_body,
        grid=(rows // tm,),
        in_specs=[pl.BlockSpec((tm, d), lambda i: (i, 0)),
                  pl.BlockSpec((1, d), lambda i: (0, 0)),
                  pl.BlockSpec((d, n), lambda i: (0, 0))],
        out_specs=pl.BlockSpec((tm, n), lambda i: (i, 0)),
        out_shape=jax.ShapeDtypeStruct((rows, n), F32),
        compiler_params=_params(1),
        name="inproj",
    )(x2, g, w)


def _gla_body(q_ref, k_ref, v_ref, glr_ref, wg_ref, bg_ref, segb_ref, s0_ref,
              o_ref, st_ref, b_sc, *, L, SB):
    xg = jnp.dot(glr_ref[0].astype(BF16), wg_ref[...], preferred_element_type=F32) + bg_ref[...]
    la = (jnp.minimum(xg, 0.0) - jnp.log(1.0 + jnp.exp(-jnp.abs(xg)))) * (1.0 / A_GATE_NORM)
    tl = lax.broadcasted_iota(jnp.int32, (L, A_QK), 0) & (SB - 1)
    sh = 1
    while sh < SB:
        la = la + jnp.where(tl >= sh, pltpu.roll(la, sh, axis=0), 0.0)
        sh *= 2
    b_sc[...] = la
    st_ref[0] = s0_ref[0]

    hm = _head_mask(A_HEADS * SB, A_QK, SB, A_DK)
    trow = lax.broadcasted_iota(jnp.int32, (SB, A_QK), 0)

    def step(i, carry):
        rows = pl.ds(pl.multiple_of(i * SB, SB), SB)
        q = q_ref[0, rows, :] * (A_DK ** -0.5)
        k = k_ref[0, rows, :]
        v = v_ref[0, rows, :]
        b = b_sc[rows, :]
        bl = b[SB - 1:SB, :]
        qt = q * jnp.exp(b)
        kh = k * jnp.exp(bl - b)
        st = st_ref[0]
        qbd = jnp.where(hm, jnp.concatenate([qt] * A_HEADS, axis=0), 0.0).astype(BF16)
        o_int = lax.dot_general(qbd, st.astype(BF16), NT_DIMS, preferred_element_type=F32)
        kbd = jnp.where(hm, jnp.concatenate([kh] * A_HEADS, axis=0), 0.0).astype(BF16)
        vst = jnp.concatenate([v[:, h * A_DV:(h + 1) * A_DV] for h in range(A_HEADS)],
                              axis=0).astype(BF16)
        upd = lax.dot_general(vst, kbd, TN_DIMS, preferred_element_type=F32)
        st_ref[0] = st * jnp.exp(bl) + upd
        ps = []
        for s in range(SB):
            e = jnp.exp(jnp.where(trow >= s, b - b[s:s + 1, :], NEG))
            ps.append(q * e * k[s:s + 1, :])
        pall = jnp.concatenate(ps, axis=0).astype(BF16)
        r = jnp.dot(pall, segb_ref[...], preferred_element_type=F32)
        od = r[0:SB] * v[0:1, :]
        for s in range(1, SB):
            od = od + r[s * SB:(s + 1) * SB] * v[s:s + 1, :]
        o = jnp.concatenate([o_int[h * SB:(h + 1) * SB] for h in range(A_HEADS)], axis=1) + od
        o_ref[0, rows, :] = o
        return carry

    nsteps = L // SB
    lax.fori_loop(0, nsteps, step, 0, unroll=2 if nsteps % 2 == 0 else 1)


def _gla(z3, wg2p, bgate, segb, s0t):
    nb, L, _ = z3.shape
    SB = GLA_STEP if L % GLA_STEP == 0 else L
    body = functools.partial(_gla_body, L=L, SB=SB)
    return pl.pallas_call(
        body,
        grid=(nb,),
        in_specs=[pl.BlockSpec((1, L, A_QK), lambda b: (b, 0, 0)),
                  pl.BlockSpec((1, L, A_QK), lambda b: (b, 0, 1)),
                  pl.BlockSpec((1, L, A_WIDTH), lambda b: (b, 0, 1)),
                  pl.BlockSpec((1, L, A_QK), lambda b: (b, 0, 10)),
                  pl.BlockSpec((A_QK, A_QK), lambda b: (0, 0)),
                  pl.BlockSpec((1, A_QK), lambda b: (0, 0)),
                  pl.BlockSpec((A_QK, A_WIDTH), lambda b: (0, 0)),
                  pl.BlockSpec((1, A_DV, A_QK), lambda b: (b, 0, 0))],
        out_specs=[pl.BlockSpec((1, L, A_WIDTH), lambda b: (b, 0, 0)),
                   pl.BlockSpec((1, A_DV, A_QK), lambda b: (b, 0, 0))],
        out_shape=[jax.ShapeDtypeStruct((nb, L, A_WIDTH), F32),
                   jax.ShapeDtypeStruct((nb, A_DV, A_QK), F32)],
        scratch_shapes=[pltpu.VMEM((L, A_QK), F32)],
        compiler_params=_params(1),
        name="gla",
    )(z3, z3, z3, z3, wg2p, bgate, segb, s0t)


def _conv_body(ga_ref, gg_ref, hdr_ref, cw_ref, cb_ref, lng_ref, lnb_ref,
               c_ref, ut_ref, xp, *, tm, tail):
    j = pl.program_id(1)

    @pl.when(j == 0)
    def _():
        xp[0:CONV_HDR, :] = hdr_ref[0]

    u = ga_ref[0] * _sigmoid(gg_ref[0])
    xp[CONV_HDR:CONV_HDR + tm, :] = u
    acc = jnp.zeros((tm, B_WIDTH), F32) + cb_ref[...]
    off = CONV_HDR - (B_CONV - 1)
    for kk in range(B_CONV):
        acc = acc + cw_ref[kk:kk + 1, :] * xp[off + kk:off + kk + tm, :]
    y = _layernorm(acc, lng_ref[...], lnb_ref[...])
    c_ref[0] = y * _sigmoid(y)
    ut_ref[0] = u[tm - tail:tm, :]
    nxt = xp[tm:tm + CONV_HDR, :]
    xp[0:CONV_HDR, :] = nxt


def _convmod(z3, hdr, cw, cb, lng, lnb, tm):
    nb, L, _ = z3.shape
    tail = min(CONV_HDR, tm)
    body = functools.partial(_conv_body, tm=tm, tail=tail)
    return pl.pallas_call(
        body,
        grid=(nb, L // tm),
        in_specs=[pl.BlockSpec((1, tm, B_WIDTH), lambda b, j: (b, j, 3)),
                  pl.BlockSpec((1, tm, B_WIDTH), lambda b, j: (b, j, 4)),
                  pl.BlockSpec((1, CONV_HDR, B_WIDTH), lambda b, j: (b, 0, 0)),
                  pl.BlockSpec((CONV_HDR, B_WIDTH), lambda b, j: (0, 0)),
                  pl.BlockSpec((1, B_WIDTH), lambda b, j: (0, 0)),
                  pl.BlockSpec((1, B_WIDTH), lambda b, j: (0, 0)),
                  pl.BlockSpec((1, B_WIDTH), lambda b, j: (0, 0))],
        out_specs=[pl.BlockSpec((1, tm, B_WIDTH), lambda b, j: (b, j, 0)),
                   pl.BlockSpec((1, tail, B_WIDTH), lambda b, j: (b, 0, 0))],
        out_shape=[jax.ShapeDtypeStruct((nb, L, B_WIDTH), F32),
                   jax.ShapeDtypeStruct((nb, tail, B_WIDTH), F32)],
        scratch_shapes=[pltpu.VMEM((CONV_HDR + tm, B_WIDTH), F32)],
        compiler_params=_params(2),
        name="convmod",
    )(z3, z3, hdr, cw, cb, lng, lnb)


def _evenout_body(o_ref, r_ref, c_ref, x_ref, gn_ref, w_ref, post_ref, xo_ref):
    o = o_ref[...]
    gn = gn_ref[...]
    oa = jnp.concatenate([_rms(o[:, h * A_DV:(h + 1) * A_DV], gn) for h in range(A_HEADS)], axis=1)
    r = r_ref[...]
    oa = oa * (r * _sigmoid(r))
    y = (jnp.dot(oa.astype(BF16), w_ref[0:A_WIDTH, :], preferred_element_type=F32)
         + jnp.dot(c_ref[...].astype(BF16), w_ref[A_WIDTH:A_WIDTH + B_WIDTH, :],
                   preferred_element_type=F32))
    xo_ref[...] = x_ref[...] + _rms(y, post_ref[...])


def _evenout(o2, z2, c2, x2, gn, w, post, tm):
    rows = x2.shape[0]
    return pl.pallas_call(
        _evenout_body,
        grid=(rows // tm,),
        in_specs=[pl.BlockSpec((tm, A_WIDTH), lambda i: (i, 0)),
                  pl.BlockSpec((tm, A_WIDTH), lambda i: (i, 2)),
                  pl.BlockSpec((tm, B_WIDTH), lambda i: (i, 0)),
                  pl.BlockSpec((tm, D_MODEL), lambda i: (i, 0)),
                  pl.BlockSpec((1, A_DV), lambda i: (0, 0)),
                  pl.BlockSpec((A_WIDTH + B_WIDTH, D_MODEL), lambda i: (0, 0)),
                  pl.BlockSpec((1, D_MODEL), lambda i: (0, 0))],
        out_specs=pl.BlockSpec((tm, D_MODEL), lambda i: (i, 0)),
        out_shape=jax.ShapeDtypeStruct((rows, D_MODEL), F32),
        compiler_params=_params(1),
        name="evenout",
    )(o2, z2, c2, x2, gn, w, post)


def _dwconv3(u, t, prev1, prev2, dw, b):
    u1 = jnp.where(t >= 1, pltpu.roll(u, 1, axis=0), prev1)
    u2 = jnp.where(t >= 2, pltpu.roll(u, 2, axis=0), prev2)
    return dw[0:1, :] * u2 + dw[1:2, :] * u1 + dw[2:3, :] * u + b


def _ffn_s_body(x_ref, pg_ref, wg_ref, wv_ref, dwg_ref, dwv_ref, bg_ref, bv_ref, wd_ref, post_ref,
                sg_ref, sv_ref, xo_ref, tg_ref, tv_ref, h_sc, acc, *, tm):
    c = pl.program_id(0)

    @pl.when(c == 0)
    def _():
        h_sc[...] = _rms(x_ref[...], pg_ref[...]).astype(BF16)
        acc[...] = jnp.zeros((tm, D_MODEL), F32)

    h = h_sc[...]
    ug = jnp.dot(h, wg_ref[...], preferred_element_type=F32)
    uv = jnp.dot(h, wv_ref[...], preferred_element_type=F32)
    t = lax.broadcasted_iota(jnp.int32, (tm, FFN_CHUNK), 0) & 7

    def conv(u, s_ref, dw_ref, b_ref):
        s = s_ref[...]
        return _dwconv3(u, t, pltpu.roll(s, tm - 1, axis=0), s, dw_ref[...], b_ref[...])

    g = conv(ug, sg_ref, dwg_ref, bg_ref)
    val = conv(uv, sv_ref, dwv_ref, bv_ref)
    act = (_gelu(g) * val).astype(BF16)
    acc[...] += jnp.dot(act, wd_ref[...], preferred_element_type=F32)
    tg_ref[...] = ug
    tv_ref[...] = uv

    @pl.when(c == FFN_NCHUNK - 1)
    def _():
        xo_ref[...] = x_ref[...] + _rms(acc[...], post_ref[...])


def _ffn_sample(x2, pg, wup, dw8, dwb, wdn, post, st8):
    tm = x2.shape[0]
    nc = FFN_NCHUNK
    g_map = lambda c: (0, c)
    v_map = lambda c: (0, nc + c)
    fix = lambda c: (0, 0)
    body = functools.partial(_ffn_s_body, tm=tm)
    return pl.pallas_call(
        body,
        grid=(nc,),
        in_specs=[pl.BlockSpec((tm, D_MODEL), fix),
                  pl.BlockSpec((1, D_MODEL), fix),
                  pl.BlockSpec((D_MODEL, FFN_CHUNK), g_map),
                  pl.BlockSpec((D_MODEL, FFN_CHUNK), v_map),
                  pl.BlockSpec((8, FFN_CHUNK), g_map),
                  pl.BlockSpec((8, FFN_CHUNK), v_map),
                  pl.BlockSpec((1, FFN_CHUNK), g_map),
                  pl.BlockSpec((1, FFN_CHUNK), v_map),
                  pl.BlockSpec((FFN_CHUNK, D_MODEL), lambda c: (c, 0)),
                  pl.BlockSpec((1, D_MODEL), fix),
                  pl.BlockSpec((tm, FFN_CHUNK), g_map),
                  pl.BlockSpec((tm, FFN_CHUNK), v_map)],
        out_specs=[pl.BlockSpec((tm, D_MODEL), fix),
                   pl.BlockSpec((tm, FFN_CHUNK), g_map),
                   pl.BlockSpec((tm, FFN_CHUNK), g_map)],
        out_shape=[jax.ShapeDtypeStruct((tm, D_MODEL), F32),
                   jax.ShapeDtypeStruct((tm, D_FF), F32),
                   jax.ShapeDtypeStruct((tm, D_FF), F32)],
        scratch_shapes=[pltpu.VMEM((tm, D_MODEL), BF16), pltpu.VMEM((tm, D_MODEL), F32)],
        compiler_params=_params(1),
        name="ffn_sample",
    )(x2, pg, wup, wup, dw8, dw8, dwb, dwb, wdn, post, st8, st8)


def _ffn_p_body(x_ref, pg_ref, wup_ref, dw_ref, b_ref, wdn_ref, post_ref,
                xo_ref, tail_ref, car, act_sc, *, tm):
    i = pl.program_id(1)

    @pl.when(i == 0)
    def _():
        car[...] = jnp.zeros((8, 2 * D_FF), F32)

    x = x_ref[...]
    h = _rms(x, pg_ref[...]).astype(BF16)

    def half(cols):
        u = jnp.dot(h, wup_ref[:, cols], preferred_element_type=F32)
        xp = jnp.concatenate([car[:, cols], u], axis=0)
        u1 = pltpu.roll(xp, 1, axis=0)[8:, :]
        u2 = pltpu.roll(xp, 2, axis=0)[8:, :]
        dw = dw_ref[:, cols]
        out = dw[0:1, :] * u2 + dw[1:2, :] * u1 + dw[2:3, :] * u + b_ref[:, cols]
        last = u[tm - 8:tm, :]
        car[:, cols] = last
        tail_ref[0, :, cols] = last
        return out

    for c in range(FFN_NCHUNK):
        g = half(slice(c * FFN_CHUNK, (c + 1) * FFN_CHUNK))
        val = half(slice(D_FF + c * FFN_CHUNK, D_FF + (c + 1) * FFN_CHUNK))
        act_sc[:, c * FFN_CHUNK:(c + 1) * FFN_CHUNK] = (_gelu(g) * val).astype(BF16)
    f = jnp.dot(act_sc[...], wdn_ref[...], preferred_element_type=F32)
    xo_ref[...] = x + _rms(f, post_ref[...])


def _ffn_prompt(x2, pg, wup, dw8, dwb, wdn, post, nb, tm):
    rows = x2.shape[0]
    nt = rows // (nb * tm)
    xmap = lambda b, i: (b * nt + i, 0)
    fix = lambda b, i: (0, 0)
    once = pl.Buffered(1)
    body = functools.partial(_ffn_p_body, tm=tm)
    return pl.pallas_call(
        body,
        grid=(nb, nt),
        in_specs=[pl.BlockSpec((tm, D_MODEL), xmap),
                  pl.BlockSpec((1, D_MODEL), fix),
                  pl.BlockSpec((D_MODEL, 2 * D_FF), fix, pipeline_mode=once),
                  pl.BlockSpec((8, 2 * D_FF), fix),
                  pl.BlockSpec((1, 2 * D_FF), fix),
                  pl.BlockSpec((D_FF, D_MODEL), fix, pipeline_mode=once),
                  pl.BlockSpec((1, D_MODEL), fix)],
        out_specs=[pl.BlockSpec((tm, D_MODEL), xmap),
                   pl.BlockSpec((1, 8, 2 * D_FF), lambda b, i: (b * nt + i, 0, 0))],
        out_shape=[jax.ShapeDtypeStruct((rows, D_MODEL), F32),
                   jax.ShapeDtypeStruct((nb * nt, 8, 2 * D_FF), F32)],
        scratch_shapes=[pltpu.VMEM((8, 2 * D_FF), F32), pltpu.VMEM((tm, D_FF), BF16)],
        compiler_params=_params(2),
        name="ffn_prompt",
    )(x2, pg, wup, dw8, dwb, wdn, post)


def _attn_p_body(q0, k0, v0, q1, k1, v1, q2, k2, v2, bias_ref, o_ref,
                 qd, kd, vd, od, ld, og, lg, *, L):
    lane = lax.broadcasted_iota(jnp.int32, (C_QBLOCK, 2 * C_DH), 1)
    first = lane < C_DH
    col = lax.broadcasted_iota(jnp.int32, (C_QBLOCK, 2 * C_QBLOCK), 1)
    groups = ((q0, k0, v0), (q1, k1, v1), (q2, k2, v2))
    for g, (qr, kr, vr) in enumerate(groups):
        d = C_PATTERNS[g][1]
        nsub = L // d
        nblk = nsub // C_QBLOCK
        if d == 1:
            qs, ks, vs = qr.at[0], kr.at[0], vr.at[0]
            os_, ls_ = og.at[g], lg.at[g]
        else:
            for r in range(d):
                dst = slice(r * nsub, (r + 1) * nsub)
                qd[dst, :] = qr[0, pl.ds(r, nsub, stride=d), :]
                kd[dst, :] = kr[0, pl.ds(r, nsub, stride=d), :]
                vd[dst, :] = vr[0, pl.ds(r, nsub, stride=d), :]
            qs, ks, vs = qd, kd, vd
            os_, ls_ = od, ld

        def blk(idx, carry, qs=qs, ks=ks, vs=vs, os_=os_, ls_=ls_, nblk=nblk, g=g):
            rows = pl.ds(pl.multiple_of(idx * C_QBLOCK, C_QBLOCK), C_QBLOCK)
            qb = qs[rows, :] * (C_DH ** -0.5)
            kc = ks[rows, :]
            vc = vs[rows, :]
            if nblk > 1:
                m = idx & (nblk - 1)
                prow = pl.ds(pl.multiple_of(jnp.maximum(idx - 1, 0) * C_QBLOCK, C_QBLOCK), C_QBLOCK)
                kcat = jnp.concatenate([ks[prow, :], kc], axis=0).astype(BF16)
                vcat = jnp.concatenate([vs[prow, :], vc], axis=0).astype(BF16)
                dead = col < jnp.where(m == 0, C_QBLOCK, 0)
            else:
                kcat = kc.astype(BF16)
                vcat = vc.astype(BF16)
            outs, lses = [], []
            for hh in range(2):
                keep = first if hh == 0 else jnp.logical_not(first)
                qm = jnp.where(keep, qb, 0.0).astype(BF16)
                s = lax.dot_general(qm, kcat, NT_DIMS, preferred_element_type=F32)
                if nblk > 1:
                    s = jnp.where(dead, NEG, s + bias_ref[g, hh])
                else:
                    s = s + bias_ref[g, hh, :, C_QBLOCK:2 * C_QBLOCK]
                mx = jnp.max(s, axis=-1, keepdims=True)
                p = jnp.exp(s - mx)
                l = jnp.sum(p, axis=-1, keepdims=True)
                r = jnp.dot(p.astype(BF16), vcat, preferred_element_type=F32)
                outs.append(r / l)
                lses.append(mx + jnp.log(l))
            os_[rows, :] = jnp.where(first, outs[0], outs[1])
            ls_[rows, :] = jnp.where(first, lses[0], lses[1])
            return carry

        lax.fori_loop(0, d * nblk, blk, 0, unroll=ATTN_UNROLL)
        if d != 1:
            for r in range(d):
                src = slice(r * nsub, (r + 1) * nsub)
                og[g, pl.ds(r, nsub, stride=d), :] = od[src, :]
                lg[g, pl.ds(r, nsub, stride=d), :] = ld[src, :]
    cr = 256
    for cidx in range(L // cr):
        rows = slice(cidx * cr, (cidx + 1) * cr)
        l0, l1, l2 = lg[0, rows, :], lg[1, rows, :], lg[2, rows, :]
        mx = jnp.maximum(jnp.maximum(l0, l1), l2)
        w0, w1, w2 = jnp.exp(l0 - mx), jnp.exp(l1 - mx), jnp.exp(l2 - mx)
        num = w0 * og[0, rows, :] + w1 * og[1, rows, :] + w2 * og[2, rows, :]
        o_ref[0, rows, :] = num / (w0 + w1 + w2)


def _attn_prompt(z3, biasmat):
    nb, L, _ = z3.shape
    lw = 2 * C_DH
    in_specs = []
    for g in range(3):
        for part in range(3):
            base = part * 6 + g * 2
            in_specs.append(pl.BlockSpec((1, L, lw), lambda b, p, base=base: (b, 0, base + p)))
    in_specs.append(pl.BlockSpec((3, 2, C_QBLOCK, 2 * C_QBLOCK), lambda b, p: (0, p, 0, 0)))
    body = functools.partial(_attn_p_body, L=L)
    return pl.pallas_call(
        body,
        grid=(nb, 2),
        in_specs=in_specs,
        out_specs=pl.BlockSpec((1, L, lw), lambda b, p: (b, 0, p)),
        out_shape=jax.ShapeDtypeStruct((nb, L, C_WIDTH), F32),
        scratch_shapes=[pltpu.VMEM((L, lw), F32)] * 5 + [pltpu.VMEM((3, L, lw), F32)] * 2,
        compiler_params=_params(2),
        name="attn_prompt",
    )(*([z3] * 9), biasmat)


def _attn_s_body(z_ref, c0, c1, c2, ma0, ma1, ma2, mb_ref, o_ref, n0, n1, n2):
    nq = 8
    z = z_ref[0]
    hm = _head_mask(C_HEADS * nq, C_WIDTH, nq, C_DH)
    lane = lax.broadcasted_iota(jnp.int32, (2 * C_WIDTH, C_QBLOCK), 1)
    outs, lses = [], []
    for g, (cref, mref, nref) in enumerate(((c0, ma0, n0), (c1, ma1, n1), (c2, ma2, n2))):
        W = C_PATTERNS[g][0]
        q = z[:, g * C_WIDTH:(g + 1) * C_WIDTH] * (C_DH ** -0.5)
        kn = z[:, (3 + g) * C_WIDTH:(4 + g) * C_WIDTH]
        vn = z[:, (6 + g) * C_WIDTH:(7 + g) * C_WIDTH]
        new = jnp.concatenate([jnp.zeros((C_QBLOCK - nq, 2 * C_WIDTH), F32),
                               jnp.concatenate([kn, vn], axis=1)], axis=0)
        new_t = new.T
        old = cref[0]
        sh = pltpu.roll(old, W - nq, axis=1)
        if W > C_QBLOCK:
            nref[0, :, 0:W - C_QBLOCK] = sh[:, 0:W - C_QBLOCK]
        nref[0, :, W - C_QBLOCK:W] = jnp.where(lane >= C_QBLOCK - nq, new_t,
                                               sh[:, W - C_QBLOCK:W])
        qbd = jnp.where(hm, jnp.concatenate([q] * C_HEADS, axis=0), 0.0).astype(BF16)
        ka = old[0:C_WIDTH, :].astype(BF16)
        va = old[C_WIDTH:2 * C_WIDTH, :].astype(BF16)
        kb = new_t[0:C_WIDTH, :].astype(BF16)
        vb = new_t[C_WIDTH:2 * C_WIDTH, :].astype(BF16)
        sa = jnp.dot(qbd, ka, preferred_element_type=F32) + mref[...]
        sb = jnp.dot(qbd, kb, preferred_element_type=F32) + mb_ref[g]
        mx = jnp.maximum(jnp.max(sa, axis=-1, keepdims=True), jnp.max(sb, axis=-1, keepdims=True))
        pa = jnp.exp(sa - mx)
        pb = jnp.exp(sb - mx)
        l = jnp.sum(pa, axis=-1, keepdims=True) + jnp.sum(pb, axis=-1, keepdims=True)
        r = (lax.dot_general(pa.astype(BF16), va, NT_DIMS, preferred_element_type=F32)
             + lax.dot_general(pb.astype(BF16), vb, NT_DIMS, preferred_element_type=F32))
        outs.append(r / l)
        lses.append(mx + jnp.log(l))
    mx = jnp.maximum(jnp.maximum(lses[0], lses[1]), lses[2])
    ws = [jnp.exp(ls - mx) for ls in lses]
    o32 = (ws[0] * outs[0] + ws[1] * outs[1] + ws[2] * outs[2]) / (ws[0] + ws[1] + ws[2])
    o32 = jnp.where(hm, o32, 0.0)
    o_ref[0] = o32[0:nq] + o32[nq:2 * nq] + o32[2 * nq:3 * nq] + o32[3 * nq:4 * nq]


def _attn_sample(z3, caches, mas, mb):
    nb, nq, _ = z3.shape
    kvw = 2 * C_WIDTH
    in_specs = [pl.BlockSpec((1, nq, Z_WIDTH), lambda b: (b, 0, 0))]
    out_specs = [pl.BlockSpec((1, nq, C_WIDTH), lambda b: (b, 0, 0))]
    out_shape = [jax.ShapeDtypeStruct((nb, nq, C_WIDTH), F32)]
    for W, _ in C_PATTERNS:
        in_specs.append(pl.BlockSpec((1, kvw, W), lambda b: (b, 0, 0)))
        out_specs.append(pl.BlockSpec((1, kvw, W), lambda b: (b, 0, 0)))
        out_shape.append(jax.ShapeDtypeStruct((nb, kvw, W), F32))
    for W, _ in C_PATTERNS:
        in_specs.append(pl.BlockSpec((C_HEADS * nq, W), lambda b: (0, 0)))
    in_specs.append(pl.BlockSpec((3, C_HEADS * nq, C_QBLOCK), lambda b: (0, 0, 0)))
    return pl.pallas_call(
        _attn_s_body,
        grid=(nb,),
        in_specs=in_specs,
        out_specs=out_specs,
        out_shape=out_shape,
        compiler_params=_params(1),
        name="attn_sample",
    )(z3, *caches, *mas, mb)


def _oddout_p_body(oc_ref, du_ref, dv_ref, lng_ref, lnb_ref, ws_ref, sb_ref, x_ref, w_ref,
                   post_ref, xo_ref, *, tm):
    u = _gelu(du_ref[0])
    vn = _layernorm(_gelu(dv_ref[0]), lng_ref[...], lnb_ref[...])
    lane_g = lax.broadcasted_iota(jnp.int32, (D_CHUNK, D_WIDTH), 1) >> int(math.log2(D_DH))
    parts = []
    for cidx in range(tm // D_CHUNK):
        vc = vn[cidx * D_CHUNK:(cidx + 1) * D_CHUNK, :]
        m = sb_ref[...]
        for g in range(D_WIDTH // D_DH):
            vm = jnp.where(lane_g == g, vc, 0.0).astype(BF16)
            m = m + jnp.dot(ws_ref[g], vm, preferred_element_type=F32)
        parts.append(m)
    od = u * jnp.concatenate(parts, axis=0)
    y = (jnp.dot(oc_ref[0].astype(BF16), w_ref[0:C_WIDTH, :], preferred_element_type=F32)
         + jnp.dot(od.astype(BF16), w_ref[C_WIDTH:C_WIDTH + D_WIDTH, :],
                   preferred_element_type=F32))
    xo_ref[0] = x_ref[0] + _rms(y, post_ref[...])


def _oddout_prompt(oc3, z3, x3, lng, lnb, wsm, sbe, w, post, tm):
    nb, L, _ = x3.shape
    body = functools.partial(_oddout_p_body, tm=tm)
    return pl.pallas_call(
        body,
        grid=(nb, L // tm),
        in_specs=[pl.BlockSpec((1, tm, C_WIDTH), lambda b, j: (b, j, 0)),
                  pl.BlockSpec((1, tm, D_WIDTH), lambda b, j: (b, j, 9)),
                  pl.BlockSpec((1, tm, D_WIDTH), lambda b, j: (b, j, 10)),
                  pl.BlockSpec((1, D_WIDTH), lambda b, j: (0, 0)),
                  pl.BlockSpec((1, D_WIDTH), lambda b, j: (0, 0)),
                  pl.BlockSpec((4, D_CHUNK, D_CHUNK), lambda b, j: (0, 0, 0)),
                  pl.BlockSpec((D_CHUNK, D_WIDTH), lambda b, j: (0, 0)),
                  pl.BlockSpec((1, tm, D_MODEL), lambda b, j: (b, j, 0)),
                  pl.BlockSpec((C_WIDTH + D_WIDTH, D_MODEL), lambda b, j: (0, 0)),
                  pl.BlockSpec((1, D_MODEL), lambda b, j: (0, 0))],
        out_specs=pl.BlockSpec((1, tm, D_MODEL), lambda b, j: (b, j, 0)),
        out_shape=jax.ShapeDtypeStruct((nb, L, D_MODEL), F32),
        compiler_params=_params(2),
        name="oddout_prompt",
    )(oc3, z3, z3, lng, lnb, wsm, sbe, x3, w, post)


def _oddout_s_body(oc_ref, du_ref, dv_ref, lng_ref, lnb_ref, we_ref, sb_ref, x_ref, w_ref,
                   post_ref, xo_ref, vn_ref):
    nb, nq, _ = du_ref.shape
    u = _gelu(du_ref[...])
    vn = _layernorm(_gelu(dv_ref[...]), lng_ref[...], lnb_ref[...])
    vn_ref[...] = vn
    m = jnp.zeros((nb, nq, D_WIDTH), F32) + sb_ref[...]
    for s in range(nq):
        m = m + we_ref[s] * vn[:, s:s + 1, :]
    od = (u * m).reshape(nb * nq, D_WIDTH)
    y = (jnp.dot(oc_ref[...].astype(BF16), w_ref[0:C_WIDTH, :], preferred_element_type=F32)
         + jnp.dot(od.astype(BF16), w_ref[C_WIDTH:C_WIDTH + D_WIDTH, :],
                   preferred_element_type=F32))
    xo_ref[...] = x_ref[...] + _rms(y, post_ref[...])


def _oddout_sample(oc2, z3, x2, lng, lnb, wexp, sbe, w, post):
    nb, nq, _ = z3.shape
    rows = nb * nq
    return pl.pallas_call(
        _oddout_s_body,
        grid=(1,),
        in_specs=[pl.BlockSpec((rows, C_WIDTH), lambda i: (0, 0)),
                  pl.BlockSpec((nb, nq, D_WIDTH), lambda i: (0, 0, 9)),
                  pl.BlockSpec((nb, nq, D_WIDTH), lambda i: (0, 0, 10)),
                  pl.BlockSpec((1, D_WIDTH), lambda i: (0, 0)),
                  pl.BlockSpec((1, D_WIDTH), lambda i: (0, 0)),
                  pl.BlockSpec((nq, nq, D_WIDTH), lambda i: (0, 0, 0)),
                  pl.BlockSpec((nq, D_WIDTH), lambda i: (0, 0)),
                  pl.BlockSpec((rows, D_MODEL), lambda i: (0, 0)),
                  pl.BlockSpec((C_WIDTH + D_WIDTH, D_MODEL), lambda i: (0, 0)),
                  pl.BlockSpec((1, D_MODEL), lambda i: (0, 0))],
        out_specs=[pl.BlockSpec((rows, D_MODEL), lambda i: (0, 0)),
                   pl.BlockSpec((nb, nq, D_WIDTH), lambda i: (0, 0, 0))],
        out_shape=[jax.ShapeDtypeStruct((rows, D_MODEL), F32),
                   jax.ShapeDtypeStruct((nb, nq, D_WIDTH), F32)],
        compiler_params=_params(1),
        name="oddout_sample",
    )(oc2, z3, z3, lng, lnb, wexp, sbe, x2, w, post)


def _t5_bucket(dist):
    max_exact = N_BUCKETS // 2
    d32 = jnp.maximum(dist, 1).astype(F32)
    large = max_exact + (jnp.log(d32 / max_exact) / math.log(MAX_DIST / max_exact)
                         * (N_BUCKETS - max_exact)).astype(jnp.int32)
    large = jnp.minimum(large, N_BUCKETS - 1)
    return jnp.where(dist < max_exact, dist, large)


def _step_bias(rel_bias, g):
    window, dil = C_PATTERNS[g]
    j = jnp.arange(window // dil + 1, dtype=jnp.int32)
    return rel_bias[_t5_bucket(dil * j), g * C_HEADS:(g + 1) * C_HEADS].T.astype(F32)


def _skew(v, rows):
    n = v.shape[-1]
    lead = v.shape[:-1]
    t = jnp.broadcast_to(v[..., None, :], lead + (rows, n)).reshape(lead + (rows * n,))
    return t[..., :rows * (n - 1)].reshape(lead + (rows, n - 1))


def _prompt_bias(rel_bias):
    mats = []
    for g in range(len(C_PATTERNS)):
        bj = _step_bias(rel_bias, g)
        v = jnp.concatenate([bj[:, ::-1], jnp.full((C_HEADS, C_QBLOCK), NEG, F32)], axis=1)
        mats.append(_skew(v, C_QBLOCK))
    return jnp.stack(mats, axis=0)


def _sample_bias(rel_bias, nq):
    mas, mbs = [], []
    i = np.arange(nq)[:, None]
    i2 = np.arange(nq)[None, :]
    for g, (window, dil) in enumerate(C_PATTERNS):
        bj = _step_bias(rel_bias, g)
        on_grid = jnp.asarray(np.arange(window + 1) % dil == 0)
        bd = jnp.where(on_grid[None], jnp.repeat(bj, dil, axis=1)[:, :window + 1], NEG)
        e = jnp.concatenate([bd[:, ::-1], jnp.full((C_HEADS, nq - 1), NEG, F32)], axis=1)
        ma = _skew(e, nq)[:, :, :window]
        mas.append(ma.reshape(C_HEADS * nq, window))
        dist = i - i2
        ok = jnp.asarray((dist >= 0) & (dist % dil == 0))
        mb = jnp.where(ok[None], bd[:, np.maximum(dist, 0)], NEG)
        mb = jnp.pad(mb, ((0, 0), (0, 0), (C_QBLOCK - nq, 0)), constant_values=NEG)
        mbs.append(mb.reshape(C_HEADS * nq, C_QBLOCK))
    return mas, jnp.stack(mbs, axis=0)


def _row(v):
    return v.reshape(1, -1).astype(F32)


def kernel(x_prompt, x_sample, state_gla, state_conv_b, cache_c_w128, cache_c_w512, cache_c_w2048,
           state_ffn_conv, norm_pre_mix, norm_post_mix, norm_pre_ffn, norm_post_ffn, w_in_even,
           w_gate2, b_gate, gla_norm, conv_b_w, conv_b_b, ln_b_g, ln_b_b, w_out_even, w_in_odd,
           rel_bias, sgu_ln_g, sgu_ln_b, sgu_w, sgu_b, w_out_odd, w_up, ffn_dw_w, ffn_dw_b, w_down):
    nbp, lp, d = x_prompt.shape
    nbs, ls, _ = x_sample.shape

    we = w_in_even[0]
    split = 2 * A_QK + A_WIDTH
    w_e = jnp.concatenate(
        [we[:, :split], we[:, split + A_GATE_RANK:],
         we[:, split:split + A_GATE_RANK],
         jnp.zeros((d, A_QK - A_GATE_RANK), F32)], axis=1).astype(BF16)
    wg2p = jnp.zeros((A_QK, A_QK), F32).at[:A_GATE_RANK].set(w_gate2[0]).astype(BF16)
    segb = jnp.asarray(np.kron(np.eye(A_HEADS), np.ones((A_DK, A_DV))), BF16)
    cw = jnp.zeros((CONV_HDR, B_WIDTH), F32).at[:B_CONV].set(conv_b_w[0])
    w_oe = w_out_even[0].astype(BF16)
    w_o = w_in_odd[0].astype(BF16)
    w_oo = w_out_odd[0].astype(BF16)
    w_up_b = w_up.astype(BF16)
    w_dn_b = w_down.astype(BF16)
    dw8 = jnp.zeros((2, 8, 2 * D_FF), F32).at[:, :3].set(ffn_dw_w)
    tril = jnp.tril(jnp.ones((D_CHUNK, D_CHUNK), F32))
    sgu_wm = sgu_w[0] * tril
    bias_p = _prompt_bias(rel_bias)
    mas, mbs = _sample_bias(rel_bias, ls)

    def ffn_args(x2, layer):
        return (x2, _row(norm_pre_ffn[layer]), w_up_b[layer], dw8[layer], _row(ffn_dw_b[layer]),
                w_dn_b[layer], _row(norm_post_ffn[layer]))

    def even_layer(x2, nb, L, s0t, hdr, conv_tm):
        z = _inproj(x2, _row(norm_pre_mix[0]), w_e, 512)
        z3 = z.reshape(nb, L, Z_WIDTH)
        o, st = _gla(z3, wg2p, _row(b_gate[0]), segb, s0t)
        c, ut = _convmod(z3, hdr, cw, _row(conv_b_b[0]), _row(ln_b_g[0]), _row(ln_b_b[0]), conv_tm)
        x1 = _evenout(o.reshape(nb * L, A_WIDTH), z, c.reshape(nb * L, B_WIDTH), x2,
                      _row(gla_norm[0]), w_oe, _row(norm_post_mix[0]), 512)
        s_new = st.reshape(nb, A_DV, A_HEADS, A_DK).transpose(0, 2, 3, 1)
        return x1, s_new, ut

    xp2 = x_prompt.reshape(nbp * lp, d)
    x1, p_gla, ut = even_layer(xp2, nbp, lp, jnp.zeros((nbp, A_DV, A_QK), F32),
                               jnp.zeros((nbp, CONV_HDR, B_WIDTH), F32), 512)
    p_conv_b = ut[:, CONV_HDR - (B_CONV - 1):]
    x2, tail0 = _ffn_prompt(*ffn_args(x1, 0), nbp, 512)
    z = _inproj(x2, _row(norm_pre_mix[1]), w_o, 512)
    z3 = z.reshape(nbp, lp, Z_WIDTH)
    oc = _attn_prompt(z3, bias_p)
    wsm = sgu_wm.astype(BF16)
    sbe = jnp.repeat(sgu_b[0].T, D_DH, axis=-1)
    x3 = _oddout_prompt(oc, z3, x2.reshape(nbp, lp, d), _row(sgu_ln_g[0]), _row(sgu_ln_b[0]),
                        wsm, sbe, w_oo, _row(norm_post_mix[1]), 512)
    x4, tail1 = _ffn_prompt(*ffn_args(x3.reshape(nbp * lp, d), 1), nbp, 512)
    y_prompt = x4.reshape(nbp, lp, d)
    p_kv = []
    for g, (window, _) in enumerate(C_PATTERNS):
        wl = min(window, lp)
        kk = z3[:, lp - wl:, (3 + g) * C_WIDTH:(4 + g) * C_WIDTH]
        vv = z3[:, lp - wl:, (6 + g) * C_WIDTH:(7 + g) * C_WIDTH]
        p_kv.append(jnp.concatenate([kk, vv], axis=-1).reshape(1, nbp, wl, 2, C_HEADS, C_DH))
    p_ffn = jnp.stack([t.reshape(nbp, -1, 8, 2 * D_FF)[:, -1, 6:8] for t in (tail0, tail1)], axis=0)

    xs2 = x_sample.reshape(nbs * ls, d)
    s0t = state_gla[0].transpose(0, 3, 1, 2).reshape(nbs, A_DV, A_QK)
    hdr = jnp.pad(state_conv_b[0], ((0, 0), (CONV_HDR - (B_CONV - 1), 0), (0, 0)))
    y1, s_gla, us = even_layer(xs2, nbs, ls, s0t, hdr, ls)
    s_conv_b = jnp.concatenate([state_conv_b[0][:, ls:], us], axis=1)

    def prev_rows(st):
        return jnp.pad(st, ((0, 0), (0, ls - st.shape[1]), (0, 0))).reshape(nbs * ls, 2 * D_FF)

    def tail_rows(tg, tv):
        return jnp.concatenate([tg.reshape(nbs, ls, D_FF)[:, ls - 2:],
                                tv.reshape(nbs, ls, D_FF)[:, ls - 2:]], axis=-1)

    y2, sg0, sv0 = _ffn_sample(*ffn_args(y1, 0), prev_rows(state_ffn_conv[0]))
    zs = _inproj(y2, _row(norm_pre_mix[1]), w_o, 512)
    zs3 = zs.reshape(nbs, ls, Z_WIDTH)
    caches = [c[0].transpose(0, 2, 3, 4, 1).reshape(nbs, 2 * C_WIDTH, c.shape[2])
              for c in (cache_c_w128, cache_c_w512, cache_c_w2048)]
    ocs, n0, n1, n2 = _attn_sample(zs3, caches, mas, mbs)
    wexp = jnp.repeat(sgu_wm[:, :ls, :ls].transpose(2, 1, 0), D_DH, axis=-1)
    sbes = jnp.repeat(sgu_b[0][:, :ls].T, D_DH, axis=-1)
    y3, s_sgu_v = _oddout_sample(ocs.reshape(nbs * ls, C_WIDTH), zs3, y2, _row(sgu_ln_g[0]),
                                 _row(sgu_ln_b[0]), wexp, sbes, w_oo, _row(norm_post_mix[1]))
    y4, sg1, sv1 = _ffn_sample(*ffn_args(y3, 1), prev_rows(state_ffn_conv[1]))
    y_sample = y4.reshape(nbs, ls, d)
    s_kv = [n.reshape(nbs, 2, C_HEADS, C_DH, n.shape[2]).transpose(0, 4, 1, 2, 3)[None]
            for n in (n0, n1, n2)]
    s_ffn = jnp.stack([tail_rows(sg0, sv0), tail_rows(sg1, sv1)], axis=0)

    return (y_prompt, y_sample, p_gla[None], p_conv_b[None], p_kv[0], p_kv[1], p_kv[2], p_ffn,
            s_gla[None], s_conv_b[None], s_kv[0], s_kv[1], s_kv[2], s_sgu_v[None], s_ffn)
```

```python
import functools
import math

import numpy as np
import jax
import jax.numpy as jnp
from jax import lax
from jax.experimental import pallas as pl
from jax.experimental.pallas import tpu as pltpu

F32 = jnp.float32
BF16 = jnp.bfloat16

D_MODEL = 1024
EPS = 1e-6
NEG = -1e30

A_HEADS = 4
A_DK = 64
A_DV = 128
A_QK = A_HEADS * A_DK
A_WIDTH = A_HEADS * A_DV
A_GATE_RANK = 16
A_GATE_NORM = 16.0
GLA_STEP = 16
GLA_UNROLL = 4
SAMPLE_NSEQ = 8
B_WIDTH = 512
B_CONV = 31
CONV_HDR = 32
C_PATTERNS = ((128, 1), (512, 4), (2048, 16))
C_HEADS = 4
C_DH = 64
C_WIDTH = C_HEADS * C_DH
C_QBLOCK = 128
ATTN_UNROLL = 4
N_BUCKETS = 32
MAX_DIST = 2048
D_WIDTH = 256
D_DH = 64
D_CHUNK = 128
D_FF = 2816
FFN_CHUNK = 256
FFN_NCHUNK = D_FF // FFN_CHUNK
Z_WIDTH = 2816

VMEM_LIMIT_BYTES = 56 * 1024 * 1024

NT_DIMS = (((1,), (1,)), ((), ()))
TN_DIMS = (((0,), (0,)), ((), ()))


def _params(n_axes):
    return pltpu.CompilerParams(dimension_semantics=("arbitrary",) * n_axes,
                                vmem_limit_bytes=VMEM_LIMIT_BYTES)


def _rms(x, g):
    return x * lax.rsqrt(jnp.mean(x * x, axis=-1, keepdims=True) + EPS) * g


def _layernorm(x, g, b):
    mu = jnp.mean(x, axis=-1, keepdims=True)
    xc = x - mu
    var = jnp.mean(xc * xc, axis=-1, keepdims=True)
    return xc * lax.rsqrt(var + EPS) * g + b


def _sigmoid(x):
    return 1.0 / (1.0 + jnp.exp(-x))


def _head_mask(rows, lanes, rows_per_head, lanes_per_head):
    r = lax.broadcasted_iota(jnp.int32, (rows, lanes), 0) >> int(math.log2(rows_per_head))
    c = lax.broadcasted_iota(jnp.int32, (rows, lanes), 1) >> int(math.log2(lanes_per_head))
    return r == c


def _gelu(x):
    c = math.sqrt(2.0 / math.pi)
    return 0.5 * x * (1.0 + jnp.tanh(c * (x + 0.044715 * (x * x * x))))


def _inproj_body(x_ref, g_ref, w_ref, o_ref):
    h = _rms(x_ref[...], g_ref[...])
    o_ref[...] = jnp.dot(h.astype(BF16), w_ref[...], preferred_element_type=F32)


def _inproj(x2, g, w, tm):
    rows, d = x2.shape
    n = w.shape[1]
    return pl.pallas_call(
        _inproj_body,
        grid=(rows // tm,),
        in_specs=[pl.BlockSpec((tm, d), lambda i: (i, 0)),
                  pl.BlockSpec((1, d), lambda i: (0, 0)),
                  pl.BlockSpec((d, n), lambda i: (0, 0))],
        out_specs=pl.BlockSpec((tm, n), lambda i: (i, 0)),
        out_shape=jax.ShapeDtypeStruct((rows, n), F32),
        compiler_params=_params(1),
        name="inproj",
    )(x2, g, w)


def _gla_body(q_ref, k_ref, v_ref, glr_ref, wg_ref, bg_ref, segb_ref, s0_ref,
              o_ref, st_ref, b_sc, *, L, SB, nseq):
    glr = glr_ref[...].reshape(nseq * L, A_QK)
    xg = jnp.dot(glr.astype(BF16), wg_ref[...], preferred_element_type=F32) + bg_ref[...]
    la = (jnp.minimum(xg, 0.0) - jnp.log(1.0 + jnp.exp(-jnp.abs(xg)))) * (1.0 / A_GATE_NORM)
    tl = lax.broadcasted_iota(jnp.int32, (nseq * L, A_QK), 0) & (SB - 1)
    sh = 1
    while sh < SB:
        la = la + jnp.where(tl >= sh, pltpu.roll(la, sh, axis=0), 0.0)
        sh *= 2
    b_sc[...] = la.reshape(nseq, L, A_QK)
    st_ref[...] = s0_ref[...]

    hm = _head_mask(A_HEADS * SB, A_QK, SB, A_DK)
    tri = lax.broadcasted_iota(jnp.int32, (8, A_QK), 0)
    ntile = SB // 8

    def one(si, rows):
        q = q_ref[si, rows, :] * (A_DK ** -0.5)
        k = k_ref[si, rows, :]
        v = v_ref[si, rows, :]
        b = b_sc[si, rows, :]
        bl = b[SB - 1:SB, :]
        qt = q * jnp.exp(b)
        kh = k * jnp.exp(bl - b)
        st = st_ref[si]
        qbd = jnp.where(hm, jnp.concatenate([qt] * A_HEADS, axis=0), 0.0).astype(BF16)
        o_int = lax.dot_general(qbd, st.astype(BF16), NT_DIMS, preferred_element_type=F32)
        kbd = jnp.where(hm, jnp.concatenate([kh] * A_HEADS, axis=0), 0.0).astype(BF16)
        vst = jnp.concatenate([v[:, h * A_DV:(h + 1) * A_DV] for h in range(A_HEADS)],
                              axis=0).astype(BF16)
        upd = lax.dot_general(vst, kbd, TN_DIMS, preferred_element_type=F32)
        st_ref[si] = st * jnp.exp(bl) + upd
        ps = []
        for s in range(SB):
            lo = 8 * (s // 8)
            bs = b[s:s + 1, :]
            ks = k[s:s + 1, :]
            dd = b[lo:lo + 8, :] - bs
            if s % 8:
                dd = jnp.where(tri >= s % 8, dd, NEG)
            ps.append(q[lo:lo + 8, :] * jnp.exp(dd) * ks)
            if lo + 8 < SB:
                ps.append(q[lo + 8:SB, :] * jnp.exp(b[lo + 8:SB, :] - bs) * ks)
        pall = jnp.concatenate(ps, axis=0).astype(BF16)
        r = jnp.dot(pall, segb_ref[...], preferred_element_type=F32)
        od = [None] * ntile
        off = 0
        for s in range(SB):
            for tq in range(s // 8, ntile):
                term = r[off:off + 8, :] * v[s:s + 1, :]
                od[tq] = term if od[tq] is None else od[tq] + term
                off += 8
        o = (jnp.concatenate([o_int[h * SB:(h + 1) * SB] for h in range(A_HEADS)], axis=1)
             + jnp.concatenate(od, axis=0))
        o_ref[si, rows, :] = o

    nsteps = L // SB
    if nsteps == 1:
        for si in range(nseq):
            one(si, slice(0, SB))
    else:
        def step(i, carry):
            rows = pl.ds(pl.multiple_of(i * SB, SB), SB)
            for si in range(nseq):
                one(si, rows)
            return carry

        lax.fori_loop(0, nsteps, step, 0, unroll=GLA_UNROLL)


def _gla(z3, wg2p, bgate, segb, s0t, nseq):
    nb, L, _ = z3.shape
    SB = GLA_STEP if L % GLA_STEP == 0 else L
    assert nseq == 1 or SB == L
    body = functools.partial(_gla_body, L=L, SB=SB, nseq=nseq)
    return pl.pallas_call(
        body,
        grid=(nb // nseq,),
        in_specs=[pl.BlockSpec((nseq, L, A_QK), lambda b: (b, 0, 0)),
                  pl.BlockSpec((nseq, L, A_QK), lambda b: (b, 0, 1)),
                  pl.BlockSpec((nseq, L, A_WIDTH), lambda b: (b, 0, 1)),
                  pl.BlockSpec((nseq, L, A_QK), lambda b: (b, 0, 10)),
                  pl.BlockSpec((A_QK, A_QK), lambda b: (0, 0)),
                  pl.BlockSpec((1, A_QK), lambda b: (0, 0)),
                  pl.BlockSpec((A_QK, A_WIDTH), lambda b: (0, 0)),
                  pl.BlockSpec((nseq, A_DV, A_QK), lambda b: (b, 0, 0))],
        out_specs=[pl.BlockSpec((nseq, L, A_WIDTH), lambda b: (b, 0, 0)),
                   pl.BlockSpec((nseq, A_DV, A_QK), lambda b: (b, 0, 0))],
        out_shape=[jax.ShapeDtypeStruct((nb, L, A_WIDTH), F32),
                   jax.ShapeDtypeStruct((nb, A_DV, A_QK), F32)],
        scratch_shapes=[pltpu.VMEM((nseq, L, A_QK), F32)],
        compiler_params=_params(1),
        name="gla",
    )(z3, z3, z3, z3, wg2p, bgate, segb, s0t)


def _conv_body(ga_ref, gg_ref, hdr_ref, cw_ref, cb_ref, lng_ref, lnb_ref,
               c_ref, ut_ref, xp, *, tm, tail, nseq):
    j = pl.program_id(1)
    n = tm + 8
    for si in range(nseq):
        xs = xp.at[si]

        @pl.when(j == 0)
        def _(xs=xs, si=si):
            xs[0:CONV_HDR, :] = hdr_ref[si]
            xs[CONV_HDR + tm:CONV_HDR + n, :] = jnp.zeros((8, B_WIDTH), F32)

        u = ga_ref[si] * _sigmoid(gg_ref[si])
        xs[CONV_HDR:CONV_HDR + tm, :] = u
        acc = None
        for r in range(7, -1, -1):
            z = None
            for q in range(5):
                kk = 8 * q + r - (CONV_HDR - (B_CONV - 1))
                if 0 <= kk < B_CONV:
                    term = cw_ref[kk:kk + 1, :] * xs[8 * q:8 * q + n, :]
                    z = term if z is None else z + term
            acc = z if acc is None else z + pltpu.roll(acc, n - 1, axis=0)
        y = _layernorm(acc[0:tm, :] + cb_ref[...], lng_ref[...], lnb_ref[...])
        c_ref[si] = y * _sigmoid(y)
        ut_ref[si] = u[tm - tail:tm, :]
        nxt = xs[tm:tm + CONV_HDR, :]
        xs[0:CONV_HDR, :] = nxt


def _convmod(z3, hdr, cw, cb, lng, lnb, tm, nseq):
    nb, L, _ = z3.shape
    assert nseq == 1 or L == tm
    tail = min(CONV_HDR, tm)
    body = functools.partial(_conv_body, tm=tm, tail=tail, nseq=nseq)
    return pl.pallas_call(
        body,
        grid=(nb // nseq, L // tm),
        in_specs=[pl.BlockSpec((nseq, tm, B_WIDTH), lambda b, j: (b, j, 3)),
                  pl.BlockSpec((nseq, tm, B_WIDTH), lambda b, j: (b, j, 4)),
                  pl.BlockSpec((nseq, CONV_HDR, B_WIDTH), lambda b, j: (b, 0, 0)),
                  pl.BlockSpec((CONV_HDR, B_WIDTH), lambda b, j: (0, 0)),
                  pl.BlockSpec((1, B_WIDTH), lambda b, j: (0, 0)),
                  pl.BlockSpec((1, B_WIDTH), lambda b, j: (0, 0)),
                  pl.BlockSpec((1, B_WIDTH), lambda b, j: (0, 0))],
        out_specs=[pl.BlockSpec((nseq, tm, B_WIDTH), lambda b, j: (b, j, 0)),
                   pl.BlockSpec((nseq, tail, B_WIDTH), lambda b, j: (b, 0, 0))],
        out_shape=[jax.ShapeDtypeStruct((nb, L, B_WIDTH), F32),
                   jax.ShapeDtypeStruct((nb, tail, B_WIDTH), F32)],
        scratch_shapes=[pltpu.VMEM((nseq, CONV_HDR + tm + 8, B_WIDTH), F32)],
        compiler_params=_params(2),
        name="convmod",
    )(z3, z3, hdr, cw, cb, lng, lnb)


def _evenout_body(o_ref, r_ref, c_ref, x_ref, gn_ref, w_ref, post_ref, xo_ref):
    o = o_ref[...]
    gn = gn_ref[...]
    oa = jnp.concatenate([_rms(o[:, h * A_DV:(h + 1) * A_DV], gn) for h in range(A_HEADS)], axis=1)
    r = r_ref[...]
    oa = oa * (r * _sigmoid(r))
    y = (jnp.dot(oa.astype(BF16), w_ref[0:A_WIDTH, :], preferred_element_type=F32)
         + jnp.dot(c_ref[...].astype(BF16), w_ref[A_WIDTH:A_WIDTH + B_WIDTH, :],
                   preferred_element_type=F32))
    xo_ref[...] = x_ref[...] + _rms(y, post_ref[...])


def _evenout(o2, z2, c2, x2, gn, w, post, tm):
    rows = x2.shape[0]
    return pl.pallas_call(
        _evenout_body,
        grid=(rows // tm,),
        in_specs=[pl.BlockSpec((tm, A_WIDTH), lambda i: (i, 0)),
                  pl.BlockSpec((tm, A_WIDTH), lambda i: (i, 2)),
                  pl.BlockSpec((tm, B_WIDTH), lambda i: (i, 0)),
                  pl.BlockSpec((tm, D_MODEL), lambda i: (i, 0)),
                  pl.BlockSpec((1, A_DV), lambda i: (0, 0)),
                  pl.BlockSpec((A_WIDTH + B_WIDTH, D_MODEL), lambda i: (0, 0)),
                  pl.BlockSpec((1, D_MODEL), lambda i: (0, 0))],
        out_specs=pl.BlockSpec((tm, D_MODEL), lambda i: (i, 0)),
        out_shape=jax.ShapeDtypeStruct((rows, D_MODEL), F32),
        compiler_params=_params(1),
        name="evenout",
    )(o2, z2, c2, x2, gn, w, post)


def _dwconv3(u, t, prev1, prev2, dw, b):
    u1 = jnp.where(t >= 1, pltpu.roll(u, 1, axis=0), prev1)
    u2 = jnp.where(t >= 2, pltpu.roll(u, 2, axis=0), prev2)
    return dw[0:1, :] * u2 + dw[1:2, :] * u1 + dw[2:3, :] * u + b


def _ffn_s_body(x_ref, pg_ref, wg_ref, wv_ref, dwg_ref, dwv_ref, bg_ref, bv_ref, wd_ref, post_ref,
                sg_ref, sv_ref, xo_ref, tg_ref, tv_ref, h_sc, acc, *, tm):
    c = pl.program_id(0)

    @pl.when(c == 0)
    def _():
        h_sc[...] = _rms(x_ref[...], pg_ref[...]).astype(BF16)
        acc[...] = jnp.zeros((tm, D_MODEL), F32)

    h = h_sc[...]
    ug = jnp.dot(h, wg_ref[...], preferred_element_type=F32)
    uv = jnp.dot(h, wv_ref[...], preferred_element_type=F32)
    t = lax.broadcasted_iota(jnp.int32, (tm, FFN_CHUNK), 0) & 7

    def conv(u, s_ref, dw_ref, b_ref):
        s = s_ref[...]
        return _dwconv3(u, t, pltpu.roll(s, tm - 1, axis=0), s, dw_ref[...], b_ref[...])

    g = conv(ug, sg_ref, dwg_ref, bg_ref)
    val = conv(uv, sv_ref, dwv_ref, bv_ref)
    act = (_gelu(g) * val).astype(BF16)
    acc[...] += jnp.dot(act, wd_ref[...], preferred_element_type=F32)
    tg_ref[...] = ug
    tv_ref[...] = uv

    @pl.when(c == FFN_NCHUNK - 1)
    def _():
        xo_ref[...] = x_ref[...] + _rms(acc[...], post_ref[...])


def _ffn_sample(x2, pg, wup, dw8, dwb, wdn, post, st8):
    tm = x2.shape[0]
    nc = FFN_NCHUNK
    g_map = lambda c: (0, c)
    v_map = lambda c: (0, nc + c)
    fix = lambda c: (0, 0)
    body = functools.partial(_ffn_s_body, tm=tm)
    return pl.pallas_call(
        body,
        grid=(nc,),
        in_specs=[pl.BlockSpec((tm, D_MODEL), fix),
                  pl.BlockSpec((1, D_MODEL), fix),
                  pl.BlockSpec((D_MODEL, FFN_CHUNK), g_map),
                  pl.BlockSpec((D_MODEL, FFN_CHUNK), v_map),
                  pl.BlockSpec((8, FFN_CHUNK), g_map),
                  pl.BlockSpec((8, FFN_CHUNK), v_map),
                  pl.BlockSpec((1, FFN_CHUNK), g_map),
                  pl.BlockSpec((1, FFN_CHUNK), v_map),
                  pl.BlockSpec((FFN_CHUNK, D_MODEL), lambda c: (c, 0)),
                  pl.BlockSpec((1, D_MODEL), fix),
                  pl.BlockSpec((tm, FFN_CHUNK), g_map),
                  pl.BlockSpec((tm, FFN_CHUNK), v_map)],
        out_specs=[pl.BlockSpec((tm, D_MODEL), fix),
                   pl.BlockSpec((tm, FFN_CHUNK), g_map),
                   pl.BlockSpec((tm, FFN_CHUNK), g_map)],
        out_shape=[jax.ShapeDtypeStruct((tm, D_MODEL), F32),
                   jax.ShapeDtypeStruct((tm, D_FF), F32),
                   jax.ShapeDtypeStruct((tm, D_FF), F32)],
        scratch_shapes=[pltpu.VMEM((tm, D_MODEL), BF16), pltpu.VMEM((tm, D_MODEL), F32)],
        compiler_params=_params(1),
        name="ffn_sample",
    )(x2, pg, wup, wup, dw8, dw8, dwb, dwb, wdn, post, st8, st8)


def _ffn_p_body(x_ref, pg_ref, wup_ref, dw_ref, b_ref, wdn_ref, post_ref,
                xo_ref, tail_ref, car, act_sc, *, tm):
    i = pl.program_id(1)

    @pl.when(i == 0)
    def _():
        car[...] = jnp.zeros((8, 2 * D_FF), F32)

    x = x_ref[...]
    h = _rms(x, pg_ref[...]).astype(BF16)

    def half(cols):
        u = jnp.dot(h, wup_ref[:, cols], preferred_element_type=F32)
        xp = jnp.concatenate([car[:, cols], u], axis=0)
        u1 = pltpu.roll(xp, 1, axis=0)[8:, :]
        u2 = pltpu.roll(xp, 2, axis=0)[8:, :]
        dw = dw_ref[:, cols]
        out = dw[0:1, :] * u2 + dw[1:2, :] * u1 + dw[2:3, :] * u + b_ref[:, cols]
        last = u[tm - 8:tm, :]
        car[:, cols] = last
        tail_ref[0, :, cols] = last
        return out

    for c in range(FFN_NCHUNK):
        g = half(slice(c * FFN_CHUNK, (c + 1) * FFN_CHUNK))
        val = half(slice(D_FF + c * FFN_CHUNK, D_FF + (c + 1) * FFN_CHUNK))
        act_sc[:, c * FFN_CHUNK:(c + 1) * FFN_CHUNK] = (_gelu(g) * val).astype(BF16)
    f = jnp.dot(act_sc[...], wdn_ref[...], preferred_element_type=F32)
    xo_ref[...] = x + _rms(f, post_ref[...])


def _ffn_prompt(x2, pg, wup, dw8, dwb, wdn, post, nb, tm):
    rows = x2.shape[0]
    nt = rows // (nb * tm)
    xmap = lambda b, i: (b * nt + i, 0)
    fix = lambda b, i: (0, 0)
    once = pl.Buffered(1)
    body = functools.partial(_ffn_p_body, tm=tm)
    return pl.pallas_call(
        body,
        grid=(nb, nt),
        in_specs=[pl.BlockSpec((tm, D_MODEL), xmap),
                  pl.BlockSpec((1, D_MODEL), fix),
                  pl.BlockSpec((D_MODEL, 2 * D_FF), fix, pipeline_mode=once),
                  pl.BlockSpec((8, 2 * D_FF), fix),
                  pl.BlockSpec((1, 2 * D_FF), fix),
                  pl.BlockSpec((D_FF, D_MODEL), fix, pipeline_mode=once),
                  pl.BlockSpec((1, D_MODEL), fix)],
        out_specs=[pl.BlockSpec((tm, D_MODEL), xmap),
                   pl.BlockSpec((1, 8, 2 * D_FF), lambda b, i: (b * nt + i, 0, 0))],
        out_shape=[jax.ShapeDtypeStruct((rows, D_MODEL), F32),
                   jax.ShapeDtypeStruct((nb * nt, 8, 2 * D_FF), F32)],
        scratch_shapes=[pltpu.VMEM((8, 2 * D_FF), F32), pltpu.VMEM((tm, D_FF), BF16)],
        compiler_params=_params(2),
        name="ffn_prompt",
    )(x2, pg, wup, dw8, dwb, wdn, post)


def _attn_p_body(q0, k0, v0, q1, k1, v1, q2, k2, v2, bias_ref, o_ref,
                 qd, kd, vd, od, ld, og, lg, *, L):
    lane = lax.broadcasted_iota(jnp.int32, (C_QBLOCK, 2 * C_DH), 1)
    first = lane < C_DH
    col = lax.broadcasted_iota(jnp.int32, (C_QBLOCK, 2 * C_QBLOCK), 1)
    groups = ((q0, k0, v0), (q1, k1, v1), (q2, k2, v2))
    for g, (qr, kr, vr) in enumerate(groups):
        d = C_PATTERNS[g][1]
        nsub = L // d
        nblk = nsub // C_QBLOCK
        if d == 1:
            qs, ks, vs = qr.at[0], kr.at[0], vr.at[0]
            os_, ls_ = og.at[g], lg.at[g]
        else:
            for r in range(d):
                dst = slice(r * nsub, (r + 1) * nsub)
                qd[dst, :] = qr[0, pl.ds(r, nsub, stride=d), :]
                kd[dst, :] = kr[0, pl.ds(r, nsub, stride=d), :]
                vd[dst, :] = vr[0, pl.ds(r, nsub, stride=d), :]
            qs, ks, vs = qd, kd, vd
            os_, ls_ = od, ld

        def blk(idx, carry, qs=qs, ks=ks, vs=vs, os_=os_, ls_=ls_, nblk=nblk, g=g):
            rows = pl.ds(pl.multiple_of(idx * C_QBLOCK, C_QBLOCK), C_QBLOCK)
            qb = qs[rows, :] * (C_DH ** -0.5)
            kc = ks[rows, :]
            vc = vs[rows, :]
            m = idx & (nblk - 1)
            prow = pl.ds(pl.multiple_of(jnp.maximum(idx - 1, 0) * C_QBLOCK, C_QBLOCK), C_QBLOCK)
            kcat = jnp.concatenate([ks[prow, :], kc], axis=0).astype(BF16)
            vcat = jnp.concatenate([vs[prow, :], vc], axis=0).astype(BF16)
            dead = col < jnp.where(m == 0, C_QBLOCK, 0)
            outs, lses = [], []
            for hh in range(2):
                keep = first if hh == 0 else jnp.logical_not(first)
                qm = jnp.where(keep, qb, 0.0).astype(BF16)
                s = lax.dot_general(qm, kcat, NT_DIMS, preferred_element_type=F32)
                s = jnp.where(dead, NEG, s + bias_ref[g, hh])
                mx = jnp.max(s, axis=-1, keepdims=True)
                p = jnp.exp(s - mx)
                l = jnp.sum(p, axis=-1, keepdims=True)
                r = jnp.dot(p.astype(BF16), vcat, preferred_element_type=F32)
                outs.append(r / l)
                lses.append(mx + jnp.log(l))
            os_[rows, :] = jnp.where(first, outs[0], outs[1])
            ls_[rows, :] = jnp.where(first, lses[0], lses[1])
            return carry

        lax.fori_loop(0, d * nblk, blk, 0, unroll=ATTN_UNROLL)
        if d != 1:
            for r in range(d):
                src = slice(r * nsub, (r + 1) * nsub)
                og[g, pl.ds(r, nsub, stride=d), :] = od[src, :]
                lg[g, pl.ds(r, nsub, stride=d), :] = ld[src, :]
    cr = 256
    for cidx in range(L // cr):
        rows = slice(cidx * cr, (cidx + 1) * cr)
        l0, l1, l2 = lg[0, rows, :], lg[1, rows, :], lg[2, rows, :]
        mx = jnp.maximum(jnp.maximum(l0, l1), l2)
        w0, w1, w2 = jnp.exp(l0 - mx), jnp.exp(l1 - mx), jnp.exp(l2 - mx)
        num = w0 * og[0, rows, :] + w1 * og[1, rows, :] + w2 * og[2, rows, :]
        o_ref[0, rows, :] = num / (w0 + w1 + w2)


def _attn_prompt(z3, biasmat):
    nb, L, _ = z3.shape
    lw = 2 * C_DH
    in_specs = []
    for g in range(3):
        for part in range(3):
            base = part * 6 + g * 2
            in_specs.append(pl.BlockSpec((1, L, lw), lambda b, p, base=base: (b, 0, base + p)))
    in_specs.append(pl.BlockSpec((3, 2, C_QBLOCK, 2 * C_QBLOCK), lambda b, p: (0, p, 0, 0)))
    body = functools.partial(_attn_p_body, L=L)
    return pl.pallas_call(
        body,
        grid=(nb, 2),
        in_specs=in_specs,
        out_specs=pl.BlockSpec((1, L, lw), lambda b, p: (b, 0, p)),
        out_shape=jax.ShapeDtypeStruct((nb, L, C_WIDTH), F32),
        scratch_shapes=[pltpu.VMEM((L, lw), F32)] * 5 + [pltpu.VMEM((3, L, lw), F32)] * 2,
        compiler_params=_params(2),
        name="attn_prompt",
    )(*([z3] * 9), biasmat)


def _attn_s_body(z_ref, c0, c1, c2, ma0, ma1, ma2, mb_ref, o_ref, n0, n1, n2):
    nq = 8
    z = z_ref[0]
    hm = _head_mask(C_HEADS * nq, C_WIDTH, nq, C_DH)
    lane = lax.broadcasted_iota(jnp.int32, (2 * C_WIDTH, C_QBLOCK), 1)
    outs, lses = [], []
    for g, (cref, mref, nref) in enumerate(((c0, ma0, n0), (c1, ma1, n1), (c2, ma2, n2))):
        W = C_PATTERNS[g][0]
        q = z[:, g * C_WIDTH:(g + 1) * C_WIDTH] * (C_DH ** -0.5)
        kn = z[:, (3 + g) * C_WIDTH:(4 + g) * C_WIDTH]
        vn = z[:, (6 + g) * C_WIDTH:(7 + g) * C_WIDTH]
        new = jnp.concatenate([jnp.zeros((C_QBLOCK - nq, 2 * C_WIDTH), F32),
                               jnp.concatenate([kn, vn], axis=1)], axis=0)
        new_t = new.T
        old = cref[0]
        sh = pltpu.roll(old, W - nq, axis=1)
        if W > C_QBLOCK:
            nref[0, :, 0:W - C_QBLOCK] = sh[:, 0:W - C_QBLOCK]
        nref[0, :, W - C_QBLOCK:W] = jnp.where(lane >= C_QBLOCK - nq, new_t,
                                               sh[:, W - C_QBLOCK:W])
        qbd = jnp.where(hm, jnp.concatenate([q] * C_HEADS, axis=0), 0.0).astype(BF16)
        ka = old[0:C_WIDTH, :].astype(BF16)
        va = old[C_WIDTH:2 * C_WIDTH, :].astype(BF16)
        kb = new_t[0:C_WIDTH, :].astype(BF16)
        vb = new_t[C_WIDTH:2 * C_WIDTH, :].astype(BF16)
        sa = jnp.dot(qbd, ka, preferred_element_type=F32) + mref[...]
        sb = jnp.dot(qbd, kb, preferred_element_type=F32) + mb_ref[g]
        mx = jnp.maximum(jnp.max(sa, axis=-1, keepdims=True), jnp.max(sb, axis=-1, keepdims=True))
        pa = jnp.exp(sa - mx)
        pb = jnp.exp(sb - mx)
        l = jnp.sum(pa, axis=-1, keepdims=True) + jnp.sum(pb, axis=-1, keepdims=True)
        r = (lax.dot_general(pa.astype(BF16), va, NT_DIMS, preferred_element_type=F32)
             + lax.dot_general(pb.astype(BF16), vb, NT_DIMS, preferred_element_type=F32))
        outs.append(r / l)
        lses.append(mx + jnp.log(l))
    mx = jnp.maximum(jnp.maximum(lses[0], lses[1]), lses[2])
    ws = [jnp.exp(ls - mx) for ls in lses]
    o32 = (ws[0] * outs[0] + ws[1] * outs[1] + ws[2] * outs[2]) / (ws[0] + ws[1] + ws[2])
    o32 = jnp.where(hm, o32, 0.0)
    o_ref[0] = o32[0:nq] + o32[nq:2 * nq] + o32[2 * nq:3 * nq] + o32[3 * nq:4 * nq]


def _attn_sample(z3, caches, mas, mb):
    nb, nq, _ = z3.shape
    kvw = 2 * C_WIDTH
    in_specs = [pl.BlockSpec((1, nq, Z_WIDTH), lambda b: (b, 0, 0))]
    out_specs = [pl.BlockSpec((1, nq, C_WIDTH), lambda b: (b, 0, 0))]
    out_shape = [jax.ShapeDtypeStruct((nb, nq, C_WIDTH), F32)]
    for W, _ in C_PATTERNS:
        in_specs.append(pl.BlockSpec((1, kvw, W), lambda b: (b, 0, 0)))
        out_specs.append(pl.BlockSpec((1, kvw, W), lambda b: (b, 0, 0)))
        out_shape.append(jax.ShapeDtypeStruct((nb, kvw, W), F32))
    for W, _ in C_PATTERNS:
        in_specs.append(pl.BlockSpec((C_HEADS * nq, W), lambda b: (0, 0)))
    in_specs.append(pl.BlockSpec((3, C_HEADS * nq, C_QBLOCK), lambda b: (0, 0, 0)))
    return pl.pallas_call(
        _attn_s_body,
        grid=(nb,),
        in_specs=in_specs,
        out_specs=out_specs,
        out_shape=out_shape,
        compiler_params=_params(1),
        name="attn_sample",
    )(z3, *caches, *mas, mb)


def _oddout_p_body(oc_ref, du_ref, dv_ref, lng_ref, lnb_ref, ws_ref, sb_ref, x_ref, w_ref,
                   post_ref, xo_ref, *, tm):
    u = _gelu(du_ref[0])
    vn = _layernorm(_gelu(dv_ref[0]), lng_ref[...], lnb_ref[...])
    lane_g = lax.broadcasted_iota(jnp.int32, (D_CHUNK, D_WIDTH), 1) >> int(math.log2(D_DH))
    parts = []
    for cidx in range(tm // D_CHUNK):
        vc = vn[cidx * D_CHUNK:(cidx + 1) * D_CHUNK, :]
        m = sb_ref[...]
        for g in range(D_WIDTH // D_DH):
            vm = jnp.where(lane_g == g, vc, 0.0).astype(BF16)
            m = m + jnp.dot(ws_ref[g], vm, preferred_element_type=F32)
        parts.append(m)
    od = u * jnp.concatenate(parts, axis=0)
    y = (jnp.dot(oc_ref[0].astype(BF16), w_ref[0:C_WIDTH, :], preferred_element_type=F32)
         + jnp.dot(od.astype(BF16), w_ref[C_WIDTH:C_WIDTH + D_WIDTH, :],
                   preferred_element_type=F32))
    xo_ref[0] = x_ref[0] + _rms(y, post_ref[...])


def _oddout_prompt(oc3, z3, x3, lng, lnb, wsm, sbe, w, post, tm):
    nb, L, _ = x3.shape
    body = functools.partial(_oddout_p_body, tm=tm)
    return pl.pallas_call(
        body,
        grid=(nb, L // tm),
        in_specs=[pl.BlockSpec((1, tm, C_WIDTH), lambda b, j: (b, j, 0)),
                  pl.BlockSpec((1, tm, D_WIDTH), lambda b, j: (b, j, 9)),
                  pl.BlockSpec((1, tm, D_WIDTH), lambda b, j: (b, j, 10)),
                  pl.BlockSpec((1, D_WIDTH), lambda b, j: (0, 0)),
                  pl.BlockSpec((1, D_WIDTH), lambda b, j: (0, 0)),
                  pl.BlockSpec((4, D_CHUNK, D_CHUNK), lambda b, j: (0, 0, 0)),
                  pl.BlockSpec((D_CHUNK, D_WIDTH), lambda b, j: (0, 0)),
                  pl.BlockSpec((1, tm, D_MODEL), lambda b, j: (b, j, 0)),
                  pl.BlockSpec((C_WIDTH + D_WIDTH, D_MODEL), lambda b, j: (0, 0)),
                  pl.BlockSpec((1, D_MODEL), lambda b, j: (0, 0))],
        out_specs=pl.BlockSpec((1, tm, D_MODEL), lambda b, j: (b, j, 0)),
        out_shape=jax.ShapeDtypeStruct((nb, L, D_MODEL), F32),
        compiler_params=_params(2),
        name="oddout_prompt",
    )(oc3, z3, z3, lng, lnb, wsm, sbe, x3, w, post)


def _oddout_s_body(oc_ref, du_ref, dv_ref, lng_ref, lnb_ref, we_ref, sb_ref, x_ref, w_ref,
                   post_ref, xo_ref, vn_ref):
    nb, nq, _ = du_ref.shape
    u = _gelu(du_ref[...])
    vn = _layernorm(_gelu(dv_ref[...]), lng_ref[...], lnb_ref[...])
    vn_ref[...] = vn
    m = jnp.zeros((nb, nq, D_WIDTH), F32) + sb_ref[...]
    for s in range(nq):
        m = m + we_ref[s] * vn[:, s:s + 1, :]
    od = (u * m).reshape(nb * nq, D_WIDTH)
    y = (jnp.dot(oc_ref[...].astype(BF16), w_ref[0:C_WIDTH, :], preferred_element_type=F32)
         + jnp.dot(od.astype(BF16), w_ref[C_WIDTH:C_WIDTH + D_WIDTH, :],
                   preferred_element_type=F32))
    xo_ref[...] = x_ref[...] + _rms(y, post_ref[...])


def _oddout_sample(oc2, z3, x2, lng, lnb, wexp, sbe, w, post):
    nb, nq, _ = z3.shape
    rows = nb * nq
    return pl.pallas_call(
        _oddout_s_body,
        grid=(1,),
        in_specs=[pl.BlockSpec((rows, C_WIDTH), lambda i: (0, 0)),
                  pl.BlockSpec((nb, nq, D_WIDTH), lambda i: (0, 0, 9)),
                  pl.BlockSpec((nb, nq, D_WIDTH), lambda i: (0, 0, 10)),
                  pl.BlockSpec((1, D_WIDTH), lambda i: (0, 0)),
                  pl.BlockSpec((1, D_WIDTH), lambda i: (0, 0)),
                  pl.BlockSpec((nq, nq, D_WIDTH), lambda i: (0, 0, 0)),
                  pl.BlockSpec((nq, D_WIDTH), lambda i: (0, 0)),
                  pl.BlockSpec((rows, D_MODEL), lambda i: (0, 0)),
                  pl.BlockSpec((C_WIDTH + D_WIDTH, D_MODEL), lambda i: (0, 0)),
                  pl.BlockSpec((1, D_MODEL), lambda i: (0, 0))],
        out_specs=[pl.BlockSpec((rows, D_MODEL), lambda i: (0, 0)),
                   pl.BlockSpec((nb, nq, D_WIDTH), lambda i: (0, 0, 0))],
        out_shape=[jax.ShapeDtypeStruct((rows, D_MODEL), F32),
                   jax.ShapeDtypeStruct((nb, nq, D_WIDTH), F32)],
        compiler_params=_params(1),
        name="oddout_sample",
    )(oc2, z3, z3, lng, lnb, wexp, sbe, x2, w, post)


def _t5_bucket(dist):
    max_exact = N_BUCKETS // 2
    d32 = jnp.maximum(dist, 1).astype(F32)
    large = max_exact + (jnp.log(d32 / max_exact) / math.log(MAX_DIST / max_exact)
                         * (N_BUCKETS - max_exact)).astype(jnp.int32)
    large = jnp.minimum(large, N_BUCKETS - 1)
    return jnp.where(dist < max_exact, dist, large)


def _step_bias(rel_bias, g):
    window, dil = C_PATTERNS[g]
    j = jnp.arange(window // dil + 1, dtype=jnp.int32)
    return rel_bias[_t5_bucket(dil * j), g * C_HEADS:(g + 1) * C_HEADS].T.astype(F32)


def _skew(v, rows):
    n = v.shape[-1]
    lead = v.shape[:-1]
    t = jnp.broadcast_to(v[..., None, :], lead + (rows, n)).reshape(lead + (rows * n,))
    return t[..., :rows * (n - 1)].reshape(lead + (rows, n - 1))


def _prompt_bias(rel_bias):
    mats = []
    for g in range(len(C_PATTERNS)):
        bj = _step_bias(rel_bias, g)
        v = jnp.concatenate([bj[:, ::-1], jnp.full((C_HEADS, C_QBLOCK), NEG, F32)], axis=1)
        mats.append(_skew(v, C_QBLOCK))
    return jnp.stack(mats, axis=0)


def _sample_bias(rel_bias, nq):
    mas, mbs = [], []
    i = np.arange(nq)[:, None]
    i2 = np.arange(nq)[None, :]
    for g, (window, dil) in enumerate(C_PATTERNS):
        bj = _step_bias(rel_bias, g)
        on_grid = jnp.asarray(np.arange(window + 1) % dil == 0)
        bd = jnp.where(on_grid[None], jnp.repeat(bj, dil, axis=1)[:, :window + 1], NEG)
        e = jnp.concatenate([bd[:, ::-1], jnp.full((C_HEADS, nq - 1), NEG, F32)], axis=1)
        ma = _skew(e, nq)[:, :, :window]
        mas.append(ma.reshape(C_HEADS * nq, window))
        dist = i - i2
        ok = jnp.asarray((dist >= 0) & (dist % dil == 0))
        mb = jnp.where(ok[None], bd[:, np.maximum(dist, 0)], NEG)
        mb = jnp.pad(mb, ((0, 0), (0, 0), (C_QBLOCK - nq, 0)), constant_values=NEG)
        mbs.append(mb.reshape(C_HEADS * nq, C_QBLOCK))
    return mas, jnp.stack(mbs, axis=0)


def _row(v):
    return v.reshape(1, -1).astype(F32)


def kernel(x_prompt, x_sample, state_gla, state_conv_b, cache_c_w128, cache_c_w512, cache_c_w2048,
           state_ffn_conv, norm_pre_mix, norm_post_mix, norm_pre_ffn, norm_post_ffn, w_in_even,
           w_gate2, b_gate, gla_norm, conv_b_w, conv_b_b, ln_b_g, ln_b_b, w_out_even, w_in_odd,
           rel_bias, sgu_ln_g, sgu_ln_b, sgu_w, sgu_b, w_out_odd, w_up, ffn_dw_w, ffn_dw_b, w_down):
    nbp, lp, d = x_prompt.shape
    nbs, ls, _ = x_sample.shape

    we = w_in_even[0]
    split = 2 * A_QK + A_WIDTH
    w_e = jnp.concatenate(
        [we[:, :split], we[:, split + A_GATE_RANK:],
         we[:, split:split + A_GATE_RANK],
         jnp.zeros((d, A_QK - A_GATE_RANK), F32)], axis=1).astype(BF16)
    wg2p = jnp.zeros((A_QK, A_QK), F32).at[:A_GATE_RANK].set(w_gate2[0]).astype(BF16)
    segb = jnp.asarray(np.kron(np.eye(A_HEADS), np.ones((A_DK, A_DV))), BF16)
    cw = jnp.zeros((CONV_HDR, B_WIDTH), F32).at[:B_CONV].set(conv_b_w[0])
    w_oe = w_out_even[0].astype(BF16)
    w_o = w_in_odd[0].astype(BF16)
    w_oo = w_out_odd[0].astype(BF16)
    w_up_b = w_up.astype(BF16)
    w_dn_b = w_down.astype(BF16)
    dw8 = jnp.zeros((2, 8, 2 * D_FF), F32).at[:, :3].set(ffn_dw_w)
    tril = jnp.tril(jnp.ones((D_CHUNK, D_CHUNK), F32))
    sgu_wm = sgu_w[0] * tril
    bias_p = _prompt_bias(rel_bias)
    mas, mbs = _sample_bias(rel_bias, ls)

    def ffn_args(x2, layer):
        return (x2, _row(norm_pre_ffn[layer]), w_up_b[layer], dw8[layer], _row(ffn_dw_b[layer]),
                w_dn_b[layer], _row(norm_post_ffn[layer]))

    def even_layer(x2, nb, L, s0t, hdr, conv_tm, nseq):
        z = _inproj(x2, _row(norm_pre_mix[0]), w_e, 512)
        z3 = z.reshape(nb, L, Z_WIDTH)
        o, st = _gla(z3, wg2p, _row(b_gate[0]), segb, s0t, nseq)
        c, ut = _convmod(z3, hdr, cw, _row(conv_b_b[0]), _row(ln_b_g[0]), _row(ln_b_b[0]), conv_tm,
                         nseq)
        x1 = _evenout(o.reshape(nb * L, A_WIDTH), z, c.reshape(nb * L, B_WIDTH), x2,
                      _row(gla_norm[0]), w_oe, _row(norm_post_mix[0]), 512)
        s_new = st.reshape(nb, A_DV, A_HEADS, A_DK).transpose(0, 2, 3, 1)
        return x1, s_new, ut

    xp2 = x_prompt.reshape(nbp * lp, d)
    x1, p_gla, ut = even_layer(xp2, nbp, lp, jnp.zeros((nbp, A_DV, A_QK), F32),
                               jnp.zeros((nbp, CONV_HDR, B_WIDTH), F32), 512, 1)
    p_conv_b = ut[:, CONV_HDR - (B_CONV - 1):]
    x2, tail0 = _ffn_prompt(*ffn_args(x1, 0), nbp, 512)
    z = _inproj(x2, _row(norm_pre_mix[1]), w_o, 512)
    z3 = z.reshape(nbp, lp, Z_WIDTH)
    oc = _attn_prompt(z3, bias_p)
    wsm = sgu_wm.astype(BF16)
    sbe = jnp.repeat(sgu_b[0].T, D_DH, axis=-1)
    x3 = _oddout_prompt(oc, z3, x2.reshape(nbp, lp, d), _row(sgu_ln_g[0]), _row(sgu_ln_b[0]),
                        wsm, sbe, w_oo, _row(norm_post_mix[1]), 512)
    x4, tail1 = _ffn_prompt(*ffn_args(x3.reshape(nbp * lp, d), 1), nbp, 512)
    y_prompt = x4.reshape(nbp, lp, d)
    p_kv = []
    for g, (window, _) in enumerate(C_PATTERNS):
        wl = min(window, lp)
        kk = z3[:, lp - wl:, (3 + g) * C_WIDTH:(4 + g) * C_WIDTH]
        vv = z3[:, lp - wl:, (6 + g) * C_WIDTH:(7 + g) * C_WIDTH]
        p_kv.append(jnp.concatenate([kk, vv], axis=-1).reshape(1, nbp, wl, 2, C_HEADS, C_DH))
    p_ffn = jnp.stack([t.reshape(nbp, -1, 8, 2 * D_FF)[:, -1, 6:8] for t in (tail0, tail1)], axis=0)

    xs2 = x_sample.reshape(nbs * ls, d)
    s0t = state_gla[0].transpose(0, 3, 1, 2).reshape(nbs, A_DV, A_QK)
    hdr = jnp.pad(state_conv_b[0], ((0, 0), (CONV_HDR - (B_CONV - 1), 0), (0, 0)))
    y1, s_gla, us = even_layer(xs2, nbs, ls, s0t, hdr, ls, SAMPLE_NSEQ)
    s_conv_b = jnp.concatenate([state_conv_b[0][:, ls:], us], axis=1)

    def prev_rows(st):
        return jnp.pad(st, ((0, 0), (0, ls - st.shape[1]), (0, 0))).reshape(nbs * ls, 2 * D_FF)

    def tail_rows(tg, tv):
        return jnp.concatenate([tg.reshape(nbs, ls, D_FF)[:, ls - 2:],
                                tv.reshape(nbs, ls, D_FF)[:, ls - 2:]], axis=-1)

    y2, sg0, sv0 = _ffn_sample(*ffn_args(y1, 0), prev_rows(state_ffn_conv[0]))
    zs = _inproj(y2, _row(norm_pre_mix[1]), w_o, 512)
    zs3 = zs.reshape(nbs, ls, Z_WIDTH)
    caches = [c[0].transpose(0, 2, 3, 4, 1).reshape(nbs, 2 * C_WIDTH, c.shape[2])
              for c in (cache_c_w128, cache_c_w512, cache_c_w2048)]
    ocs, n0, n1, n2 = _attn_sample(zs3, caches, mas, mbs)
    wexp = jnp.repeat(sgu_wm[:, :ls, :ls].transpose(2, 1, 0), D_DH, axis=-1)
    sbes = jnp.repeat(sgu_b[0][:, :ls].T, D_DH, axis=-1)
    y3, s_sgu_v = _oddout_sample(ocs.reshape(nbs * ls, C_WIDTH), zs3, y2, _row(sgu_ln_g[0]),
                                 _row(sgu_ln_b[0]), wexp, sbes, w_oo, _row(norm_post_mix[1]))
    y4, sg1, sv1 = _ffn_sample(*ffn_args(y3, 1), prev_rows(state_ffn_conv[1]))
    y_sample = y4.reshape(nbs, ls, d)
    s_kv = [n.reshape(nbs, 2, C_HEADS, C_DH, n.shape[2]).transpose(0, 4, 1, 2, 3)[None]
            for n in (n0, n1, n2)]
    s_ffn = jnp.stack([tail_rows(sg0, sv0), tail_rows(sg1, sv1)], axis=0)

    return (y_prompt, y_sample, p_gla[None], p_conv_b[None], p_kv[0], p_kv[1], p_kv[2], p_ffn,
            s_gla[None], s_conv_b[None], s_kv[0], s_kv[1], s_kv[2], s_sgu_v[None], s_ffn)
```

```python
import functools
import math

import numpy as np
import jax
import jax.numpy as jnp
from jax import lax
from jax.experimental import pallas as pl
from jax.experimental.pallas import tpu as pltpu

F32 = jnp.float32
BF16 = jnp.bfloat16

D_MODEL = 1024
EPS = 1e-6
NEG = -1e30

A_HEADS = 4
A_DK = 64
A_DV = 128
A_QK = A_HEADS * A_DK
A_WIDTH = A_HEADS * A_DV
A_GATE_RANK = 16
A_GATE_NORM = 16.0
GLA_STEP = 16
GLA_UNROLL = 4
SAMPLE_NSEQ = 8
B_WIDTH = 512
B_CONV = 31
CONV_HDR = 32
C_PATTERNS = ((128, 1), (512, 4), (2048, 16))
C_HEADS = 4
C_DH = 64
C_WIDTH = C_HEADS * C_DH
C_QBLOCK = 128
ATTN_UNROLL = 4
N_BUCKETS = 32
MAX_DIST = 2048
D_WIDTH = 256
D_DH = 64
D_CHUNK = 128
D_FF = 2816
FFN_CHUNK = 256
FFN_NCHUNK = D_FF // FFN_CHUNK
Z_WIDTH = 2816

VMEM_LIMIT_BYTES = 56 * 1024 * 1024

NT_DIMS = (((1,), (1,)), ((), ()))
TN_DIMS = (((0,), (0,)), ((), ()))


def _params(n_axes):
    return pltpu.CompilerParams(dimension_semantics=("arbitrary",) * n_axes,
                                vmem_limit_bytes=VMEM_LIMIT_BYTES)


def _rms(x, g):
    return x * lax.rsqrt(jnp.mean(x * x, axis=-1, keepdims=True) + EPS) * g


def _layernorm(x, g, b):
    mu = jnp.mean(x, axis=-1, keepdims=True)
    xc = x - mu
    var = jnp.mean(xc * xc, axis=-1, keepdims=True)
    return xc * lax.rsqrt(var + EPS) * g + b


def _sigmoid(x):
    return 1.0 / (1.0 + jnp.exp(-x))


def _head_mask(rows, lanes, rows_per_head, lanes_per_head):
    r = lax.broadcasted_iota(jnp.int32, (rows, lanes), 0) >> int(math.log2(rows_per_head))
    c = lax.broadcasted_iota(jnp.int32, (rows, lanes), 1) >> int(math.log2(lanes_per_head))
    return r == c


def _gelu(x):
    c = math.sqrt(2.0 / math.pi)
    return 0.5 * x * (1.0 + jnp.tanh(c * (x + 0.044715 * (x * x * x))))


def _inproj_body(x_ref, g_ref, w_ref, o_ref):
    h = _rms(x_ref[...], g_ref[...])
    o_ref[...] = jnp.dot(h.astype(BF16), w_ref[...], preferred_element_type=F32)


def _inproj(x2, g, w, tm):
    rows, d = x2.shape
    n = w.shape[1]
    return pl.pallas_call(
        _inproj_body,
        grid=(rows // tm,),
        in_specs=[pl.BlockSpec((tm, d), lambda i: (i, 0)),
                  pl.BlockSpec((1, d), lambda i: (0, 0)),
                  pl.BlockSpec((d, n), lambda i: (0, 0))],
        out_specs=pl.BlockSpec((tm, n), lambda i: (i, 0)),
        out_shape=jax.ShapeDtypeStruct((rows, n), F32),
        compiler_params=_params(1),
        name="inproj",
    )(x2, g, w)


def _gla_body(q_ref, k_ref, v_ref, glr_ref, wg_ref, bg_ref, segb_ref, s0_ref,
              o_ref, sn_ref, b_sc, st_sc, *, L, SB, nseq):
    glr = glr_ref[...].reshape(nseq * L, A_QK)
    xg = jnp.dot(glr.astype(BF16), wg_ref[...], preferred_element_type=F32) + bg_ref[...]
    la = (jnp.minimum(xg, 0.0) - jnp.log(1.0 + jnp.exp(-jnp.abs(xg)))) * (1.0 / A_GATE_NORM)
    tl = lax.broadcasted_iota(jnp.int32, (nseq * L, A_QK), 0) & (SB - 1)
    sh = 1
    while sh < SB:
        la = la + jnp.where(tl >= sh, pltpu.roll(la, sh, axis=0), 0.0)
        sh *= 2
    b_sc[...] = la.reshape(nseq, L, A_QK)
    for si in range(nseq):
        st_sc[si] = s0_ref[si].T

    hm = _head_mask(A_HEADS * SB, A_QK, SB, A_DK)
    tri = lax.broadcasted_iota(jnp.int32, (8, A_QK), 0)
    ntile = SB // 8

    def one(si, rows):
        q = q_ref[si, rows, :] * (A_DK ** -0.5)
        k = k_ref[si, rows, :]
        v = v_ref[si, rows, :]
        b = b_sc[si, rows, :]
        bl = b[SB - 1:SB, :]
        qt = q * jnp.exp(b)
        kh = k * jnp.exp(bl - b)
        st = st_sc[si]
        qbd = jnp.where(hm, jnp.concatenate([qt] * A_HEADS, axis=0), 0.0).astype(BF16)
        o_int = lax.dot_general(qbd, st.astype(BF16), NT_DIMS, preferred_element_type=F32)
        kbd = jnp.where(hm, jnp.concatenate([kh] * A_HEADS, axis=0), 0.0).astype(BF16)
        vst = jnp.concatenate([v[:, h * A_DV:(h + 1) * A_DV] for h in range(A_HEADS)],
                              axis=0).astype(BF16)
        upd = lax.dot_general(vst, kbd, TN_DIMS, preferred_element_type=F32)
        st_sc[si] = st * jnp.exp(bl) + upd
        ps = []
        for s in range(SB):
            lo = 8 * (s // 8)
            bs = b[s:s + 1, :]
            ks = k[s:s + 1, :]
            dd = b[lo:lo + 8, :] - bs
            if s % 8:
                dd = jnp.where(tri >= s % 8, dd, NEG)
            ps.append(q[lo:lo + 8, :] * jnp.exp(dd) * ks)
            if lo + 8 < SB:
                ps.append(q[lo + 8:SB, :] * jnp.exp(b[lo + 8:SB, :] - bs) * ks)
        pall = jnp.concatenate(ps, axis=0).astype(BF16)
        r = jnp.dot(pall, segb_ref[...], preferred_element_type=F32)
        od = [None] * ntile
        off = 0
        for s in range(SB):
            for tq in range(s // 8, ntile):
                term = r[off:off + 8, :] * v[s:s + 1, :]
                od[tq] = term if od[tq] is None else od[tq] + term
                off += 8
        o = (jnp.concatenate([o_int[h * SB:(h + 1) * SB] for h in range(A_HEADS)], axis=1)
             + jnp.concatenate(od, axis=0))
        o_ref[si, rows, :] = o

    nsteps = L // SB
    if nsteps == 1:
        for si in range(nseq):
            one(si, slice(0, SB))
    else:
        def step(i, carry):
            rows = pl.ds(pl.multiple_of(i * SB, SB), SB)
            for si in range(nseq):
                one(si, rows)
            return carry

        lax.fori_loop(0, nsteps, step, 0, unroll=GLA_UNROLL)
    for si in range(nseq):
        sn_ref[si] = st_sc[si].T


def _gla(z3, wg2p, bgate, segb, s0t, nseq):
    nb, L, _ = z3.shape
    SB = GLA_STEP if L % GLA_STEP == 0 else L
    assert nseq == 1 or SB == L
    body = functools.partial(_gla_body, L=L, SB=SB, nseq=nseq)
    return pl.pallas_call(
        body,
        grid=(nb // nseq,),
        in_specs=[pl.BlockSpec((nseq, L, A_QK), lambda b: (b, 0, 0)),
                  pl.BlockSpec((nseq, L, A_QK), lambda b: (b, 0, 1)),
                  pl.BlockSpec((nseq, L, A_WIDTH), lambda b: (b, 0, 1)),
                  pl.BlockSpec((nseq, L, A_QK), lambda b: (b, 0, 10)),
                  pl.BlockSpec((A_QK, A_QK), lambda b: (0, 0)),
                  pl.BlockSpec((1, A_QK), lambda b: (0, 0)),
                  pl.BlockSpec((A_QK, A_WIDTH), lambda b: (0, 0)),
                  pl.BlockSpec((nseq, A_QK, A_DV), lambda b: (b, 0, 0))],
        out_specs=[pl.BlockSpec((nseq, L, A_WIDTH), lambda b: (b, 0, 0)),
                   pl.BlockSpec((nseq, A_QK, A_DV), lambda b: (b, 0, 0))],
        out_shape=[jax.ShapeDtypeStruct((nb, L, A_WIDTH), F32),
                   jax.ShapeDtypeStruct((nb, A_QK, A_DV), F32)],
        scratch_shapes=[pltpu.VMEM((nseq, L, A_QK), F32), pltpu.VMEM((nseq, A_DV, A_QK), F32)],
        compiler_params=_params(1),
        name="gla",
    )(z3, z3, z3, z3, wg2p, bgate, segb, s0t)


def _conv_body(ga_ref, gg_ref, hdr_ref, cw_ref, cb_ref, lng_ref, lnb_ref,
               c_ref, ut_ref, xp, *, tm, tail, nseq):
    j = pl.program_id(1)
    n = tm + 8
    for si in range(nseq):
        xs = xp.at[si]

        @pl.when(j == 0)
        def _(xs=xs, si=si):
            xs[0:CONV_HDR, :] = hdr_ref[si]
            xs[CONV_HDR + tm:CONV_HDR + n, :] = jnp.zeros((8, B_WIDTH), F32)

        u = ga_ref[si] * _sigmoid(gg_ref[si])
        xs[CONV_HDR:CONV_HDR + tm, :] = u
        acc = None
        for r in range(7, -1, -1):
            z = None
            for q in range(5):
                kk = 8 * q + r - (CONV_HDR - (B_CONV - 1))
                if 0 <= kk < B_CONV:
                    term = cw_ref[kk:kk + 1, :] * xs[8 * q:8 * q + n, :]
                    z = term if z is None else z + term
            acc = z if acc is None else z + pltpu.roll(acc, n - 1, axis=0)
        y = _layernorm(acc[0:tm, :] + cb_ref[...], lng_ref[...], lnb_ref[...])
        c_ref[si] = y * _sigmoid(y)
        ut_ref[si] = u[tm - tail:tm, :]
        nxt = xs[tm:tm + CONV_HDR, :]
        xs[0:CONV_HDR, :] = nxt


def _convmod(z3, hdr, cw, cb, lng, lnb, tm, nseq):
    nb, L, _ = z3.shape
    assert nseq == 1 or L == tm
    tail = min(CONV_HDR, tm)
    body = functools.partial(_conv_body, tm=tm, tail=tail, nseq=nseq)
    return pl.pallas_call(
        body,
        grid=(nb // nseq, L // tm),
        in_specs=[pl.BlockSpec((nseq, tm, B_WIDTH), lambda b, j: (b, j, 3)),
                  pl.BlockSpec((nseq, tm, B_WIDTH), lambda b, j: (b, j, 4)),
                  pl.BlockSpec((nseq, CONV_HDR, B_WIDTH), lambda b, j: (b, 0, 0)),
                  pl.BlockSpec((CONV_HDR, B_WIDTH), lambda b, j: (0, 0)),
                  pl.BlockSpec((1, B_WIDTH), lambda b, j: (0, 0)),
                  pl.BlockSpec((1, B_WIDTH), lambda b, j: (0, 0)),
                  pl.BlockSpec((1, B_WIDTH), lambda b, j: (0, 0))],
        out_specs=[pl.BlockSpec((nseq, tm, B_WIDTH), lambda b, j: (b, j, 0)),
                   pl.BlockSpec((nseq, tail, B_WIDTH), lambda b, j: (b, 0, 0))],
        out_shape=[jax.ShapeDtypeStruct((nb, L, B_WIDTH), F32),
                   jax.ShapeDtypeStruct((nb, tail, B_WIDTH), F32)],
        scratch_shapes=[pltpu.VMEM((nseq, CONV_HDR + tm + 8, B_WIDTH), F32)],
        compiler_params=_params(2),
        name="convmod",
    )(z3, z3, hdr, cw, cb, lng, lnb)


def _evenout_body(o_ref, r_ref, c_ref, x_ref, gn_ref, w_ref, post_ref, xo_ref):
    o = o_ref[...]
    gn = gn_ref[...]
    oa = jnp.concatenate([_rms(o[:, h * A_DV:(h + 1) * A_DV], gn) for h in range(A_HEADS)], axis=1)
    r = r_ref[...]
    oa = oa * (r * _sigmoid(r))
    y = (jnp.dot(oa.astype(BF16), w_ref[0:A_WIDTH, :], preferred_element_type=F32)
         + jnp.dot(c_ref[...].astype(BF16), w_ref[A_WIDTH:A_WIDTH + B_WIDTH, :],
                   preferred_element_type=F32))
    xo_ref[...] = x_ref[...] + _rms(y, post_ref[...])


def _evenout(o2, z2, c2, x2, gn, w, post, tm):
    rows = x2.shape[0]
    return pl.pallas_call(
        _evenout_body,
        grid=(rows // tm,),
        in_specs=[pl.BlockSpec((tm, A_WIDTH), lambda i: (i, 0)),
                  pl.BlockSpec((tm, A_WIDTH), lambda i: (i, 2)),
                  pl.BlockSpec((tm, B_WIDTH), lambda i: (i, 0)),
                  pl.BlockSpec((tm, D_MODEL), lambda i: (i, 0)),
                  pl.BlockSpec((1, A_DV), lambda i: (0, 0)),
                  pl.BlockSpec((A_WIDTH + B_WIDTH, D_MODEL), lambda i: (0, 0)),
                  pl.BlockSpec((1, D_MODEL), lambda i: (0, 0))],
        out_specs=pl.BlockSpec((tm, D_MODEL), lambda i: (i, 0)),
        out_shape=jax.ShapeDtypeStruct((rows, D_MODEL), F32),
        compiler_params=_params(1),
        name="evenout",
    )(o2, z2, c2, x2, gn, w, post)


def _gelu_twice(x):
    c = math.sqrt(2.0 / math.pi)
    return x + x * jnp.tanh(x * (c + (c * 0.044715) * (x * x)))


def _ffn_body(*refs, tm, seq8):
    if seq8:
        (x_ref, pg_ref, wup_ref, dw_ref, b_ref, wdn_ref, post_ref, st_ref,
         xo_ref, tail_ref, act_sc, p1_sc, p2_sc, u_sc) = refs
    else:
        (x_ref, pg_ref, wup_ref, dw_ref, b_ref, wdn_ref, post_ref,
         xo_ref, tail_ref, act_sc, car) = refs
    i = pl.program_id(1)
    ns = tm // 8

    @pl.when(i == 0)
    def _():
        if seq8:
            p1_sc[...] = jnp.zeros(p1_sc.shape, F32)
            p2_sc[...] = jnp.zeros(p2_sc.shape, F32)
        else:
            car[...] = jnp.zeros((8, 2 * D_FF), F32)

    x = x_ref[...]
    h = _rms(x, pg_ref[...]).astype(BF16)
    t = lax.broadcasted_iota(jnp.int32, (tm, FFN_CHUNK), 0) & 7

    def half(cols):
        u = jnp.dot(h, wup_ref[:, cols], preferred_element_type=F32)
        if seq8:
            for hf in range(FFN_CHUNK // 128):
                lo = cols.start + 128 * hf
                c0 = slice(lo, lo + 128)
                c1 = slice(2 * D_FF + lo, 2 * D_FF + lo + 128)
                p1_sc[hf, pl.ds(0, ns, stride=8), :] = st_ref[:, c1]
                p2_sc[hf, pl.ds(0, ns, stride=8), :] = st_ref[:, c0]
                p2_sc[hf, pl.ds(1, ns, stride=8), :] = st_ref[:, c1]
                u_sc[hf] = u[:, 128 * hf:128 * (hf + 1)]
                tail_ref[:, c0] = u_sc[hf, pl.ds(6, ns, stride=8), :]
                tail_ref[:, c1] = u_sc[hf, pl.ds(7, ns, stride=8), :]
            p1 = jnp.concatenate([p1_sc[hf] for hf in range(FFN_CHUNK // 128)], axis=1)
            p2 = jnp.concatenate([p2_sc[hf] for hf in range(FFN_CHUNK // 128)], axis=1)
            u1 = jnp.where(t >= 1, pltpu.roll(u, 1, axis=0), p1)
            u2 = jnp.where(t >= 2, pltpu.roll(u, 2, axis=0), p2)
        else:
            xp = jnp.concatenate([car[:, cols], u], axis=0)
            u1 = pltpu.roll(xp, 1, axis=0)[8:, :]
            u2 = pltpu.roll(xp, 2, axis=0)[8:, :]
            last = u[tm - 8:tm, :]
            car[:, cols] = last
            tail_ref[0, :, cols] = last
        dw = dw_ref[:, cols]
        return dw[0:1, :] * u2 + dw[1:2, :] * u1 + dw[2:3, :] * u + b_ref[:, cols]

    for c in range(FFN_NCHUNK):
        g = half(slice(c * FFN_CHUNK, (c + 1) * FFN_CHUNK))
        val = half(slice(D_FF + c * FFN_CHUNK, D_FF + (c + 1) * FFN_CHUNK))
        act_sc[:, c * FFN_CHUNK:(c + 1) * FFN_CHUNK] = (_gelu_twice(g) * val).astype(BF16)
    f = jnp.dot(act_sc[...], wdn_ref[...], preferred_element_type=F32)
    xo_ref[...] = x + _rms(f, post_ref[...])


def _ffn(x2, pg, wup, dw8, dwb, wdn_half, post, nb, tm, st2=None):
    rows = x2.shape[0]
    nt = rows // (nb * tm)
    seq8 = st2 is not None
    xmap = lambda b, i: (b * nt + i, 0)
    fix = lambda b, i: (0, 0)
    once = pl.Buffered(1)
    in_specs = [pl.BlockSpec((tm, D_MODEL), xmap),
                pl.BlockSpec((1, D_MODEL), fix),
                pl.BlockSpec((D_MODEL, 2 * D_FF), fix, pipeline_mode=once),
                pl.BlockSpec((8, 2 * D_FF), fix),
                pl.BlockSpec((1, 2 * D_FF), fix),
                pl.BlockSpec((D_FF, D_MODEL), fix, pipeline_mode=once),
                pl.BlockSpec((1, D_MODEL), fix)]
    args = [x2, pg, wup, dw8, dwb, wdn_half, post]
    scratch = [pltpu.VMEM((tm, D_FF), BF16)]
    if seq8:
        ns = tm // 8
        in_specs.append(pl.BlockSpec((ns, 4 * D_FF), xmap))
        args.append(st2)
        tail_spec = pl.BlockSpec((ns, 4 * D_FF), xmap)
        tail_shape = jax.ShapeDtypeStruct((rows // 8, 4 * D_FF), F32)
        scratch += [pltpu.VMEM((FFN_CHUNK // 128, tm, 128), F32)] * 3
    else:
        tail_spec = pl.BlockSpec((1, 8, 2 * D_FF), lambda b, i: (b * nt + i, 0, 0))
        tail_shape = jax.ShapeDtypeStruct((nb * nt, 8, 2 * D_FF), F32)
        scratch.append(pltpu.VMEM((8, 2 * D_FF), F32))
    body = functools.partial(_ffn_body, tm=tm, seq8=seq8)
    return pl.pallas_call(
        body,
        grid=(nb, nt),
        in_specs=in_specs,
        out_specs=[pl.BlockSpec((tm, D_MODEL), xmap), tail_spec],
        out_shape=[jax.ShapeDtypeStruct((rows, D_MODEL), F32), tail_shape],
        scratch_shapes=scratch,
        compiler_params=_params(2),
        name="ffn_seq8" if seq8 else "ffn",
    )(*args)


def _attn_p_body(q0, k0, v0, q1, k1, v1, q2, k2, v2, bias_ref, o_ref, kv0, kv1, kv2,
                 qd, kd, vd, od, ld, og, lg, *, L):
    lane = lax.broadcasted_iota(jnp.int32, (C_QBLOCK, 2 * C_DH), 1)
    first = lane < C_DH
    col = lax.broadcasted_iota(jnp.int32, (C_QBLOCK, 2 * C_QBLOCK), 1)
    groups = ((q0, k0, v0), (q1, k1, v1), (q2, k2, v2))
    for g, (qr, kr, vr) in enumerate(groups):
        kvr = (kv0, kv1, kv2)[g]
        wl = kvr.shape[-1]
        piece = min(wl, 512)
        for c0 in range(0, wl, piece):
            src = slice(L - wl + c0, L - wl + c0 + piece)
            kvr[0, 0, :, c0:c0 + piece] = kr[0, src, :].T
            kvr[0, 1, :, c0:c0 + piece] = vr[0, src, :].T
        d = C_PATTERNS[g][1]
        nsub = L // d
        nblk = nsub // C_QBLOCK
        if d == 1:
            qs, ks, vs = qr.at[0], kr.at[0], vr.at[0]
            os_, ls_ = og.at[g], lg.at[g]
        else:
            for r in range(d):
                dst = slice(r * nsub, (r + 1) * nsub)
                qd[dst, :] = qr[0, pl.ds(r, nsub, stride=d), :]
                kd[dst, :] = kr[0, pl.ds(r, nsub, stride=d), :]
                vd[dst, :] = vr[0, pl.ds(r, nsub, stride=d), :]
            qs, ks, vs = qd, kd, vd
            os_, ls_ = od, ld

        def blk(idx, carry, qs=qs, ks=ks, vs=vs, os_=os_, ls_=ls_, nblk=nblk, g=g):
            rows = pl.ds(pl.multiple_of(idx * C_QBLOCK, C_QBLOCK), C_QBLOCK)
            qb = qs[rows, :] * (C_DH ** -0.5)
            kc = ks[rows, :]
            vc = vs[rows, :]
            m = idx & (nblk - 1)
            prow = pl.ds(pl.multiple_of(jnp.maximum(idx - 1, 0) * C_QBLOCK, C_QBLOCK), C_QBLOCK)
            kcat = jnp.concatenate([ks[prow, :], kc], axis=0).astype(BF16)
            vcat = jnp.concatenate([vs[prow, :], vc], axis=0).astype(BF16)
            dead = col < jnp.where(m == 0, C_QBLOCK, 0)
            outs, lses = [], []
            for hh in range(2):
                keep = first if hh == 0 else jnp.logical_not(first)
                qm = jnp.where(keep, qb, 0.0).astype(BF16)
                s = lax.dot_general(qm, kcat, NT_DIMS, preferred_element_type=F32)
                s = jnp.where(dead, NEG, s + bias_ref[g, hh])
                mx = jnp.max(s, axis=-1, keepdims=True)
                p = jnp.exp(s - mx)
                l = jnp.sum(p, axis=-1, keepdims=True)
                r = jnp.dot(p.astype(BF16), vcat, preferred_element_type=F32)
                outs.append(r / l)
                lses.append(mx + jnp.log(l))
            os_[rows, :] = jnp.where(first, outs[0], outs[1])
            ls_[rows, :] = jnp.where(first, lses[0], lses[1])
            return carry

        lax.fori_loop(0, d * nblk, blk, 0, unroll=ATTN_UNROLL)
        if d != 1:
            for r in range(d):
                src = slice(r * nsub, (r + 1) * nsub)
                og[g, pl.ds(r, nsub, stride=d), :] = od[src, :]
                lg[g, pl.ds(r, nsub, stride=d), :] = ld[src, :]
    cr = 256
    for cidx in range(L // cr):
        rows = slice(cidx * cr, (cidx + 1) * cr)
        l0, l1, l2 = lg[0, rows, :], lg[1, rows, :], lg[2, rows, :]
        mx = jnp.maximum(jnp.maximum(l0, l1), l2)
        w0, w1, w2 = jnp.exp(l0 - mx), jnp.exp(l1 - mx), jnp.exp(l2 - mx)
        num = w0 * og[0, rows, :] + w1 * og[1, rows, :] + w2 * og[2, rows, :]
        o_ref[0, rows, :] = num / (w0 + w1 + w2)


def _attn_prompt(z3, biasmat):
    nb, L, _ = z3.shape
    lw = 2 * C_DH
    in_specs = []
    for g in range(3):
        for part in range(3):
            base = part * 6 + g * 2
            in_specs.append(pl.BlockSpec((1, L, lw), lambda b, p, base=base: (b, 0, base + p)))
    in_specs.append(pl.BlockSpec((3, 2, C_QBLOCK, 2 * C_QBLOCK), lambda b, p: (0, p, 0, 0)))
    out_specs = [pl.BlockSpec((1, L, lw), lambda b, p: (b, 0, p))]
    out_shape = [jax.ShapeDtypeStruct((nb, L, C_WIDTH), F32)]
    for window, _ in C_PATTERNS:
        wl = min(window, L)
        out_specs.append(pl.BlockSpec((1, 2, lw, wl), lambda b, p: (b, 0, p, 0)))
        out_shape.append(jax.ShapeDtypeStruct((nb, 2, C_WIDTH, wl), F32))
    body = functools.partial(_attn_p_body, L=L)
    return pl.pallas_call(
        body,
        grid=(nb, 2),
        in_specs=in_specs,
        out_specs=out_specs,
        out_shape=out_shape,
        scratch_shapes=[pltpu.VMEM((L, lw), F32)] * 5 + [pltpu.VMEM((3, L, lw), F32)] * 2,
        compiler_params=_params(2),
        name="attn_prompt",
    )(*([z3] * 9), biasmat)


def _attn_s_body(z_ref, c0, c1, c2, ma0, ma1, ma2, mb_ref, o_ref, n0, n1, n2):
    nq = 8
    z = z_ref[0]
    hm = _head_mask(C_HEADS * nq, C_WIDTH, nq, C_DH)
    lane = lax.broadcasted_iota(jnp.int32, (2 * C_WIDTH, C_QBLOCK), 1)
    outs, lses = [], []
    for g, (cref, mref, nref) in enumerate(((c0, ma0, n0), (c1, ma1, n1), (c2, ma2, n2))):
        W = C_PATTERNS[g][0]
        q = z[:, g * C_WIDTH:(g + 1) * C_WIDTH] * (C_DH ** -0.5)
        kn = z[:, (3 + g) * C_WIDTH:(4 + g) * C_WIDTH]
        vn = z[:, (6 + g) * C_WIDTH:(7 + g) * C_WIDTH]
        new = jnp.concatenate([jnp.zeros((C_QBLOCK - nq, 2 * C_WIDTH), F32),
                               jnp.concatenate([kn, vn], axis=1)], axis=0)
        new_t = new.T
        old = cref[0]
        sh = pltpu.roll(old, W - nq, axis=1)
        if W > C_QBLOCK:
            nref[0, :, 0:W - C_QBLOCK] = sh[:, 0:W - C_QBLOCK]
        nref[0, :, W - C_QBLOCK:W] = jnp.where(lane >= C_QBLOCK - nq, new_t,
                                               sh[:, W - C_QBLOCK:W])
        qbd = jnp.where(hm, jnp.concatenate([q] * C_HEADS, axis=0), 0.0).astype(BF16)
        ka = old[0:C_WIDTH, :].astype(BF16)
        va = old[C_WIDTH:2 * C_WIDTH, :].astype(BF16)
        kb = new_t[0:C_WIDTH, :].astype(BF16)
        vb = new_t[C_WIDTH:2 * C_WIDTH, :].astype(BF16)
        sa = jnp.dot(qbd, ka, preferred_element_type=F32) + mref[...]
        sb = jnp.dot(qbd, kb, preferred_element_type=F32) + mb_ref[g]
        mx = jnp.maximum(jnp.max(sa, axis=-1, keepdims=True), jnp.max(sb, axis=-1, keepdims=True))
        pa = jnp.exp(sa - mx)
        pb = jnp.exp(sb - mx)
        l = jnp.sum(pa, axis=-1, keepdims=True) + jnp.sum(pb, axis=-1, keepdims=True)
        r = (lax.dot_general(pa.astype(BF16), va, NT_DIMS, preferred_element_type=F32)
             + lax.dot_general(pb.astype(BF16), vb, NT_DIMS, preferred_element_type=F32))
        outs.append(r / l)
        lses.append(mx + jnp.log(l))
    mx = jnp.maximum(jnp.maximum(lses[0], lses[1]), lses[2])
    ws = [jnp.exp(ls - mx) for ls in lses]
    o32 = (ws[0] * outs[0] + ws[1] * outs[1] + ws[2] * outs[2]) / (ws[0] + ws[1] + ws[2])
    o32 = jnp.where(hm, o32, 0.0)
    o_ref[0] = o32[0:nq] + o32[nq:2 * nq] + o32[2 * nq:3 * nq] + o32[3 * nq:4 * nq]


def _attn_sample(z3, caches, mas, mb):
    nb, nq, _ = z3.shape
    kvw = 2 * C_WIDTH
    in_specs = [pl.BlockSpec((1, nq, Z_WIDTH), lambda b: (b, 0, 0))]
    out_specs = [pl.BlockSpec((1, nq, C_WIDTH), lambda b: (b, 0, 0))]
    out_shape = [jax.ShapeDtypeStruct((nb, nq, C_WIDTH), F32)]
    for W, _ in C_PATTERNS:
        in_specs.append(pl.BlockSpec((1, kvw, W), lambda b: (b, 0, 0)))
        out_specs.append(pl.BlockSpec((1, kvw, W), lambda b: (b, 0, 0)))
        out_shape.append(jax.ShapeDtypeStruct((nb, kvw, W), F32))
    for W, _ in C_PATTERNS:
        in_specs.append(pl.BlockSpec((C_HEADS * nq, W), lambda b: (0, 0)))
    in_specs.append(pl.BlockSpec((3, C_HEADS * nq, C_QBLOCK), lambda b: (0, 0, 0)))
    return pl.pallas_call(
        _attn_s_body,
        grid=(nb,),
        in_specs=in_specs,
        out_specs=out_specs,
        out_shape=out_shape,
        compiler_params=_params(1),
        name="attn_sample",
    )(z3, *caches, *mas, mb)


def _oddout_p_body(oc_ref, du_ref, dv_ref, lng_ref, lnb_ref, ws_ref, sb_ref, x_ref, w_ref,
                   post_ref, xo_ref, *, tm):
    u = _gelu(du_ref[0])
    vn = _layernorm(_gelu(dv_ref[0]), lng_ref[...], lnb_ref[...])
    lane_g = lax.broadcasted_iota(jnp.int32, (D_CHUNK, D_WIDTH), 1) >> int(math.log2(D_DH))
    parts = []
    for cidx in range(tm // D_CHUNK):
        vc = vn[cidx * D_CHUNK:(cidx + 1) * D_CHUNK, :]
        m = sb_ref[...]
        for g in range(D_WIDTH // D_DH):
            vm = jnp.where(lane_g == g, vc, 0.0).astype(BF16)
            m = m + jnp.dot(ws_ref[g], vm, preferred_element_type=F32)
        parts.append(m)
    od = u * jnp.concatenate(parts, axis=0)
    y = (jnp.dot(oc_ref[0].astype(BF16), w_ref[0:C_WIDTH, :], preferred_element_type=F32)
         + jnp.dot(od.astype(BF16), w_ref[C_WIDTH:C_WIDTH + D_WIDTH, :],
                   preferred_element_type=F32))
    xo_ref[0] = x_ref[0] + _rms(y, post_ref[...])


def _oddout_prompt(oc3, z3, x3, lng, lnb, wsm, sbe, w, post, tm):
    nb, L, _ = x3.shape
    body = functools.partial(_oddout_p_body, tm=tm)
    return pl.pallas_call(
        body,
        grid=(nb, L // tm),
        in_specs=[pl.BlockSpec((1, tm, C_WIDTH), lambda b, j: (b, j, 0)),
                  pl.BlockSpec((1, tm, D_WIDTH), lambda b, j: (b, j, 9)),
                  pl.BlockSpec((1, tm, D_WIDTH), lambda b, j: (b, j, 10)),
                  pl.BlockSpec((1, D_WIDTH), lambda b, j: (0, 0)),
                  pl.BlockSpec((1, D_WIDTH), lambda b, j: (0, 0)),
                  pl.BlockSpec((4, D_CHUNK, D_CHUNK), lambda b, j: (0, 0, 0)),
                  pl.BlockSpec((D_CHUNK, D_WIDTH), lambda b, j: (0, 0)),
                  pl.BlockSpec((1, tm, D_MODEL), lambda b, j: (b, j, 0)),
                  pl.BlockSpec((C_WIDTH + D_WIDTH, D_MODEL), lambda b, j: (0, 0)),
                  pl.BlockSpec((1, D_MODEL), lambda b, j: (0, 0))],
        out_specs=pl.BlockSpec((1, tm, D_MODEL), lambda b, j: (b, j, 0)),
        out_shape=jax.ShapeDtypeStruct((nb, L, D_MODEL), F32),
        compiler_params=_params(2),
        name="oddout_prompt",
    )(oc3, z3, z3, lng, lnb, wsm, sbe, x3, w, post)


def _oddout_s_body(oc_ref, du_ref, dv_ref, lng_ref, lnb_ref, we_ref, sb_ref, x_ref, w_ref,
                   post_ref, xo_ref, vn_ref):
    nb, nq, _ = du_ref.shape
    u = _gelu(du_ref[...])
    vn = _layernorm(_gelu(dv_ref[...]), lng_ref[...], lnb_ref[...])
    vn_ref[...] = vn
    m = jnp.zeros((nb, nq, D_WIDTH), F32) + sb_ref[...]
    for s in range(nq):
        m = m + we_ref[s] * vn[:, s:s + 1, :]
    od = (u * m).reshape(nb * nq, D_WIDTH)
    y = (jnp.dot(oc_ref[...].astype(BF16), w_ref[0:C_WIDTH, :], preferred_element_type=F32)
         + jnp.dot(od.astype(BF16), w_ref[C_WIDTH:C_WIDTH + D_WIDTH, :],
                   preferred_element_type=F32))
    xo_ref[...] = x_ref[...] + _rms(y, post_ref[...])


def _oddout_sample(oc2, z3, x2, lng, lnb, wexp, sbe, w, post):
    nb, nq, _ = z3.shape
    rows = nb * nq
    return pl.pallas_call(
        _oddout_s_body,
        grid=(1,),
        in_specs=[pl.BlockSpec((rows, C_WIDTH), lambda i: (0, 0)),
                  pl.BlockSpec((nb, nq, D_WIDTH), lambda i: (0, 0, 9)),
                  pl.BlockSpec((nb, nq, D_WIDTH), lambda i: (0, 0, 10)),
                  pl.BlockSpec((1, D_WIDTH), lambda i: (0, 0)),
                  pl.BlockSpec((1, D_WIDTH), lambda i: (0, 0)),
                  pl.BlockSpec((nq, nq, D_WIDTH), lambda i: (0, 0, 0)),
                  pl.BlockSpec((nq, D_WIDTH), lambda i: (0, 0)),
                  pl.BlockSpec((rows, D_MODEL), lambda i: (0, 0)),
                  pl.BlockSpec((C_WIDTH + D_WIDTH, D_MODEL), lambda i: (0, 0)),
                  pl.BlockSpec((1, D_MODEL), lambda i: (0, 0))],
        out_specs=[pl.BlockSpec((rows, D_MODEL), lambda i: (0, 0)),
                   pl.BlockSpec((nb, nq, D_WIDTH), lambda i: (0, 0, 0))],
        out_shape=[jax.ShapeDtypeStruct((rows, D_MODEL), F32),
                   jax.ShapeDtypeStruct((nb, nq, D_WIDTH), F32)],
        compiler_params=_params(1),
        name="oddout_sample",
    )(oc2, z3, z3, lng, lnb, wexp, sbe, x2, w, post)


def _t5_bucket(dist):
    max_exact = N_BUCKETS // 2
    d32 = jnp.maximum(dist, 1).astype(F32)
    large = max_exact + (jnp.log(d32 / max_exact) / math.log(MAX_DIST / max_exact)
                         * (N_BUCKETS - max_exact)).astype(jnp.int32)
    large = jnp.minimum(large, N_BUCKETS - 1)
    return jnp.where(dist < max_exact, dist, large)


def _step_bias(rel_bias, g):
    window, dil = C_PATTERNS[g]
    j = jnp.arange(window // dil + 1, dtype=jnp.int32)
    return rel_bias[_t5_bucket(dil * j), g * C_HEADS:(g + 1) * C_HEADS].T.astype(F32)


def _skew(v, rows):
    n = v.shape[-1]
    lead = v.shape[:-1]
    t = jnp.broadcast_to(v[..., None, :], lead + (rows, n)).reshape(lead + (rows * n,))
    return t[..., :rows * (n - 1)].reshape(lead + (rows, n - 1))


def _prompt_bias(rel_bias):
    mats = []
    for g in range(len(C_PATTERNS)):
        bj = _step_bias(rel_bias, g)
        v = jnp.concatenate([bj[:, ::-1], jnp.full((C_HEADS, C_QBLOCK), NEG, F32)], axis=1)
        mats.append(_skew(v, C_QBLOCK))
    return jnp.stack(mats, axis=0)


def _sample_bias(rel_bias, nq):
    mas, mbs = [], []
    i = np.arange(nq)[:, None]
    i2 = np.arange(nq)[None, :]
    for g, (window, dil) in enumerate(C_PATTERNS):
        bj = _step_bias(rel_bias, g)
        on_grid = jnp.asarray(np.arange(window + 1) % dil == 0)
        bd = jnp.where(on_grid[None], jnp.repeat(bj, dil, axis=1)[:, :window + 1], NEG)
        e = jnp.concatenate([bd[:, ::-1], jnp.full((C_HEADS, nq - 1), NEG, F32)], axis=1)
        ma = _skew(e, nq)[:, :, :window]
        mas.append(ma.reshape(C_HEADS * nq, window))
        dist = i - i2
        ok = jnp.asarray((dist >= 0) & (dist % dil == 0))
        mb = jnp.where(ok[None], bd[:, np.maximum(dist, 0)], NEG)
        mb = jnp.pad(mb, ((0, 0), (0, 0), (C_QBLOCK - nq, 0)), constant_values=NEG)
        mbs.append(mb.reshape(C_HEADS * nq, C_QBLOCK))
    return mas, jnp.stack(mbs, axis=0)


def _row(v):
    return v.reshape(1, -1).astype(F32)


def kernel(x_prompt, x_sample, state_gla, state_conv_b, cache_c_w128, cache_c_w512, cache_c_w2048,
           state_ffn_conv, norm_pre_mix, norm_post_mix, norm_pre_ffn, norm_post_ffn, w_in_even,
           w_gate2, b_gate, gla_norm, conv_b_w, conv_b_b, ln_b_g, ln_b_b, w_out_even, w_in_odd,
           rel_bias, sgu_ln_g, sgu_ln_b, sgu_w, sgu_b, w_out_odd, w_up, ffn_dw_w, ffn_dw_b, w_down):
    nbp, lp, d = x_prompt.shape
    nbs, ls, _ = x_sample.shape

    we = w_in_even[0]
    split = 2 * A_QK + A_WIDTH
    w_e = jnp.concatenate(
        [we[:, :split], we[:, split + A_GATE_RANK:],
         we[:, split:split + A_GATE_RANK],
         jnp.zeros((d, A_QK - A_GATE_RANK), F32)], axis=1).astype(BF16)
    wg2p = jnp.zeros((A_QK, A_QK), F32).at[:A_GATE_RANK].set(w_gate2[0]).astype(BF16)
    segb = jnp.asarray(np.kron(np.eye(A_HEADS), np.ones((A_DK, A_DV))), BF16)
    cw = jnp.zeros((CONV_HDR, B_WIDTH), F32).at[:B_CONV].set(conv_b_w[0])
    w_oe = w_out_even[0].astype(BF16)
    w_o = w_in_odd[0].astype(BF16)
    w_oo = w_out_odd[0].astype(BF16)
    w_up_b = w_up.astype(BF16)
    w_dn_b = (0.5 * w_down).astype(BF16)
    dw8 = jnp.zeros((2, 8, 2 * D_FF), F32).at[:, :3].set(ffn_dw_w)
    tril = jnp.tril(jnp.ones((D_CHUNK, D_CHUNK), F32))
    sgu_wm = sgu_w[0] * tril
    bias_p = _prompt_bias(rel_bias)
    mas, mbs = _sample_bias(rel_bias, ls)

    def ffn_args(x2, layer):
        return (x2, _row(norm_pre_ffn[layer]), w_up_b[layer], dw8[layer], _row(ffn_dw_b[layer]),
                w_dn_b[layer], _row(norm_post_ffn[layer]))

    def even_layer(x2, nb, L, s0t, hdr, conv_tm, nseq):
        z = _inproj(x2, _row(norm_pre_mix[0]), w_e, 512)
        z3 = z.reshape(nb, L, Z_WIDTH)
        o, st = _gla(z3, wg2p, _row(b_gate[0]), segb, s0t, nseq)
        c, ut = _convmod(z3, hdr, cw, _row(conv_b_b[0]), _row(ln_b_g[0]), _row(ln_b_b[0]), conv_tm,
                         nseq)
        x1 = _evenout(o.reshape(nb * L, A_WIDTH), z, c.reshape(nb * L, B_WIDTH), x2,
                      _row(gla_norm[0]), w_oe, _row(norm_post_mix[0]), 512)
        return x1, st.reshape(nb, A_HEADS, A_DK, A_DV), ut

    def kv_window(t):
        return t.reshape(t.shape[0], 2, C_HEADS, C_DH, t.shape[-1]).transpose(0, 4, 1, 2, 3)[None]

    xp2 = x_prompt.reshape(nbp * lp, d)
    x1, p_gla, ut = even_layer(xp2, nbp, lp, jnp.zeros((nbp, A_QK, A_DV), F32),
                               jnp.zeros((nbp, CONV_HDR, B_WIDTH), F32), 512, 1)
    p_conv_b = ut[:, CONV_HDR - (B_CONV - 1):]
    x2, tail0 = _ffn(*ffn_args(x1, 0), nbp, 512)
    z = _inproj(x2, _row(norm_pre_mix[1]), w_o, 512)
    z3 = z.reshape(nbp, lp, Z_WIDTH)
    oc, *p_kvt = _attn_prompt(z3, bias_p)
    wsm = sgu_wm.astype(BF16)
    sbe = jnp.repeat(sgu_b[0].T, D_DH, axis=-1)
    x3 = _oddout_prompt(oc, z3, x2.reshape(nbp, lp, d), _row(sgu_ln_g[0]), _row(sgu_ln_b[0]),
                        wsm, sbe, w_oo, _row(norm_post_mix[1]), 512)
    x4, tail1 = _ffn(*ffn_args(x3.reshape(nbp * lp, d), 1), nbp, 512)
    y_prompt = x4.reshape(nbp, lp, d)
    p_kv = [kv_window(t) for t in p_kvt]
    p_ffn = jnp.stack([t.reshape(nbp, -1, 8, 2 * D_FF)[:, -1, 6:8] for t in (tail0, tail1)], axis=0)

    xs2 = x_sample.reshape(nbs * ls, d)
    hdr = jnp.pad(state_conv_b[0], ((0, 0), (CONV_HDR - (B_CONV - 1), 0), (0, 0)))
    y1, s_gla, us = even_layer(xs2, nbs, ls, state_gla[0].reshape(nbs, A_QK, A_DV), hdr, ls,
                               SAMPLE_NSEQ)
    s_conv_b = jnp.concatenate([state_conv_b[0][:, ls:], us], axis=1)

    def ffn_sample(x2, layer):
        st2 = state_ffn_conv[layer].reshape(nbs, 4 * D_FF)
        xo, tail = _ffn(*ffn_args(x2, layer), 1, 512, st2)
        return xo, tail.reshape(nbs, 2, 2 * D_FF)

    y2, s_ffn0 = ffn_sample(y1, 0)
    zs = _inproj(y2, _row(norm_pre_mix[1]), w_o, 512)
    zs3 = zs.reshape(nbs, ls, Z_WIDTH)
    caches = [c[0].transpose(0, 2, 3, 4, 1).reshape(nbs, 2 * C_WIDTH, c.shape[2])
              for c in (cache_c_w128, cache_c_w512, cache_c_w2048)]
    ocs, n0, n1, n2 = _attn_sample(zs3, caches, mas, mbs)
    wexp = jnp.repeat(sgu_wm[:, :ls, :ls].transpose(2, 1, 0), D_DH, axis=-1)
    sbes = jnp.repeat(sgu_b[0][:, :ls].T, D_DH, axis=-1)
    y3, s_sgu_v = _oddout_sample(ocs.reshape(nbs * ls, C_WIDTH), zs3, y2, _row(sgu_ln_g[0]),
                                 _row(sgu_ln_b[0]), wexp, sbes, w_oo, _row(norm_post_mix[1]))
    y4, s_ffn1 = ffn_sample(y3, 1)
    y_sample = y4.reshape(nbs, ls, d)
    s_kv = [kv_window(n.reshape(nbs, 2, C_WIDTH, n.shape[2])) for n in (n0, n1, n2)]
    s_ffn = jnp.stack([s_ffn0, s_ffn1], axis=0)

    return (y_prompt, y_sample, p_gla[None], p_conv_b[None], p_kv[0], p_kv[1], p_kv[2], p_ffn,
            s_gla[None], s_conv_b[None], s_kv[0], s_kv[1], s_kv[2], s_sgu_v[None], s_ffn)
```

```python
import functools
import math

import numpy as np
import jax
import jax.numpy as jnp
from jax import lax
from jax.experimental import pallas as pl
from jax.experimental.pallas import tpu as pltpu

F32 = jnp.float32
BF16 = jnp.bfloat16

D_MODEL = 1024
EPS = 1e-6
NEG = -1e30

A_HEADS = 4
A_DK = 64
A_DV = 128
A_QK = A_HEADS * A_DK
A_WIDTH = A_HEADS * A_DV
A_GATE_RANK = 16
A_GATE_NORM = 16.0
GLA_STEP = 16
GLA_UNROLL = 4
SAMPLE_NSEQ = 8
B_WIDTH = 512
B_CONV = 31
CONV_HDR = 32
C_PATTERNS = ((128, 1), (512, 4), (2048, 16))
C_HEADS = 4
C_DH = 64
C_WIDTH = C_HEADS * C_DH
C_QBLOCK = 128
ATTN_UNROLL = 4
N_BUCKETS = 32
MAX_DIST = 2048
D_WIDTH = 256
D_DH = 64
D_CHUNK = 128
D_FF = 2816
FFN_CHUNK = 256
FFN_NCHUNK = D_FF // FFN_CHUNK
Z_WIDTH = 2816

VMEM_LIMIT_BYTES = 56 * 1024 * 1024

NT_DIMS = (((1,), (1,)), ((), ()))
TN_DIMS = (((0,), (0,)), ((), ()))


def _params(n_axes):
    return pltpu.CompilerParams(dimension_semantics=("arbitrary",) * n_axes,
                                vmem_limit_bytes=VMEM_LIMIT_BYTES)


def _rms(x, g):
    return x * lax.rsqrt(jnp.mean(x * x, axis=-1, keepdims=True) + EPS) * g


def _layernorm(x, g, b):
    mu = jnp.mean(x, axis=-1, keepdims=True)
    xc = x - mu
    var = jnp.mean(xc * xc, axis=-1, keepdims=True)
    return xc * lax.rsqrt(var + EPS) * g + b


def _sigmoid(x):
    return 1.0 / (1.0 + jnp.exp(-x))


def _head_mask(rows, lanes, rows_per_head, lanes_per_head):
    r = lax.broadcasted_iota(jnp.int32, (rows, lanes), 0) >> int(math.log2(rows_per_head))
    c = lax.broadcasted_iota(jnp.int32, (rows, lanes), 1) >> int(math.log2(lanes_per_head))
    return r == c


def _gelu(x):
    c = math.sqrt(2.0 / math.pi)
    return 0.5 * x * (1.0 + jnp.tanh(c * (x + 0.044715 * (x * x * x))))


def _inproj_body(x_ref, g_ref, w_ref, o_ref):
    h = _rms(x_ref[...], g_ref[...])
    o_ref[...] = jnp.dot(h.astype(BF16), w_ref[...], preferred_element_type=F32)


def _inproj(x2, g, w, tm):
    rows, d = x2.shape
    n = w.shape[1]
    return pl.pallas_call(
        _inproj_body,
        grid=(rows // tm,),
        in_specs=[pl.BlockSpec((tm, d), lambda i: (i, 0)),
                  pl.BlockSpec((1, d), lambda i: (0, 0)),
                  pl.BlockSpec((d, n), lambda i: (0, 0))],
        out_specs=pl.BlockSpec((tm, n), lambda i: (i, 0)),
        out_shape=jax.ShapeDtypeStruct((rows, n), F32),
        compiler_params=_params(1),
        name="inproj",
    )(x2, g, w)


def _gla_body(q_ref, k_ref, v_ref, glr_ref, wg_ref, bg_ref, segb_ref, s0_ref,
              o_ref, sn_ref, b_sc, st_sc, *, L, SB, nseq):
    glr = glr_ref[...].reshape(nseq * L, A_QK)
    xg = jnp.dot(glr.astype(BF16), wg_ref[...], preferred_element_type=F32) + bg_ref[...]
    la = (jnp.minimum(xg, 0.0) - jnp.log(1.0 + jnp.exp(-jnp.abs(xg)))) * (1.0 / A_GATE_NORM)
    tl = lax.broadcasted_iota(jnp.int32, (nseq * L, A_QK), 0) & (SB - 1)
    sh = 1
    while sh < SB:
        la = la + jnp.where(tl >= sh, pltpu.roll(la, sh, axis=0), 0.0)
        sh *= 2
    b_sc[...] = la.reshape(nseq, L, A_QK)
    for si in range(nseq):
        st_sc[si] = s0_ref[si].T

    hm = _head_mask(A_HEADS * SB, A_QK, SB, A_DK)
    tri = lax.broadcasted_iota(jnp.int32, (8, A_QK), 0)
    ntile = SB // 8

    def one(si, rows):
        q = q_ref[si, rows, :] * (A_DK ** -0.5)
        k = k_ref[si, rows, :]
        v = v_ref[si, rows, :]
        b = b_sc[si, rows, :]
        bl = b[SB - 1:SB, :]
        qt = q * jnp.exp(b)
        kh = k * jnp.exp(bl - b)
        st = st_sc[si]
        qbd = jnp.where(hm, jnp.concatenate([qt] * A_HEADS, axis=0), 0.0).astype(BF16)
        o_int = lax.dot_general(qbd, st.astype(BF16), NT_DIMS, preferred_element_type=F32)
        kbd = jnp.where(hm, jnp.concatenate([kh] * A_HEADS, axis=0), 0.0).astype(BF16)
        vst = jnp.concatenate([v[:, h * A_DV:(h + 1) * A_DV] for h in range(A_HEADS)],
                              axis=0).astype(BF16)
        upd = lax.dot_general(vst, kbd, TN_DIMS, preferred_element_type=F32)
        st_sc[si] = st * jnp.exp(bl) + upd
        ps = []
        for s in range(SB):
            lo = 8 * (s // 8)
            bs = b[s:s + 1, :]
            ks = k[s:s + 1, :]
            dd = b[lo:lo + 8, :] - bs
            if s % 8:
                dd = jnp.where(tri >= s % 8, dd, NEG)
            ps.append(q[lo:lo + 8, :] * jnp.exp(dd) * ks)
            if lo + 8 < SB:
                ps.append(q[lo + 8:SB, :] * jnp.exp(b[lo + 8:SB, :] - bs) * ks)
        pall = jnp.concatenate(ps, axis=0).astype(BF16)
        r = jnp.dot(pall, segb_ref[...], preferred_element_type=F32)
        od = [None] * ntile
        off = 0
        for s in range(SB):
            for tq in range(s // 8, ntile):
                term = r[off:off + 8, :] * v[s:s + 1, :]
                od[tq] = term if od[tq] is None else od[tq] + term
                off += 8
        o = (jnp.concatenate([o_int[h * SB:(h + 1) * SB] for h in range(A_HEADS)], axis=1)
             + jnp.concatenate(od, axis=0))
        o_ref[si, rows, :] = o

    nsteps = L // SB
    if nsteps == 1:
        for si in range(nseq):
            one(si, slice(0, SB))
    else:
        def step(i, carry):
            rows = pl.ds(pl.multiple_of(i * SB, SB), SB)
            for si in range(nseq):
                one(si, rows)
            return carry

        lax.fori_loop(0, nsteps, step, 0, unroll=GLA_UNROLL)
    for si in range(nseq):
        sn_ref[si] = st_sc[si].T


def _gla(z3, wg2p, bgate, segb, s0t, nseq):
    nb, L, _ = z3.shape
    SB = GLA_STEP if L % GLA_STEP == 0 else L
    assert nseq == 1 or SB == L
    body = functools.partial(_gla_body, L=L, SB=SB, nseq=nseq)
    return pl.pallas_call(
        body,
        grid=(nb // nseq,),
        in_specs=[pl.BlockSpec((nseq, L, A_QK), lambda b: (b, 0, 0)),
                  pl.BlockSpec((nseq, L, A_QK), lambda b: (b, 0, 1)),
                  pl.BlockSpec((nseq, L, A_WIDTH), lambda b: (b, 0, 1)),
                  pl.BlockSpec((nseq, L, A_QK), lambda b: (b, 0, 10)),
                  pl.BlockSpec((A_QK, A_QK), lambda b: (0, 0)),
                  pl.BlockSpec((1, A_QK), lambda b: (0, 0)),
                  pl.BlockSpec((A_QK, A_WIDTH), lambda b: (0, 0)),
                  pl.BlockSpec((nseq, A_QK, A_DV), lambda b: (b, 0, 0))],
        out_specs=[pl.BlockSpec((nseq, L, A_WIDTH), lambda b: (b, 0, 0)),
                   pl.BlockSpec((nseq, A_QK, A_DV), lambda b: (b, 0, 0))],
        out_shape=[jax.ShapeDtypeStruct((nb, L, A_WIDTH), F32),
                   jax.ShapeDtypeStruct((nb, A_QK, A_DV), F32)],
        scratch_shapes=[pltpu.VMEM((nseq, L, A_QK), F32), pltpu.VMEM((nseq, A_DV, A_QK), F32)],
        compiler_params=_params(1),
        name="gla",
    )(z3, z3, z3, z3, wg2p, bgate, segb, s0t)


def _conv_body(ga_ref, gg_ref, hdr_ref, cw_ref, cb_ref, lng_ref, lnb_ref,
               c_ref, ut_ref, xp, *, tm, tail, nseq):
    j = pl.program_id(1)
    n = tm + 8
    for si in range(nseq):
        xs = xp.at[si]

        @pl.when(j == 0)
        def _(xs=xs, si=si):
            xs[0:CONV_HDR, :] = hdr_ref[si]
            xs[CONV_HDR + tm:CONV_HDR + n, :] = jnp.zeros((8, B_WIDTH), F32)

        u = ga_ref[si] * _sigmoid(gg_ref[si])
        xs[CONV_HDR:CONV_HDR + tm, :] = u
        acc = None
        for r in range(7, -1, -1):
            z = None
            for q in range(5):
                kk = 8 * q + r - (CONV_HDR - (B_CONV - 1))
                if 0 <= kk < B_CONV:
                    term = cw_ref[kk:kk + 1, :] * xs[8 * q:8 * q + n, :]
                    z = term if z is None else z + term
            acc = z if acc is None else z + pltpu.roll(acc, n - 1, axis=0)
        y = _layernorm(acc[0:tm, :] + cb_ref[...], lng_ref[...], lnb_ref[...])
        c_ref[si] = y * _sigmoid(y)
        ut_ref[si] = u[tm - tail:tm, :]
        nxt = xs[tm:tm + CONV_HDR, :]
        xs[0:CONV_HDR, :] = nxt


def _convmod(z3, hdr, cw, cb, lng, lnb, tm, nseq):
    nb, L, _ = z3.shape
    assert nseq == 1 or L == tm
    tail = min(CONV_HDR, tm)
    body = functools.partial(_conv_body, tm=tm, tail=tail, nseq=nseq)
    return pl.pallas_call(
        body,
        grid=(nb // nseq, L // tm),
        in_specs=[pl.BlockSpec((nseq, tm, B_WIDTH), lambda b, j: (b, j, 3)),
                  pl.BlockSpec((nseq, tm, B_WIDTH), lambda b, j: (b, j, 4)),
                  pl.BlockSpec((nseq, CONV_HDR, B_WIDTH), lambda b, j: (b, 0, 0)),
                  pl.BlockSpec((CONV_HDR, B_WIDTH), lambda b, j: (0, 0)),
                  pl.BlockSpec((1, B_WIDTH), lambda b, j: (0, 0)),
                  pl.BlockSpec((1, B_WIDTH), lambda b, j: (0, 0)),
                  pl.BlockSpec((1, B_WIDTH), lambda b, j: (0, 0))],
        out_specs=[pl.BlockSpec((nseq, tm, B_WIDTH), lambda b, j: (b, j, 0)),
                   pl.BlockSpec((nseq, tail, B_WIDTH), lambda b, j: (b, 0, 0))],
        out_shape=[jax.ShapeDtypeStruct((nb, L, B_WIDTH), F32),
                   jax.ShapeDtypeStruct((nb, tail, B_WIDTH), F32)],
        scratch_shapes=[pltpu.VMEM((nseq, CONV_HDR + tm + 8, B_WIDTH), F32)],
        compiler_params=_params(2),
        name="convmod",
    )(z3, z3, hdr, cw, cb, lng, lnb)


def _even_mix(o_ref, r_ref, c_ref, x_ref, gn_ref, w_ref, post_ref):
    o = o_ref[...]
    gn = gn_ref[...]
    oa = jnp.concatenate([_rms(o[:, h * A_DV:(h + 1) * A_DV], gn) for h in range(A_HEADS)], axis=1)
    r = r_ref[...]
    oa = oa * (r * _sigmoid(r))
    y = (jnp.dot(oa.astype(BF16), w_ref[0:A_WIDTH, :], preferred_element_type=F32)
         + jnp.dot(c_ref[...].astype(BF16), w_ref[A_WIDTH:A_WIDTH + B_WIDTH, :],
                   preferred_element_type=F32))
    return x_ref[...] + _rms(y, post_ref[...])


def _gelu_twice(x):
    c = math.sqrt(2.0 / math.pi)
    return x + x * jnp.tanh(x * (c + (c * 0.044715) * (x * x)))


N_MIX_REFS = {None: 1, "even": 7, "odd": 10}


def _ffn_body(*refs, tm, seq8, mix):
    head, refs = refs[:N_MIX_REFS[mix]], refs[N_MIX_REFS[mix]:]
    if seq8:
        (pg_ref, wup_ref, dw_ref, b_ref, wdn_ref, post_ref, st_ref,
         xo_ref, tail_ref, act_sc, p1_sc, p2_sc, u_sc) = refs
    else:
        (pg_ref, wup_ref, dw_ref, b_ref, wdn_ref, post_ref,
         xo_ref, tail_ref, act_sc, car) = refs
    i = pl.program_id(1)
    ns = tm // 8

    @pl.when(i == 0)
    def _():
        if seq8:
            p1_sc[...] = jnp.zeros(p1_sc.shape, F32)
            p2_sc[...] = jnp.zeros(p2_sc.shape, F32)
        else:
            car[...] = jnp.zeros((8, 2 * D_FF), F32)

    if mix == "even":
        x = _even_mix(*head)
    elif mix == "odd":
        x = _odd_mix(*head)
    else:
        x = head[0][...]
    h = _rms(x, pg_ref[...]).astype(BF16)
    t = lax.broadcasted_iota(jnp.int32, (tm, FFN_CHUNK), 0) & 7

    def half(cols):
        u = jnp.dot(h, wup_ref[:, cols], preferred_element_type=F32)
        if seq8:
            for hf in range(FFN_CHUNK // 128):
                lo = cols.start + 128 * hf
                c0 = slice(lo, lo + 128)
                c1 = slice(2 * D_FF + lo, 2 * D_FF + lo + 128)
                p1_sc[hf, pl.ds(0, ns, stride=8), :] = st_ref[:, c1]
                p2_sc[hf, pl.ds(0, ns, stride=8), :] = st_ref[:, c0]
                p2_sc[hf, pl.ds(1, ns, stride=8), :] = st_ref[:, c1]
                u_sc[hf] = u[:, 128 * hf:128 * (hf + 1)]
                tail_ref[:, c0] = u_sc[hf, pl.ds(6, ns, stride=8), :]
                tail_ref[:, c1] = u_sc[hf, pl.ds(7, ns, stride=8), :]
            p1 = jnp.concatenate([p1_sc[hf] for hf in range(FFN_CHUNK // 128)], axis=1)
            p2 = jnp.concatenate([p2_sc[hf] for hf in range(FFN_CHUNK // 128)], axis=1)
            u1 = jnp.where(t >= 1, pltpu.roll(u, 1, axis=0), p1)
            u2 = jnp.where(t >= 2, pltpu.roll(u, 2, axis=0), p2)
        else:
            xp = jnp.concatenate([car[:, cols], u], axis=0)
            u1 = pltpu.roll(xp, 1, axis=0)[8:, :]
            u2 = pltpu.roll(xp, 2, axis=0)[8:, :]
            last = u[tm - 8:tm, :]
            car[:, cols] = last
            tail_ref[0, :, cols] = last
        dw = dw_ref[:, cols]
        return dw[0:1, :] * u2 + dw[1:2, :] * u1 + dw[2:3, :] * u + b_ref[:, cols]

    for c in range(FFN_NCHUNK):
        g = half(slice(c * FFN_CHUNK, (c + 1) * FFN_CHUNK))
        val = half(slice(D_FF + c * FFN_CHUNK, D_FF + (c + 1) * FFN_CHUNK))
        act_sc[:, c * FFN_CHUNK:(c + 1) * FFN_CHUNK] = (_gelu_twice(g) * val).astype(BF16)
    f = jnp.dot(act_sc[...], wdn_ref[...], preferred_element_type=F32)
    xo_ref[...] = x + _rms(f, post_ref[...])


def _ffn(x_in, pg, wup, dw8, dwb, wdn_half, post, nb, tm, st2=None, mix=None):
    seq8 = st2 is not None
    xmap = lambda b, i: (b * nt + i, 0)
    fix = lambda b, i: (0, 0)
    fix3 = lambda b, i: (0, 0, 0)
    once = pl.Buffered(1)

    def col(block):
        return lambda b, i: (b * nt + i, block)

    if mix == "even":
        o2, z2, c2, x2, gn, w_mix, post_mix = x_in
        in_specs = [pl.BlockSpec((tm, A_WIDTH), xmap),
                    pl.BlockSpec((tm, A_WIDTH), col(2)),
                    pl.BlockSpec((tm, B_WIDTH), xmap),
                    pl.BlockSpec((tm, D_MODEL), xmap),
                    pl.BlockSpec((1, A_DV), fix),
                    pl.BlockSpec((A_WIDTH + B_WIDTH, D_MODEL), fix, pipeline_mode=once),
                    pl.BlockSpec((1, D_MODEL), fix)]
        args = [o2, z2, c2, x2, gn, w_mix, post_mix]
    elif mix == "odd":
        oc2, z2, x2, lng, lnb, wsm, sbe, w_mix, post_mix = x_in
        in_specs = [pl.BlockSpec((tm, C_WIDTH), xmap),
                    pl.BlockSpec((tm, D_WIDTH), col(9)),
                    pl.BlockSpec((tm, D_WIDTH), col(10)),
                    pl.BlockSpec((1, D_WIDTH), fix),
                    pl.BlockSpec((1, D_WIDTH), fix),
                    pl.BlockSpec((4, D_CHUNK, D_CHUNK), fix3),
                    pl.BlockSpec((D_CHUNK, D_WIDTH), fix),
                    pl.BlockSpec((tm, D_MODEL), xmap),
                    pl.BlockSpec((C_WIDTH + D_WIDTH, D_MODEL), fix, pipeline_mode=once),
                    pl.BlockSpec((1, D_MODEL), fix)]
        args = [oc2, z2, z2, lng, lnb, wsm, sbe, x2, w_mix, post_mix]
    else:
        x2 = x_in
        in_specs = [pl.BlockSpec((tm, D_MODEL), xmap)]
        args = [x2]
    assert len(args) == N_MIX_REFS[mix]
    rows = x2.shape[0]
    nt = rows // (nb * tm)
    in_specs += [pl.BlockSpec((1, D_MODEL), fix),
                 pl.BlockSpec((D_MODEL, 2 * D_FF), fix, pipeline_mode=once),
                 pl.BlockSpec((8, 2 * D_FF), fix),
                 pl.BlockSpec((1, 2 * D_FF), fix),
                 pl.BlockSpec((D_FF, D_MODEL), fix, pipeline_mode=once),
                 pl.BlockSpec((1, D_MODEL), fix)]
    args += [pg, wup, dw8, dwb, wdn_half, post]
    scratch = [pltpu.VMEM((tm, D_FF), BF16)]
    if seq8:
        ns = tm // 8
        in_specs.append(pl.BlockSpec((ns, 4 * D_FF), xmap))
        args.append(st2)
        tail_spec = pl.BlockSpec((ns, 4 * D_FF), xmap)
        tail_shape = jax.ShapeDtypeStruct((rows // 8, 4 * D_FF), F32)
        scratch += [pltpu.VMEM((FFN_CHUNK // 128, tm, 128), F32)] * 3
    else:
        tail_spec = pl.BlockSpec((1, 8, 2 * D_FF), lambda b, i: (b * nt + i, 0, 0))
        tail_shape = jax.ShapeDtypeStruct((nb * nt, 8, 2 * D_FF), F32)
        scratch.append(pltpu.VMEM((8, 2 * D_FF), F32))
    body = functools.partial(_ffn_body, tm=tm, seq8=seq8, mix=mix)
    return pl.pallas_call(
        body,
        grid=(nb, nt),
        in_specs=in_specs,
        out_specs=[pl.BlockSpec((tm, D_MODEL), xmap), tail_spec],
        out_shape=[jax.ShapeDtypeStruct((rows, D_MODEL), F32), tail_shape],
        scratch_shapes=scratch,
        compiler_params=_params(2),
        name="ffn" + ("_seq8" if seq8 else "") + ("_" + mix if mix else ""),
    )(*args)


def _attn_p_body(q0, k0, v0, q1, k1, v1, q2, k2, v2, bias_ref, o_ref, kv0, kv1, kv2,
                 qd, kd, vd, od, ld, og, lg, *, L):
    lane = lax.broadcasted_iota(jnp.int32, (C_QBLOCK, 2 * C_DH), 1)
    first = lane < C_DH
    col = lax.broadcasted_iota(jnp.int32, (C_QBLOCK, 2 * C_QBLOCK), 1)
    groups = ((q0, k0, v0), (q1, k1, v1), (q2, k2, v2))
    for g, (qr, kr, vr) in enumerate(groups):
        kvr = (kv0, kv1, kv2)[g]
        wl = kvr.shape[-1]
        piece = min(wl, 512)
        for c0 in range(0, wl, piece):
            src = slice(L - wl + c0, L - wl + c0 + piece)
            kvr[0, 0, :, c0:c0 + piece] = kr[0, src, :].T
            kvr[0, 1, :, c0:c0 + piece] = vr[0, src, :].T
        d = C_PATTERNS[g][1]
        nsub = L // d
        nblk = nsub // C_QBLOCK
        if d == 1:
            qs, ks, vs = qr.at[0], kr.at[0], vr.at[0]
            os_, ls_ = og.at[g], lg.at[g]
        else:
            for r in range(d):
                dst = slice(r * nsub, (r + 1) * nsub)
                qd[dst, :] = qr[0, pl.ds(r, nsub, stride=d), :]
                kd[dst, :] = kr[0, pl.ds(r, nsub, stride=d), :]
                vd[dst, :] = vr[0, pl.ds(r, nsub, stride=d), :]
            qs, ks, vs = qd, kd, vd
            os_, ls_ = od, ld

        def blk(idx, carry, qs=qs, ks=ks, vs=vs, os_=os_, ls_=ls_, nblk=nblk, g=g):
            rows = pl.ds(pl.multiple_of(idx * C_QBLOCK, C_QBLOCK), C_QBLOCK)
            qb = qs[rows, :] * (C_DH ** -0.5)
            kc = ks[rows, :]
            vc = vs[rows, :]
            m = idx & (nblk - 1)
            prow = pl.ds(pl.multiple_of(jnp.maximum(idx - 1, 0) * C_QBLOCK, C_QBLOCK), C_QBLOCK)
            kcat = jnp.concatenate([ks[prow, :], kc], axis=0).astype(BF16)
            vcat = jnp.concatenate([vs[prow, :], vc], axis=0).astype(BF16)
            dead = col < jnp.where(m == 0, C_QBLOCK, 0)
            outs, lses = [], []
            for hh in range(2):
                keep = first if hh == 0 else jnp.logical_not(first)
                qm = jnp.where(keep, qb, 0.0).astype(BF16)
                s = lax.dot_general(qm, kcat, NT_DIMS, preferred_element_type=F32)
                s = jnp.where(dead, NEG, s + bias_ref[g, hh])
                mx = jnp.max(s, axis=-1, keepdims=True)
                p = jnp.exp(s - mx)
                l = jnp.sum(p, axis=-1, keepdims=True)
                r = jnp.dot(p.astype(BF16), vcat, preferred_element_type=F32)
                outs.append(r / l)
                lses.append(mx + jnp.log(l))
            os_[rows, :] = jnp.where(first, outs[0], outs[1])
            ls_[rows, :] = jnp.where(first, lses[0], lses[1])
            return carry

        lax.fori_loop(0, d * nblk, blk, 0, unroll=ATTN_UNROLL)
        if d != 1:
            for r in range(d):
                src = slice(r * nsub, (r + 1) * nsub)
                og[g, pl.ds(r, nsub, stride=d), :] = od[src, :]
                lg[g, pl.ds(r, nsub, stride=d), :] = ld[src, :]
    cr = 256
    for cidx in range(L // cr):
        rows = slice(cidx * cr, (cidx + 1) * cr)
        l0, l1, l2 = lg[0, rows, :], lg[1, rows, :], lg[2, rows, :]
        mx = jnp.maximum(jnp.maximum(l0, l1), l2)
        w0, w1, w2 = jnp.exp(l0 - mx), jnp.exp(l1 - mx), jnp.exp(l2 - mx)
        num = w0 * og[0, rows, :] + w1 * og[1, rows, :] + w2 * og[2, rows, :]
        o_ref[0, rows, :] = num / (w0 + w1 + w2)


def _attn_prompt(z3, biasmat):
    nb, L, _ = z3.shape
    lw = 2 * C_DH
    in_specs = []
    for g in range(3):
        for part in range(3):
            base = part * 6 + g * 2
            in_specs.append(pl.BlockSpec((1, L, lw), lambda b, p, base=base: (b, 0, base + p)))
    in_specs.append(pl.BlockSpec((3, 2, C_QBLOCK, 2 * C_QBLOCK), lambda b, p: (0, p, 0, 0)))
    out_specs = [pl.BlockSpec((1, L, lw), lambda b, p: (b, 0, p))]
    out_shape = [jax.ShapeDtypeStruct((nb, L, C_WIDTH), F32)]
    for window, _ in C_PATTERNS:
        wl = min(window, L)
        out_specs.append(pl.BlockSpec((1, 2, lw, wl), lambda b, p: (b, 0, p, 0)))
        out_shape.append(jax.ShapeDtypeStruct((nb, 2, C_WIDTH, wl), F32))
    body = functools.partial(_attn_p_body, L=L)
    return pl.pallas_call(
        body,
        grid=(nb, 2),
        in_specs=in_specs,
        out_specs=out_specs,
        out_shape=out_shape,
        scratch_shapes=[pltpu.VMEM((L, lw), F32)] * 5 + [pltpu.VMEM((3, L, lw), F32)] * 2,
        compiler_params=_params(2),
        name="attn_prompt",
    )(*([z3] * 9), biasmat)


def _attn_s_body(z_ref, c0, c1, c2, ma0, ma1, ma2, mb_ref, o_ref, n0, n1, n2):
    nq = 8
    z = z_ref[0]
    hm = _head_mask(C_HEADS * nq, C_WIDTH, nq, C_DH)
    lane = lax.broadcasted_iota(jnp.int32, (2 * C_WIDTH, C_QBLOCK), 1)
    outs, lses = [], []
    for g, (cref, mref, nref) in enumerate(((c0, ma0, n0), (c1, ma1, n1), (c2, ma2, n2))):
        W = C_PATTERNS[g][0]
        q = z[:, g * C_WIDTH:(g + 1) * C_WIDTH] * (C_DH ** -0.5)
        kn = z[:, (3 + g) * C_WIDTH:(4 + g) * C_WIDTH]
        vn = z[:, (6 + g) * C_WIDTH:(7 + g) * C_WIDTH]
        new = jnp.concatenate([jnp.zeros((C_QBLOCK - nq, 2 * C_WIDTH), F32),
                               jnp.concatenate([kn, vn], axis=1)], axis=0)
        new_t = new.T
        old = cref[0]
        sh = pltpu.roll(old, W - nq, axis=1)
        if W > C_QBLOCK:
            nref[0, :, 0:W - C_QBLOCK] = sh[:, 0:W - C_QBLOCK]
        nref[0, :, W - C_QBLOCK:W] = jnp.where(lane >= C_QBLOCK - nq, new_t,
                                               sh[:, W - C_QBLOCK:W])
        qbd = jnp.where(hm, jnp.concatenate([q] * C_HEADS, axis=0), 0.0).astype(BF16)
        ka = old[0:C_WIDTH, :].astype(BF16)
        va = old[C_WIDTH:2 * C_WIDTH, :].astype(BF16)
        kb = new_t[0:C_WIDTH, :].astype(BF16)
        vb = new_t[C_WIDTH:2 * C_WIDTH, :].astype(BF16)
        sa = jnp.dot(qbd, ka, preferred_element_type=F32) + mref[...]
        sb = jnp.dot(qbd, kb, preferred_element_type=F32) + mb_ref[g]
        mx = jnp.maximum(jnp.max(sa, axis=-1, keepdims=True), jnp.max(sb, axis=-1, keepdims=True))
        pa = jnp.exp(sa - mx)
        pb = jnp.exp(sb - mx)
        l = jnp.sum(pa, axis=-1, keepdims=True) + jnp.sum(pb, axis=-1, keepdims=True)
        r = (lax.dot_general(pa.astype(BF16), va, NT_DIMS, preferred_element_type=F32)
             + lax.dot_general(pb.astype(BF16), vb, NT_DIMS, preferred_element_type=F32))
        outs.append(r / l)
        lses.append(mx + jnp.log(l))
    mx = jnp.maximum(jnp.maximum(lses[0], lses[1]), lses[2])
    ws = [jnp.exp(ls - mx) for ls in lses]
    o32 = (ws[0] * outs[0] + ws[1] * outs[1] + ws[2] * outs[2]) / (ws[0] + ws[1] + ws[2])
    o32 = jnp.where(hm, o32, 0.0)
    o_ref[0] = o32[0:nq] + o32[nq:2 * nq] + o32[2 * nq:3 * nq] + o32[3 * nq:4 * nq]


def _attn_sample(z3, caches, mas, mb):
    nb, nq, _ = z3.shape
    kvw = 2 * C_WIDTH
    in_specs = [pl.BlockSpec((1, nq, Z_WIDTH), lambda b: (b, 0, 0))]
    out_specs = [pl.BlockSpec((1, nq, C_WIDTH), lambda b: (b, 0, 0))]
    out_shape = [jax.ShapeDtypeStruct((nb, nq, C_WIDTH), F32)]
    for W, _ in C_PATTERNS:
        in_specs.append(pl.BlockSpec((1, kvw, W), lambda b: (b, 0, 0)))
        out_specs.append(pl.BlockSpec((1, kvw, W), lambda b: (b, 0, 0)))
        out_shape.append(jax.ShapeDtypeStruct((nb, kvw, W), F32))
    for W, _ in C_PATTERNS:
        in_specs.append(pl.BlockSpec((C_HEADS * nq, W), lambda b: (0, 0)))
    in_specs.append(pl.BlockSpec((3, C_HEADS * nq, C_QBLOCK), lambda b: (0, 0, 0)))
    return pl.pallas_call(
        _attn_s_body,
        grid=(nb,),
        in_specs=in_specs,
        out_specs=out_specs,
        out_shape=out_shape,
        compiler_params=_params(1),
        name="attn_sample",
    )(z3, *caches, *mas, mb)


def _odd_mix(oc_ref, du_ref, dv_ref, lng_ref, lnb_ref, ws_ref, sb_ref, x_ref, w_ref, post_ref):
    tm = x_ref.shape[0]
    u = _gelu(du_ref[...])
    vn = _layernorm(_gelu(dv_ref[...]), lng_ref[...], lnb_ref[...])
    lane_g = lax.broadcasted_iota(jnp.int32, (D_CHUNK, D_WIDTH), 1) >> int(math.log2(D_DH))
    parts = []
    for cidx in range(tm // D_CHUNK):
        vc = vn[cidx * D_CHUNK:(cidx + 1) * D_CHUNK, :]
        m = sb_ref[...]
        for g in range(D_WIDTH // D_DH):
            vm = jnp.where(lane_g == g, vc, 0.0).astype(BF16)
            m = m + jnp.dot(ws_ref[g], vm, preferred_element_type=F32)
        parts.append(m)
    od = u * jnp.concatenate(parts, axis=0)
    y = (jnp.dot(oc_ref[...].astype(BF16), w_ref[0:C_WIDTH, :], preferred_element_type=F32)
         + jnp.dot(od.astype(BF16), w_ref[C_WIDTH:C_WIDTH + D_WIDTH, :],
                   preferred_element_type=F32))
    return x_ref[...] + _rms(y, post_ref[...])


def _oddout_s_body(oc_ref, du_ref, dv_ref, lng_ref, lnb_ref, we_ref, sb_ref, x_ref, w_ref,
                   post_ref, xo_ref, vn_ref):
    nb, nq, _ = du_ref.shape
    u = _gelu(du_ref[...])
    vn = _layernorm(_gelu(dv_ref[...]), lng_ref[...], lnb_ref[...])
    vn_ref[...] = vn
    m = jnp.zeros((nb, nq, D_WIDTH), F32) + sb_ref[...]
    for s in range(nq):
        m = m + we_ref[s] * vn[:, s:s + 1, :]
    od = (u * m).reshape(nb * nq, D_WIDTH)
    y = (jnp.dot(oc_ref[...].astype(BF16), w_ref[0:C_WIDTH, :], preferred_element_type=F32)
         + jnp.dot(od.astype(BF16), w_ref[C_WIDTH:C_WIDTH + D_WIDTH, :],
                   preferred_element_type=F32))
    xo_ref[...] = x_ref[...] + _rms(y, post_ref[...])


def _oddout_sample(oc2, z3, x2, lng, lnb, wexp, sbe, w, post):
    nb, nq, _ = z3.shape
    rows = nb * nq
    return pl.pallas_call(
        _oddout_s_body,
        grid=(1,),
        in_specs=[pl.BlockSpec((rows, C_WIDTH), lambda i: (0, 0)),
                  pl.BlockSpec((nb, nq, D_WIDTH), lambda i: (0, 0, 9)),
                  pl.BlockSpec((nb, nq, D_WIDTH), lambda i: (0, 0, 10)),
                  pl.BlockSpec((1, D_WIDTH), lambda i: (0, 0)),
                  pl.BlockSpec((1, D_WIDTH), lambda i: (0, 0)),
                  pl.BlockSpec((nq, nq, D_WIDTH), lambda i: (0, 0, 0)),
                  pl.BlockSpec((nq, D_WIDTH), lambda i: (0, 0)),
                  pl.BlockSpec((rows, D_MODEL), lambda i: (0, 0)),
                  pl.BlockSpec((C_WIDTH + D_WIDTH, D_MODEL), lambda i: (0, 0)),
                  pl.BlockSpec((1, D_MODEL), lambda i: (0, 0))],
        out_specs=[pl.BlockSpec((rows, D_MODEL), lambda i: (0, 0)),
                   pl.BlockSpec((nb, nq, D_WIDTH), lambda i: (0, 0, 0))],
        out_shape=[jax.ShapeDtypeStruct((rows, D_MODEL), F32),
                   jax.ShapeDtypeStruct((nb, nq, D_WIDTH), F32)],
        compiler_params=_params(1),
        name="oddout_sample",
    )(oc2, z3, z3, lng, lnb, wexp, sbe, x2, w, post)


def _t5_bucket(dist):
    max_exact = N_BUCKETS // 2
    d32 = jnp.maximum(dist, 1).astype(F32)
    large = max_exact + (jnp.log(d32 / max_exact) / math.log(MAX_DIST / max_exact)
                         * (N_BUCKETS - max_exact)).astype(jnp.int32)
    large = jnp.minimum(large, N_BUCKETS - 1)
    return jnp.where(dist < max_exact, dist, large)


def _step_bias(rel_bias, g):
    window, dil = C_PATTERNS[g]
    j = jnp.arange(window // dil + 1, dtype=jnp.int32)
    return rel_bias[_t5_bucket(dil * j), g * C_HEADS:(g + 1) * C_HEADS].T.astype(F32)


def _skew(v, rows):
    n = v.shape[-1]
    lead = v.shape[:-1]
    t = jnp.broadcast_to(v[..., None, :], lead + (rows, n)).reshape(lead + (rows * n,))
    return t[..., :rows * (n - 1)].reshape(lead + (rows, n - 1))


def _prompt_bias(rel_bias):
    mats = []
    for g in range(len(C_PATTERNS)):
        bj = _step_bias(rel_bias, g)
        v = jnp.concatenate([bj[:, ::-1], jnp.full((C_HEADS, C_QBLOCK), NEG, F32)], axis=1)
        mats.append(_skew(v, C_QBLOCK))
    return jnp.stack(mats, axis=0)


def _sample_bias(rel_bias, nq):
    mas, mbs = [], []
    i = np.arange(nq)[:, None]
    i2 = np.arange(nq)[None, :]
    for g, (window, dil) in enumerate(C_PATTERNS):
        bj = _step_bias(rel_bias, g)
        on_grid = jnp.asarray(np.arange(window + 1) % dil == 0)
        bd = jnp.where(on_grid[None], jnp.repeat(bj, dil, axis=1)[:, :window + 1], NEG)
        e = jnp.concatenate([bd[:, ::-1], jnp.full((C_HEADS, nq - 1), NEG, F32)], axis=1)
        ma = _skew(e, nq)[:, :, :window]
        mas.append(ma.reshape(C_HEADS * nq, window))
        dist = i - i2
        ok = jnp.asarray((dist >= 0) & (dist % dil == 0))
        mb = jnp.where(ok[None], bd[:, np.maximum(dist, 0)], NEG)
        mb = jnp.pad(mb, ((0, 0), (0, 0), (C_QBLOCK - nq, 0)), constant_values=NEG)
        mbs.append(mb.reshape(C_HEADS * nq, C_QBLOCK))
    return mas, jnp.stack(mbs, axis=0)


def _row(v):
    return v.reshape(1, -1).astype(F32)


def kernel(x_prompt, x_sample, state_gla, state_conv_b, cache_c_w128, cache_c_w512, cache_c_w2048,
           state_ffn_conv, norm_pre_mix, norm_post_mix, norm_pre_ffn, norm_post_ffn, w_in_even,
           w_gate2, b_gate, gla_norm, conv_b_w, conv_b_b, ln_b_g, ln_b_b, w_out_even, w_in_odd,
           rel_bias, sgu_ln_g, sgu_ln_b, sgu_w, sgu_b, w_out_odd, w_up, ffn_dw_w, ffn_dw_b, w_down):
    nbp, lp, d = x_prompt.shape
    nbs, ls, _ = x_sample.shape

    we = w_in_even[0]
    split = 2 * A_QK + A_WIDTH
    w_e = jnp.concatenate(
        [we[:, :split], we[:, split + A_GATE_RANK:],
         we[:, split:split + A_GATE_RANK],
         jnp.zeros((d, A_QK - A_GATE_RANK), F32)], axis=1).astype(BF16)
    wg2p = jnp.zeros((A_QK, A_QK), F32).at[:A_GATE_RANK].set(w_gate2[0]).astype(BF16)
    segb = jnp.asarray(np.kron(np.eye(A_HEADS), np.ones((A_DK, A_DV))), BF16)
    cw = jnp.zeros((CONV_HDR, B_WIDTH), F32).at[:B_CONV].set(conv_b_w[0])
    w_oe = w_out_even[0].astype(BF16)
    w_o = w_in_odd[0].astype(BF16)
    w_oo = w_out_odd[0].astype(BF16)
    w_up_b = w_up.astype(BF16)
    w_dn_b = (0.5 * w_down).astype(BF16)
    dw8 = jnp.zeros((2, 8, 2 * D_FF), F32).at[:, :3].set(ffn_dw_w)
    tril = jnp.tril(jnp.ones((D_CHUNK, D_CHUNK), F32))
    sgu_wm = sgu_w[0] * tril
    bias_p = _prompt_bias(rel_bias)
    mas, mbs = _sample_bias(rel_bias, ls)

    def ffn_params(layer):
        return (_row(norm_pre_ffn[layer]), w_up_b[layer], dw8[layer], _row(ffn_dw_b[layer]),
                w_dn_b[layer], _row(norm_post_ffn[layer]))

    def even_layer(x2, nb, L, s0t, hdr, conv_tm, nseq):
        z = _inproj(x2, _row(norm_pre_mix[0]), w_e, 512)
        z3 = z.reshape(nb, L, Z_WIDTH)
        o, st = _gla(z3, wg2p, _row(b_gate[0]), segb, s0t, nseq)
        c, ut = _convmod(z3, hdr, cw, _row(conv_b_b[0]), _row(ln_b_g[0]), _row(ln_b_b[0]), conv_tm,
                         nseq)
        mix_in = (o.reshape(nb * L, A_WIDTH), z, c.reshape(nb * L, B_WIDTH), x2,
                  _row(gla_norm[0]), w_oe, _row(norm_post_mix[0]))
        return mix_in, st.reshape(nb, A_HEADS, A_DK, A_DV), ut

    def kv_window(t):
        return t.reshape(t.shape[0], 2, C_HEADS, C_DH, t.shape[-1]).transpose(0, 4, 1, 2, 3)[None]

    xp2 = x_prompt.reshape(nbp * lp, d)
    mix0, p_gla, ut = even_layer(xp2, nbp, lp, jnp.zeros((nbp, A_QK, A_DV), F32),
                                 jnp.zeros((nbp, CONV_HDR, B_WIDTH), F32), 512, 1)
    p_conv_b = ut[:, CONV_HDR - (B_CONV - 1):]
    x2, tail0 = _ffn(mix0, *ffn_params(0), nbp, 512, mix="even")
    z = _inproj(x2, _row(norm_pre_mix[1]), w_o, 512)
    z3 = z.reshape(nbp, lp, Z_WIDTH)
    oc, *p_kvt = _attn_prompt(z3, bias_p)
    wsm = sgu_wm.astype(BF16)
    sbe = jnp.repeat(sgu_b[0].T, D_DH, axis=-1)
    mix1 = (oc.reshape(nbp * lp, C_WIDTH), z, x2, _row(sgu_ln_g[0]), _row(sgu_ln_b[0]), wsm, sbe,
            w_oo, _row(norm_post_mix[1]))
    x4, tail1 = _ffn(mix1, *ffn_params(1), nbp, 512, mix="odd")
    y_prompt = x4.reshape(nbp, lp, d)
    p_kv = [kv_window(t) for t in p_kvt]
    p_ffn = jnp.stack([t.reshape(nbp, -1, 8, 2 * D_FF)[:, -1, 6:8] for t in (tail0, tail1)], axis=0)

    xs2 = x_sample.reshape(nbs * ls, d)
    hdr = jnp.pad(state_conv_b[0], ((0, 0), (CONV_HDR - (B_CONV - 1), 0), (0, 0)))
    smix0, s_gla, us = even_layer(xs2, nbs, ls, state_gla[0].reshape(nbs, A_QK, A_DV), hdr, ls,
                                  SAMPLE_NSEQ)
    s_conv_b = jnp.concatenate([state_conv_b[0][:, ls:], us], axis=1)

    def ffn_sample(x_in, layer, mix):
        st2 = state_ffn_conv[layer].reshape(nbs, 4 * D_FF)
        xo, tail = _ffn(x_in, *ffn_params(layer), 1, 256, st2, mix)
        return xo, tail.reshape(nbs, 2, 2 * D_FF)

    y2, s_ffn0 = ffn_sample(smix0, 0, "even")
    zs = _inproj(y2, _row(norm_pre_mix[1]), w_o, 512)
    zs3 = zs.reshape(nbs, ls, Z_WIDTH)
    caches = [c[0].transpose(0, 2, 3, 4, 1).reshape(nbs, 2 * C_WIDTH, c.shape[2])
              for c in (cache_c_w128, cache_c_w512, cache_c_w2048)]
    ocs, n0, n1, n2 = _attn_sample(zs3, caches, mas, mbs)
    wexp = jnp.repeat(sgu_wm[:, :ls, :ls].transpose(2, 1, 0), D_DH, axis=-1)
    sbes = jnp.repeat(sgu_b[0][:, :ls].T, D_DH, axis=-1)
    y3, s_sgu_v = _oddout_sample(ocs.reshape(nbs * ls, C_WIDTH), zs3, y2, _row(sgu_ln_g[0]),
                                 _row(sgu_ln_b[0]), wexp, sbes, w_oo, _row(norm_post_mix[1]))
    y4, s_ffn1 = ffn_sample(y3, 1, None)
    y_sample = y4.reshape(nbs, ls, d)
    s_kv = [kv_window(n.reshape(nbs, 2, C_WIDTH, n.shape[2])) for n in (n0, n1, n2)]
    s_ffn = jnp.stack([s_ffn0, s_ffn1], axis=0)

    return (y_prompt, y_sample, p_gla[None], p_conv_b[None], p_kv[0], p_kv[1], p_kv[2], p_ffn,
            s_gla[None], s_conv_b[None], s_kv[0], s_kv[1], s_kv[2], s_sgu_v[None], s_ffn)
```

```python
import functools
import math

import numpy as np
import jax
import jax.numpy as jnp
from jax import lax
from jax.experimental import pallas as pl
from jax.experimental.pallas import tpu as pltpu

F32 = jnp.float32
BF16 = jnp.bfloat16

D_MODEL = 1024
EPS = 1e-6
NEG = -1e30

A_HEADS = 4
A_DK = 64
A_DV = 128
A_QK = A_HEADS * A_DK
A_WIDTH = A_HEADS * A_DV
A_GATE_RANK = 16
A_GATE_NORM = 16.0
GLA_STEP = 16
GLA_UNROLL = 8
SAMPLE_NSEQ = 8
B_WIDTH = 512
B_CONV = 31
CONV_HDR = 32
C_PATTERNS = ((128, 1), (512, 4), (2048, 16))
C_HEADS = 4
C_DH = 64
C_WIDTH = C_HEADS * C_DH
C_QBLOCK = 128
ATTN_UNROLL = 16
N_BUCKETS = 32
MAX_DIST = 2048
D_WIDTH = 256
D_DH = 64
D_CHUNK = 128
D_FF = 2816
FFN_CHUNK = 256
FFN_NCHUNK = D_FF // FFN_CHUNK
Z_WIDTH = 2816

VMEM_LIMIT_BYTES = 56 * 1024 * 1024

NT_DIMS = (((1,), (1,)), ((), ()))
TN_DIMS = (((0,), (0,)), ((), ()))


def _params(n_axes):
    return pltpu.CompilerParams(dimension_semantics=("arbitrary",) * n_axes,
                                vmem_limit_bytes=VMEM_LIMIT_BYTES)


def _rms(x, g):
    return x * lax.rsqrt(jnp.mean(x * x, axis=-1, keepdims=True) + EPS) * g


def _layernorm(x, g, b):
    mu = jnp.mean(x, axis=-1, keepdims=True)
    xc = x - mu
    var = jnp.mean(xc * xc, axis=-1, keepdims=True)
    return xc * lax.rsqrt(var + EPS) * g + b


def _sigmoid(x):
    return 1.0 / (1.0 + jnp.exp(-x))


def _head_mask(rows, lanes, rows_per_head, lanes_per_head):
    r = lax.broadcasted_iota(jnp.int32, (rows, lanes), 0) >> int(math.log2(rows_per_head))
    c = lax.broadcasted_iota(jnp.int32, (rows, lanes), 1) >> int(math.log2(lanes_per_head))
    return r == c


def _gelu(x):
    c = math.sqrt(2.0 / math.pi)
    return 0.5 * x * (1.0 + jnp.tanh(c * (x + 0.044715 * (x * x * x))))


def _inproj_body(x_ref, g_ref, w_ref, o_ref):
    h = _rms(x_ref[...], g_ref[...])
    o_ref[...] = jnp.dot(h.astype(BF16), w_ref[...], preferred_element_type=F32)


def _inproj(x2, g, w, tm):
    rows, d = x2.shape
    n = w.shape[1]
    return pl.pallas_call(
        _inproj_body,
        grid=(rows // tm,),
        in_specs=[pl.BlockSpec((tm, d), lambda i: (i, 0)),
                  pl.BlockSpec((1, d), lambda i: (0, 0)),
                  pl.BlockSpec((d, n), lambda i: (0, 0))],
        out_specs=pl.BlockSpec((tm, n), lambda i: (i, 0)),
        out_shape=jax.ShapeDtypeStruct((rows, n), F32),
        compiler_params=_params(1),
        name="inproj",
    )(x2, g, w)


def _gla_body(q_ref, k_ref, v_ref, glr_ref, wg_ref, bg_ref, segb_ref, s0_ref,
              o_ref, sn_ref, b_sc, st_sc, *, L, SB, nseq):
    glr = glr_ref[...].reshape(nseq * L, A_QK)
    xg = jnp.dot(glr.astype(BF16), wg_ref[...], preferred_element_type=F32) + bg_ref[...]
    la = (jnp.minimum(xg, 0.0) - jnp.log(1.0 + jnp.exp(-jnp.abs(xg)))) * (1.0 / A_GATE_NORM)
    tl = lax.broadcasted_iota(jnp.int32, (nseq * L, A_QK), 0) & (SB - 1)
    sh = 1
    while sh < SB:
        la = la + jnp.where(tl >= sh, pltpu.roll(la, sh, axis=0), 0.0)
        sh *= 2
    b_sc[...] = la.reshape(nseq, L, A_QK)
    for si in range(nseq):
        st_sc[si] = s0_ref[si].T

    hm = _head_mask(A_HEADS * SB, A_QK, SB, A_DK)
    tri = lax.broadcasted_iota(jnp.int32, (8, A_QK), 0)
    ntile = SB // 8

    def one(si, rows):
        q = q_ref[si, rows, :] * (A_DK ** -0.5)
        k = k_ref[si, rows, :]
        v = v_ref[si, rows, :]
        b = b_sc[si, rows, :]
        bl = b[SB - 1:SB, :]
        qt = q * jnp.exp(b)
        kh = k * jnp.exp(bl - b)
        st = st_sc[si]
        qbd = jnp.where(hm, jnp.concatenate([qt] * A_HEADS, axis=0), 0.0).astype(BF16)
        o_int = lax.dot_general(qbd, st.astype(BF16), NT_DIMS, preferred_element_type=F32)
        kbd = jnp.where(hm, jnp.concatenate([kh] * A_HEADS, axis=0), 0.0).astype(BF16)
        vst = jnp.concatenate([v[:, h * A_DV:(h + 1) * A_DV] for h in range(A_HEADS)],
                              axis=0).astype(BF16)
        upd = lax.dot_general(vst, kbd, TN_DIMS, preferred_element_type=F32)
        st_sc[si] = st * jnp.exp(bl) + upd
        ps = []
        for s in range(SB):
            lo = 8 * (s // 8)
            bs = b[s:s + 1, :]
            ks = k[s:s + 1, :]
            dd = b[lo:lo + 8, :] - bs
            if s % 8:
                dd = jnp.where(tri >= s % 8, dd, NEG)
            ps.append(q[lo:lo + 8, :] * jnp.exp(dd) * ks)
            if lo + 8 < SB:
                ps.append(q[lo + 8:SB, :] * jnp.exp(b[lo + 8:SB, :] - bs) * ks)
        pall = jnp.concatenate(ps, axis=0).astype(BF16)
        r = jnp.dot(pall, segb_ref[...], preferred_element_type=F32)
        od = [None] * ntile
        off = 0
        for s in range(SB):
            for tq in range(s // 8, ntile):
                term = r[off:off + 8, :] * v[s:s + 1, :]
                od[tq] = term if od[tq] is None else od[tq] + term
                off += 8
        o = (jnp.concatenate([o_int[h * SB:(h + 1) * SB] for h in range(A_HEADS)], axis=1)
             + jnp.concatenate(od, axis=0))
        o_ref[si, rows, :] = o

    nsteps = L // SB
    if nsteps == 1:
        for si in range(nseq):
            one(si, slice(0, SB))
    else:
        def step(i, carry):
            rows = pl.ds(pl.multiple_of(i * SB, SB), SB)
            for si in range(nseq):
                one(si, rows)
            return carry

        lax.fori_loop(0, nsteps, step, 0, unroll=GLA_UNROLL)
    for si in range(nseq):
        sn_ref[si] = st_sc[si].T


def _gla(z3, wg2p, bgate, segb, s0t, nseq):
    nb, L, _ = z3.shape
    SB = GLA_STEP if L % GLA_STEP == 0 else L
    assert nseq == 1 or SB == L
    body = functools.partial(_gla_body, L=L, SB=SB, nseq=nseq)
    return pl.pallas_call(
        body,
        grid=(nb // nseq,),
        in_specs=[pl.BlockSpec((nseq, L, A_QK), lambda b: (b, 0, 0)),
                  pl.BlockSpec((nseq, L, A_QK), lambda b: (b, 0, 1)),
                  pl.BlockSpec((nseq, L, A_WIDTH), lambda b: (b, 0, 1)),
                  pl.BlockSpec((nseq, L, A_QK), lambda b: (b, 0, 10)),
                  pl.BlockSpec((A_QK, A_QK), lambda b: (0, 0)),
                  pl.BlockSpec((1, A_QK), lambda b: (0, 0)),
                  pl.BlockSpec((A_QK, A_WIDTH), lambda b: (0, 0)),
                  pl.BlockSpec((nseq, A_QK, A_DV), lambda b: (b, 0, 0))],
        out_specs=[pl.BlockSpec((nseq, L, A_WIDTH), lambda b: (b, 0, 0)),
                   pl.BlockSpec((nseq, A_QK, A_DV), lambda b: (b, 0, 0))],
        out_shape=[jax.ShapeDtypeStruct((nb, L, A_WIDTH), F32),
                   jax.ShapeDtypeStruct((nb, A_QK, A_DV), F32)],
        scratch_shapes=[pltpu.VMEM((nseq, L, A_QK), F32), pltpu.VMEM((nseq, A_DV, A_QK), F32)],
        compiler_params=_params(1),
        name="gla",
    )(z3, z3, z3, z3, wg2p, bgate, segb, s0t)


def _conv_body(ga_ref, gg_ref, hdr_ref, cw_ref, cb_ref, lng_ref, lnb_ref,
               c_ref, ut_ref, xp, *, tm, tail, nseq):
    j = pl.program_id(1)
    n = tm + 8
    for si in range(nseq):
        xs = xp.at[si]

        @pl.when(j == 0)
        def _(xs=xs, si=si):
            xs[0:CONV_HDR, :] = hdr_ref[si]
            xs[CONV_HDR + tm:CONV_HDR + n, :] = jnp.zeros((8, B_WIDTH), F32)

        u = ga_ref[si] * _sigmoid(gg_ref[si])
        xs[CONV_HDR:CONV_HDR + tm, :] = u
        acc = None
        for r in range(7, -1, -1):
            z = None
            for q in range(5):
                kk = 8 * q + r - (CONV_HDR - (B_CONV - 1))
                if 0 <= kk < B_CONV:
                    term = cw_ref[kk:kk + 1, :] * xs[8 * q:8 * q + n, :]
                    z = term if z is None else z + term
            acc = z if acc is None else z + pltpu.roll(acc, n - 1, axis=0)
        y = _layernorm(acc[0:tm, :] + cb_ref[...], lng_ref[...], lnb_ref[...])
        c_ref[si] = y * _sigmoid(y)
        ut_ref[si] = u[tm - tail:tm, :]
        nxt = xs[tm:tm + CONV_HDR, :]
        xs[0:CONV_HDR, :] = nxt


def _convmod(z3, hdr, cw, cb, lng, lnb, tm, nseq):
    nb, L, _ = z3.shape
    assert nseq == 1 or L == tm
    tail = min(CONV_HDR, tm)
    body = functools.partial(_conv_body, tm=tm, tail=tail, nseq=nseq)
    return pl.pallas_call(
        body,
        grid=(nb // nseq, L // tm),
        in_specs=[pl.BlockSpec((nseq, tm, B_WIDTH), lambda b, j: (b, j, 3)),
                  pl.BlockSpec((nseq, tm, B_WIDTH), lambda b, j: (b, j, 4)),
                  pl.BlockSpec((nseq, CONV_HDR, B_WIDTH), lambda b, j: (b, 0, 0)),
                  pl.BlockSpec((CONV_HDR, B_WIDTH), lambda b, j: (0, 0)),
                  pl.BlockSpec((1, B_WIDTH), lambda b, j: (0, 0)),
                  pl.BlockSpec((1, B_WIDTH), lambda b, j: (0, 0)),
                  pl.BlockSpec((1, B_WIDTH), lambda b, j: (0, 0))],
        out_specs=[pl.BlockSpec((nseq, tm, B_WIDTH), lambda b, j: (b, j, 0)),
                   pl.BlockSpec((nseq, tail, B_WIDTH), lambda b, j: (b, 0, 0))],
        out_shape=[jax.ShapeDtypeStruct((nb, L, B_WIDTH), F32),
                   jax.ShapeDtypeStruct((nb, tail, B_WIDTH), F32)],
        scratch_shapes=[pltpu.VMEM((nseq, CONV_HDR + tm + 8, B_WIDTH), F32)],
        compiler_params=_params(2),
        name="convmod",
    )(z3, z3, hdr, cw, cb, lng, lnb)


def _even_mix(o_ref, r_ref, c_ref, x_ref, gn_ref, w_ref, post_ref):
    o = o_ref[...]
    gn = gn_ref[...]
    oa = jnp.concatenate([_rms(o[:, h * A_DV:(h + 1) * A_DV], gn) for h in range(A_HEADS)], axis=1)
    r = r_ref[...]
    oa = oa * (r * _sigmoid(r))
    y = (jnp.dot(oa.astype(BF16), w_ref[0:A_WIDTH, :], preferred_element_type=F32)
         + jnp.dot(c_ref[...].astype(BF16), w_ref[A_WIDTH:A_WIDTH + B_WIDTH, :],
                   preferred_element_type=F32))
    return x_ref[...] + _rms(y, post_ref[...])


def _gelu_twice(x):
    c = math.sqrt(2.0 / math.pi)
    return x + x * jnp.tanh(x * (c + (c * 0.044715) * (x * x)))


N_MIX_REFS = {None: 1, "even": 7, "odd": 10}


def _ffn_body(*refs, tm, seq8, mix):
    head, refs = refs[:N_MIX_REFS[mix]], refs[N_MIX_REFS[mix]:]
    if seq8:
        (pg_ref, wup_ref, dw_ref, b_ref, wdn_ref, post_ref, st_ref,
         xo_ref, tail_ref, act_sc, p1_sc, p2_sc, u_sc) = refs
    else:
        (pg_ref, wup_ref, dw_ref, b_ref, wdn_ref, post_ref,
         xo_ref, tail_ref, act_sc, car) = refs
    i = pl.program_id(1)
    ns = tm // 8

    @pl.when(i == 0)
    def _():
        if seq8:
            p1_sc[...] = jnp.zeros(p1_sc.shape, F32)
            p2_sc[...] = jnp.zeros(p2_sc.shape, F32)
        else:
            car[...] = jnp.zeros((8, 2 * D_FF), F32)

    if mix == "even":
        x = _even_mix(*head)
    elif mix == "odd":
        x = _odd_mix(*head)
    else:
        x = head[0][...]
    h = _rms(x, pg_ref[...]).astype(BF16)
    t = lax.broadcasted_iota(jnp.int32, (tm, FFN_CHUNK), 0) & 7

    def half(cols):
        u = jnp.dot(h, wup_ref[:, cols], preferred_element_type=F32)
        if seq8:
            for hf in range(FFN_CHUNK // 128):
                lo = cols.start + 128 * hf
                c0 = slice(lo, lo + 128)
                c1 = slice(2 * D_FF + lo, 2 * D_FF + lo + 128)
                p1_sc[hf, pl.ds(0, ns, stride=8), :] = st_ref[:, c1]
                p2_sc[hf, pl.ds(0, ns, stride=8), :] = st_ref[:, c0]
                p2_sc[hf, pl.ds(1, ns, stride=8), :] = st_ref[:, c1]
                u_sc[hf] = u[:, 128 * hf:128 * (hf + 1)]
                tail_ref[:, c0] = u_sc[hf, pl.ds(6, ns, stride=8), :]
                tail_ref[:, c1] = u_sc[hf, pl.ds(7, ns, stride=8), :]
            p1 = jnp.concatenate([p1_sc[hf] for hf in range(FFN_CHUNK // 128)], axis=1)
            p2 = jnp.concatenate([p2_sc[hf] for hf in range(FFN_CHUNK // 128)], axis=1)
            u1 = jnp.where(t >= 1, pltpu.roll(u, 1, axis=0), p1)
            u2 = jnp.where(t >= 2, pltpu.roll(u, 2, axis=0), p2)
        else:
            xp = jnp.concatenate([car[:, cols], u], axis=0)
            u1 = pltpu.roll(xp, 1, axis=0)[8:, :]
            u2 = pltpu.roll(xp, 2, axis=0)[8:, :]
            last = u[tm - 8:tm, :]
            car[:, cols] = last
            tail_ref[0, :, cols] = last
        dw = dw_ref[:, cols]
        return dw[0:1, :] * u2 + dw[1:2, :] * u1 + dw[2:3, :] * u + b_ref[:, cols]

    for c in range(FFN_NCHUNK):
        g = half(slice(c * FFN_CHUNK, (c + 1) * FFN_CHUNK))
        val = half(slice(D_FF + c * FFN_CHUNK, D_FF + (c + 1) * FFN_CHUNK))
        act_sc[:, c * FFN_CHUNK:(c + 1) * FFN_CHUNK] = (_gelu_twice(g) * val).astype(BF16)
    f = jnp.dot(act_sc[...], wdn_ref[...], preferred_element_type=F32)
    xo_ref[...] = x + _rms(f, post_ref[...])


def _ffn(x_in, pg, wup, dw8, dwb, wdn_half, post, nb, tm, st2=None, mix=None):
    seq8 = st2 is not None
    xmap = lambda b, i: (b * nt + i, 0)
    fix = lambda b, i: (0, 0)
    fix3 = lambda b, i: (0, 0, 0)
    once = pl.Buffered(1)

    def col(block):
        return lambda b, i: (b * nt + i, block)

    if mix == "even":
        o2, z2, c2, x2, gn, w_mix, post_mix = x_in
        in_specs = [pl.BlockSpec((tm, A_WIDTH), xmap),
                    pl.BlockSpec((tm, A_WIDTH), col(2)),
                    pl.BlockSpec((tm, B_WIDTH), xmap),
                    pl.BlockSpec((tm, D_MODEL), xmap),
                    pl.BlockSpec((1, A_DV), fix),
                    pl.BlockSpec((A_WIDTH + B_WIDTH, D_MODEL), fix, pipeline_mode=once),
                    pl.BlockSpec((1, D_MODEL), fix)]
        args = [o2, z2, c2, x2, gn, w_mix, post_mix]
    elif mix == "odd":
        oc2, z2, x2, lng, lnb, wsm, sbe, w_mix, post_mix = x_in
        in_specs = [pl.BlockSpec((tm, C_WIDTH), xmap),
                    pl.BlockSpec((tm, D_WIDTH), col(9)),
                    pl.BlockSpec((tm, D_WIDTH), col(10)),
                    pl.BlockSpec((1, D_WIDTH), fix),
                    pl.BlockSpec((1, D_WIDTH), fix),
                    pl.BlockSpec((4, D_CHUNK, D_CHUNK), fix3),
                    pl.BlockSpec((D_CHUNK, D_WIDTH), fix),
                    pl.BlockSpec((tm, D_MODEL), xmap),
                    pl.BlockSpec((C_WIDTH + D_WIDTH, D_MODEL), fix, pipeline_mode=once),
                    pl.BlockSpec((1, D_MODEL), fix)]
        args = [oc2, z2, z2, lng, lnb, wsm, sbe, x2, w_mix, post_mix]
    else:
        x2 = x_in
        in_specs = [pl.BlockSpec((tm, D_MODEL), xmap)]
        args = [x2]
    assert len(args) == N_MIX_REFS[mix]
    rows = x2.shape[0]
    nt = rows // (nb * tm)
    in_specs += [pl.BlockSpec((1, D_MODEL), fix),
                 pl.BlockSpec((D_MODEL, 2 * D_FF), fix, pipeline_mode=once),
                 pl.BlockSpec((8, 2 * D_FF), fix),
                 pl.BlockSpec((1, 2 * D_FF), fix),
                 pl.BlockSpec((D_FF, D_MODEL), fix, pipeline_mode=once),
                 pl.BlockSpec((1, D_MODEL), fix)]
    args += [pg, wup, dw8, dwb, wdn_half, post]
    scratch = [pltpu.VMEM((tm, D_FF), BF16)]
    if seq8:
        ns = tm // 8
        in_specs.append(pl.BlockSpec((ns, 4 * D_FF), xmap))
        args.append(st2)
        tail_spec = pl.BlockSpec((ns, 4 * D_FF), xmap)
        tail_shape = jax.ShapeDtypeStruct((rows // 8, 4 * D_FF), F32)
        scratch += [pltpu.VMEM((FFN_CHUNK // 128, tm, 128), F32)] * 3
    else:
        tail_spec = pl.BlockSpec((1, 8, 2 * D_FF), lambda b, i: (b * nt + i, 0, 0))
        tail_shape = jax.ShapeDtypeStruct((nb * nt, 8, 2 * D_FF), F32)
        scratch.append(pltpu.VMEM((8, 2 * D_FF), F32))
    body = functools.partial(_ffn_body, tm=tm, seq8=seq8, mix=mix)
    return pl.pallas_call(
        body,
        grid=(nb, nt),
        in_specs=in_specs,
        out_specs=[pl.BlockSpec((tm, D_MODEL), xmap), tail_spec],
        out_shape=[jax.ShapeDtypeStruct((rows, D_MODEL), F32), tail_shape],
        scratch_shapes=scratch,
        compiler_params=_params(2),
        name="ffn" + ("_seq8" if seq8 else "") + ("_" + mix if mix else ""),
    )(*args)


def _attn_p_body(q0, k0, v0, q1, k1, v1, q2, k2, v2, bias_ref, o_ref, kv0, kv1, kv2,
                 qd, kd, vd, od, ld, og, lg, *, L):
    lane = lax.broadcasted_iota(jnp.int32, (C_QBLOCK, 2 * C_DH), 1)
    first = lane < C_DH
    col = lax.broadcasted_iota(jnp.int32, (C_QBLOCK, 2 * C_QBLOCK), 1)
    groups = ((q0, k0, v0), (q1, k1, v1), (q2, k2, v2))
    for g, (qr, kr, vr) in enumerate(groups):
        kvr = (kv0, kv1, kv2)[g]
        wl = kvr.shape[-1]
        piece = min(wl, 512)
        for c0 in range(0, wl, piece):
            src = slice(L - wl + c0, L - wl + c0 + piece)
            kvr[0, 0, :, c0:c0 + piece] = kr[0, src, :].T
            kvr[0, 1, :, c0:c0 + piece] = vr[0, src, :].T
        d = C_PATTERNS[g][1]
        nsub = L // d
        nblk = nsub // C_QBLOCK
        if d == 1:
            qs, ks, vs = qr.at[0], kr.at[0], vr.at[0]
            os_, ls_ = og.at[g], lg.at[g]
        else:
            for r in range(d):
                dst = slice(r * nsub, (r + 1) * nsub)
                qd[dst, :] = qr[0, pl.ds(r, nsub, stride=d), :]
                kd[dst, :] = kr[0, pl.ds(r, nsub, stride=d), :]
                vd[dst, :] = vr[0, pl.ds(r, nsub, stride=d), :]
            qs, ks, vs = qd, kd, vd
            os_, ls_ = od, ld

        def blk(idx, carry, qs=qs, ks=ks, vs=vs, os_=os_, ls_=ls_, nblk=nblk, g=g):
            rows = pl.ds(pl.multiple_of(idx * C_QBLOCK, C_QBLOCK), C_QBLOCK)
            qb = qs[rows, :] * (C_DH ** -0.5)
            kc = ks[rows, :]
            vc = vs[rows, :]
            m = idx & (nblk - 1)
            prow = pl.ds(pl.multiple_of(jnp.maximum(idx - 1, 0) * C_QBLOCK, C_QBLOCK), C_QBLOCK)
            kcat = jnp.concatenate([ks[prow, :], kc], axis=0).astype(BF16)
            vcat = jnp.concatenate([vs[prow, :], vc], axis=0).astype(BF16)
            dead = col < jnp.where(m == 0, C_QBLOCK, 0)
            outs, lses = [], []
            for hh in range(2):
                keep = first if hh == 0 else jnp.logical_not(first)
                qm = jnp.where(keep, qb, 0.0).astype(BF16)
                s = lax.dot_general(qm, kcat, NT_DIMS, preferred_element_type=F32)
                s = jnp.where(dead, NEG, s + bias_ref[g, hh])
                mx = jnp.max(s, axis=-1, keepdims=True)
                p = jnp.exp(s - mx)
                l = jnp.sum(p, axis=-1, keepdims=True)
                r = jnp.dot(p.astype(BF16), vcat, preferred_element_type=F32)
                outs.append(r / l)
                lses.append(mx + jnp.log(l))
            os_[rows, :] = jnp.where(first, outs[0], outs[1])
            ls_[rows, :] = jnp.where(first, lses[0], lses[1])
            return carry

        lax.fori_loop(0, d * nblk, blk, 0, unroll=ATTN_UNROLL)
        if d != 1:
            for r in range(d):
                src = slice(r * nsub, (r + 1) * nsub)
                og[g, pl.ds(r, nsub, stride=d), :] = od[src, :]
                lg[g, pl.ds(r, nsub, stride=d), :] = ld[src, :]
    cr = 256
    for cidx in range(L // cr):
        rows = slice(cidx * cr, (cidx + 1) * cr)
        l0, l1, l2 = lg[0, rows, :], lg[1, rows, :], lg[2, rows, :]
        mx = jnp.maximum(jnp.maximum(l0, l1), l2)
        w0, w1, w2 = jnp.exp(l0 - mx), jnp.exp(l1 - mx), jnp.exp(l2 - mx)
        num = w0 * og[0, rows, :] + w1 * og[1, rows, :] + w2 * og[2, rows, :]
        o_ref[0, rows, :] = num / (w0 + w1 + w2)


def _attn_prompt(z3, biasmat):
    nb, L, _ = z3.shape
    lw = 2 * C_DH
    in_specs = []
    for g in range(3):
        for part in range(3):
            base = part * 6 + g * 2
            in_specs.append(pl.BlockSpec((1, L, lw), lambda b, p, base=base: (b, 0, base + p)))
    in_specs.append(pl.BlockSpec((3, 2, C_QBLOCK, 2 * C_QBLOCK), lambda b, p: (0, p, 0, 0)))
    out_specs = [pl.BlockSpec((1, L, lw), lambda b, p: (b, 0, p))]
    out_shape = [jax.ShapeDtypeStruct((nb, L, C_WIDTH), F32)]
    for window, _ in C_PATTERNS:
        wl = min(window, L)
        out_specs.append(pl.BlockSpec((1, 2, lw, wl), lambda b, p: (b, 0, p, 0)))
        out_shape.append(jax.ShapeDtypeStruct((nb, 2, C_WIDTH, wl), F32))
    body = functools.partial(_attn_p_body, L=L)
    return pl.pallas_call(
        body,
        grid=(nb, 2),
        in_specs=in_specs,
        out_specs=out_specs,
        out_shape=out_shape,
        scratch_shapes=[pltpu.VMEM((L, lw), F32)] * 5 + [pltpu.VMEM((3, L, lw), F32)] * 2,
        compiler_params=_params(2),
        name="attn_prompt",
    )(*([z3] * 9), biasmat)


def _attn_s_body(z_ref, c0, c1, c2, ma0, ma1, ma2, mb_ref, o_ref, n0, n1, n2):
    nq = 8
    z = z_ref[0]
    hm = _head_mask(C_HEADS * nq, C_WIDTH, nq, C_DH)
    lane = lax.broadcasted_iota(jnp.int32, (2 * C_WIDTH, C_QBLOCK), 1)
    outs, lses = [], []
    for g, (cref, mref, nref) in enumerate(((c0, ma0, n0), (c1, ma1, n1), (c2, ma2, n2))):
        W = C_PATTERNS[g][0]
        q = z[:, g * C_WIDTH:(g + 1) * C_WIDTH] * (C_DH ** -0.5)
        kn = z[:, (3 + g) * C_WIDTH:(4 + g) * C_WIDTH]
        vn = z[:, (6 + g) * C_WIDTH:(7 + g) * C_WIDTH]
        new = jnp.concatenate([jnp.zeros((C_QBLOCK - nq, 2 * C_WIDTH), F32),
                               jnp.concatenate([kn, vn], axis=1)], axis=0)
        new_t = new.T
        old = cref[0]
        sh = pltpu.roll(old, W - nq, axis=1)
        if W > C_QBLOCK:
            nref[0, :, 0:W - C_QBLOCK] = sh[:, 0:W - C_QBLOCK]
        nref[0, :, W - C_QBLOCK:W] = jnp.where(lane >= C_QBLOCK - nq, new_t,
                                               sh[:, W - C_QBLOCK:W])
        qbd = jnp.where(hm, jnp.concatenate([q] * C_HEADS, axis=0), 0.0).astype(BF16)
        ka = old[0:C_WIDTH, :].astype(BF16)
        va = old[C_WIDTH:2 * C_WIDTH, :].astype(BF16)
        kb = new_t[0:C_WIDTH, :].astype(BF16)
        vb = new_t[C_WIDTH:2 * C_WIDTH, :].astype(BF16)
        sa = jnp.dot(qbd, ka, preferred_element_type=F32) + mref[...]
        sb = jnp.dot(qbd, kb, preferred_element_type=F32) + mb_ref[g]
        mx = jnp.maximum(jnp.max(sa, axis=-1, keepdims=True), jnp.max(sb, axis=-1, keepdims=True))
        pa = jnp.exp(sa - mx)
        pb = jnp.exp(sb - mx)
        l = jnp.sum(pa, axis=-1, keepdims=True) + jnp.sum(pb, axis=-1, keepdims=True)
        r = (lax.dot_general(pa.astype(BF16), va, NT_DIMS, preferred_element_type=F32)
             + lax.dot_general(pb.astype(BF16), vb, NT_DIMS, preferred_element_type=F32))
        outs.append(r / l)
        lses.append(mx + jnp.log(l))
    mx = jnp.maximum(jnp.maximum(lses[0], lses[1]), lses[2])
    ws = [jnp.exp(ls - mx) for ls in lses]
    o32 = (ws[0] * outs[0] + ws[1] * outs[1] + ws[2] * outs[2]) / (ws[0] + ws[1] + ws[2])
    o32 = jnp.where(hm, o32, 0.0)
    o_ref[0] = o32[0:nq] + o32[nq:2 * nq] + o32[2 * nq:3 * nq] + o32[3 * nq:4 * nq]


def _attn_sample(z3, caches, mas, mb):
    nb, nq, _ = z3.shape
    kvw = 2 * C_WIDTH
    in_specs = [pl.BlockSpec((1, nq, Z_WIDTH), lambda b: (b, 0, 0))]
    out_specs = [pl.BlockSpec((1, nq, C_WIDTH), lambda b: (b, 0, 0))]
    out_shape = [jax.ShapeDtypeStruct((nb, nq, C_WIDTH), F32)]
    for W, _ in C_PATTERNS:
        in_specs.append(pl.BlockSpec((1, kvw, W), lambda b: (b, 0, 0)))
        out_specs.append(pl.BlockSpec((1, kvw, W), lambda b: (b, 0, 0)))
        out_shape.append(jax.ShapeDtypeStruct((nb, kvw, W), F32))
    for W, _ in C_PATTERNS:
        in_specs.append(pl.BlockSpec((C_HEADS * nq, W), lambda b: (0, 0)))
    in_specs.append(pl.BlockSpec((3, C_HEADS * nq, C_QBLOCK), lambda b: (0, 0, 0)))
    return pl.pallas_call(
        _attn_s_body,
        grid=(nb,),
        in_specs=in_specs,
        out_specs=out_specs,
        out_shape=out_shape,
        compiler_params=_params(1),
        name="attn_sample",
    )(z3, *caches, *mas, mb)


def _odd_mix(oc_ref, du_ref, dv_ref, lng_ref, lnb_ref, ws_ref, sb_ref, x_ref, w_ref, post_ref):
    tm = x_ref.shape[0]
    u = _gelu(du_ref[...])
    vn = _layernorm(_gelu(dv_ref[...]), lng_ref[...], lnb_ref[...])
    lane_g = lax.broadcasted_iota(jnp.int32, (D_CHUNK, D_WIDTH), 1) >> int(math.log2(D_DH))
    parts = []
    for cidx in range(tm // D_CHUNK):
        vc = vn[cidx * D_CHUNK:(cidx + 1) * D_CHUNK, :]
        m = sb_ref[...]
        for g in range(D_WIDTH // D_DH):
            vm = jnp.where(lane_g == g, vc, 0.0).astype(BF16)
            m = m + jnp.dot(ws_ref[g], vm, preferred_element_type=F32)
        parts.append(m)
    od = u * jnp.concatenate(parts, axis=0)
    y = (jnp.dot(oc_ref[...].astype(BF16), w_ref[0:C_WIDTH, :], preferred_element_type=F32)
         + jnp.dot(od.astype(BF16), w_ref[C_WIDTH:C_WIDTH + D_WIDTH, :],
                   preferred_element_type=F32))
    return x_ref[...] + _rms(y, post_ref[...])


def _oddout_s_body(oc_ref, du_ref, dv_ref, lng_ref, lnb_ref, we_ref, sb_ref, x_ref, w_ref,
                   post_ref, xo_ref, vn_ref):
    nb, nq, _ = du_ref.shape
    u = _gelu(du_ref[...])
    vn = _layernorm(_gelu(dv_ref[...]), lng_ref[...], lnb_ref[...])
    vn_ref[...] = vn
    m = jnp.zeros((nb, nq, D_WIDTH), F32) + sb_ref[...]
    for s in range(nq):
        m = m + we_ref[s] * vn[:, s:s + 1, :]
    od = (u * m).reshape(nb * nq, D_WIDTH)
    y = (jnp.dot(oc_ref[...].astype(BF16), w_ref[0:C_WIDTH, :], preferred_element_type=F32)
         + jnp.dot(od.astype(BF16), w_ref[C_WIDTH:C_WIDTH + D_WIDTH, :],
                   preferred_element_type=F32))
    xo_ref[...] = x_ref[...] + _rms(y, post_ref[...])


def _oddout_sample(oc2, z3, x2, lng, lnb, wexp, sbe, w, post):
    nb, nq, _ = z3.shape
    rows = nb * nq
    return pl.pallas_call(
        _oddout_s_body,
        grid=(1,),
        in_specs=[pl.BlockSpec((rows, C_WIDTH), lambda i: (0, 0)),
                  pl.BlockSpec((nb, nq, D_WIDTH), lambda i: (0, 0, 9)),
                  pl.BlockSpec((nb, nq, D_WIDTH), lambda i: (0, 0, 10)),
                  pl.BlockSpec((1, D_WIDTH), lambda i: (0, 0)),
                  pl.BlockSpec((1, D_WIDTH), lambda i: (0, 0)),
                  pl.BlockSpec((nq, nq, D_WIDTH), lambda i: (0, 0, 0)),
                  pl.BlockSpec((nq, D_WIDTH), lambda i: (0, 0)),
                  pl.BlockSpec((rows, D_MODEL), lambda i: (0, 0)),
                  pl.BlockSpec((C_WIDTH + D_WIDTH, D_MODEL), lambda i: (0, 0)),
                  pl.BlockSpec((1, D_MODEL), lambda i: (0, 0))],
        out_specs=[pl.BlockSpec((rows, D_MODEL), lambda i: (0, 0)),
                   pl.BlockSpec((nb, nq, D_WIDTH), lambda i: (0, 0, 0))],
        out_shape=[jax.ShapeDtypeStruct((rows, D_MODEL), F32),
                   jax.ShapeDtypeStruct((nb, nq, D_WIDTH), F32)],
        compiler_params=_params(1),
        name="oddout_sample",
    )(oc2, z3, z3, lng, lnb, wexp, sbe, x2, w, post)


def _t5_bucket(dist):
    max_exact = N_BUCKETS // 2
    d32 = jnp.maximum(dist, 1).astype(F32)
    large = max_exact + (jnp.log(d32 / max_exact) / math.log(MAX_DIST / max_exact)
                         * (N_BUCKETS - max_exact)).astype(jnp.int32)
    large = jnp.minimum(large, N_BUCKETS - 1)
    return jnp.where(dist < max_exact, dist, large)


def _step_bias(rel_bias, g):
    window, dil = C_PATTERNS[g]
    j = jnp.arange(window // dil + 1, dtype=jnp.int32)
    return rel_bias[_t5_bucket(dil * j), g * C_HEADS:(g + 1) * C_HEADS].T.astype(F32)


def _skew(v, rows):
    n = v.shape[-1]
    lead = v.shape[:-1]
    t = jnp.broadcast_to(v[..., None, :], lead + (rows, n)).reshape(lead + (rows * n,))
    return t[..., :rows * (n - 1)].reshape(lead + (rows, n - 1))


def _prompt_bias(rel_bias):
    mats = []
    for g in range(len(C_PATTERNS)):
        bj = _step_bias(rel_bias, g)
        v = jnp.concatenate([bj[:, ::-1], jnp.full((C_HEADS, C_QBLOCK), NEG, F32)], axis=1)
        mats.append(_skew(v, C_QBLOCK))
    return jnp.stack(mats, axis=0)


def _sample_bias(rel_bias, nq):
    mas, mbs = [], []
    i = np.arange(nq)[:, None]
    i2 = np.arange(nq)[None, :]
    for g, (window, dil) in enumerate(C_PATTERNS):
        bj = _step_bias(rel_bias, g)
        on_grid = jnp.asarray(np.arange(window + 1) % dil == 0)
        bd = jnp.where(on_grid[None], jnp.repeat(bj, dil, axis=1)[:, :window + 1], NEG)
        e = jnp.concatenate([bd[:, ::-1], jnp.full((C_HEADS, nq - 1), NEG, F32)], axis=1)
        ma = _skew(e, nq)[:, :, :window]
        mas.append(ma.reshape(C_HEADS * nq, window))
        dist = i - i2
        ok = jnp.asarray((dist >= 0) & (dist % dil == 0))
        mb = jnp.where(ok[None], bd[:, np.maximum(dist, 0)], NEG)
        mb = jnp.pad(mb, ((0, 0), (0, 0), (C_QBLOCK - nq, 0)), constant_values=NEG)
        mbs.append(mb.reshape(C_HEADS * nq, C_QBLOCK))
    return mas, jnp.stack(mbs, axis=0)


def _row(v):
    return v.reshape(1, -1).astype(F32)


def kernel(x_prompt, x_sample, state_gla, state_conv_b, cache_c_w128, cache_c_w512, cache_c_w2048,
           state_ffn_conv, norm_pre_mix, norm_post_mix, norm_pre_ffn, norm_post_ffn, w_in_even,
           w_gate2, b_gate, gla_norm, conv_b_w, conv_b_b, ln_b_g, ln_b_b, w_out_even, w_in_odd,
           rel_bias, sgu_ln_g, sgu_ln_b, sgu_w, sgu_b, w_out_odd, w_up, ffn_dw_w, ffn_dw_b, w_down):
    nbp, lp, d = x_prompt.shape
    nbs, ls, _ = x_sample.shape

    we = w_in_even[0]
    split = 2 * A_QK + A_WIDTH
    w_e = jnp.concatenate(
        [we[:, :split], we[:, split + A_GATE_RANK:],
         we[:, split:split + A_GATE_RANK],
         jnp.zeros((d, A_QK - A_GATE_RANK), F32)], axis=1).astype(BF16)
    wg2p = jnp.zeros((A_QK, A_QK), F32).at[:A_GATE_RANK].set(w_gate2[0]).astype(BF16)
    segb = jnp.asarray(np.kron(np.eye(A_HEADS), np.ones((A_DK, A_DV))), BF16)
    cw = jnp.zeros((CONV_HDR, B_WIDTH), F32).at[:B_CONV].set(conv_b_w[0])
    w_oe = w_out_even[0].astype(BF16)
    w_o = w_in_odd[0].astype(BF16)
    w_oo = w_out_odd[0].astype(BF16)
    w_up_b = w_up.astype(BF16)
    w_dn_b = (0.5 * w_down).astype(BF16)
    dw8 = jnp.zeros((2, 8, 2 * D_FF), F32).at[:, :3].set(ffn_dw_w)
    tril = jnp.tril(jnp.ones((D_CHUNK, D_CHUNK), F32))
    sgu_wm = sgu_w[0] * tril
    bias_p = _prompt_bias(rel_bias)
    mas, mbs = _sample_bias(rel_bias, ls)

    def ffn_params(layer):
        return (_row(norm_pre_ffn[layer]), w_up_b[layer], dw8[layer], _row(ffn_dw_b[layer]),
                w_dn_b[layer], _row(norm_post_ffn[layer]))

    def even_layer(x2, nb, L, s0t, hdr, conv_tm, nseq):
        z = _inproj(x2, _row(norm_pre_mix[0]), w_e, 512)
        z3 = z.reshape(nb, L, Z_WIDTH)
        o, st = _gla(z3, wg2p, _row(b_gate[0]), segb, s0t, nseq)
        c, ut = _convmod(z3, hdr, cw, _row(conv_b_b[0]), _row(ln_b_g[0]), _row(ln_b_b[0]), conv_tm,
                         nseq)
        mix_in = (o.reshape(nb * L, A_WIDTH), z, c.reshape(nb * L, B_WIDTH), x2,
                  _row(gla_norm[0]), w_oe, _row(norm_post_mix[0]))
        return mix_in, st.reshape(nb, A_HEADS, A_DK, A_DV), ut

    def kv_window(t):
        return t.reshape(t.shape[0], 2, C_HEADS, C_DH, t.shape[-1]).transpose(0, 4, 1, 2, 3)[None]

    xp2 = x_prompt.reshape(nbp * lp, d)
    mix0, p_gla, ut = even_layer(xp2, nbp, lp, jnp.zeros((nbp, A_QK, A_DV), F32),
                                 jnp.zeros((nbp, CONV_HDR, B_WIDTH), F32), 512, 1)
    p_conv_b = ut[:, CONV_HDR - (B_CONV - 1):]
    x2, tail0 = _ffn(mix0, *ffn_params(0), nbp, 512, mix="even")
    z = _inproj(x2, _row(norm_pre_mix[1]), w_o, 1024)
    z3 = z.reshape(nbp, lp, Z_WIDTH)
    oc, *p_kvt = _attn_prompt(z3, bias_p)
    wsm = sgu_wm.astype(BF16)
    sbe = jnp.repeat(sgu_b[0].T, D_DH, axis=-1)
    mix1 = (oc.reshape(nbp * lp, C_WIDTH), z, x2, _row(sgu_ln_g[0]), _row(sgu_ln_b[0]), wsm, sbe,
            w_oo, _row(norm_post_mix[1]))
    x4, tail1 = _ffn(mix1, *ffn_params(1), nbp, 512, mix="odd")
    y_prompt = x4.reshape(nbp, lp, d)
    p_kv = [kv_window(t) for t in p_kvt]
    p_ffn = jnp.stack([t.reshape(nbp, -1, 8, 2 * D_FF)[:, -1, 6:8] for t in (tail0, tail1)], axis=0)

    xs2 = x_sample.reshape(nbs * ls, d)
    hdr = jnp.pad(state_conv_b[0], ((0, 0), (CONV_HDR - (B_CONV - 1), 0), (0, 0)))
    smix0, s_gla, us = even_layer(xs2, nbs, ls, state_gla[0].reshape(nbs, A_QK, A_DV), hdr, ls,
                                  SAMPLE_NSEQ)
    s_conv_b = jnp.concatenate([state_conv_b[0][:, ls:], us], axis=1)

    def ffn_sample(x_in, layer, mix):
        st2 = state_ffn_conv[layer].reshape(nbs, 4 * D_FF)
        xo, tail = _ffn(x_in, *ffn_params(layer), 1, 256, st2, mix)
        return xo, tail.reshape(nbs, 2, 2 * D_FF)

    y2, s_ffn0 = ffn_sample(smix0, 0, "even")
    zs = _inproj(y2, _row(norm_pre_mix[1]), w_o, 512)
    zs3 = zs.reshape(nbs, ls, Z_WIDTH)
    caches = [c[0].transpose(0, 2, 3, 4, 1).reshape(nbs, 2 * C_WIDTH, c.shape[2])
              for c in (cache_c_w128, cache_c_w512, cache_c_w2048)]
    ocs, n0, n1, n2 = _attn_sample(zs3, caches, mas, mbs)
    wexp = jnp.repeat(sgu_wm[:, :ls, :ls].transpose(2, 1, 0), D_DH, axis=-1)
    sbes = jnp.repeat(sgu_b[0][:, :ls].T, D_DH, axis=-1)
    y3, s_sgu_v = _oddout_sample(ocs.reshape(nbs * ls, C_WIDTH), zs3, y2, _row(sgu_ln_g[0]),
                                 _row(sgu_ln_b[0]), wexp, sbes, w_oo, _row(norm_post_mix[1]))
    y4, s_ffn1 = ffn_sample(y3, 1, None)
    y_sample = y4.reshape(nbs, ls, d)
    s_kv = [kv_window(n.reshape(nbs, 2, C_WIDTH, n.shape[2])) for n in (n0, n1, n2)]
    s_ffn = jnp.stack([s_ffn0, s_ffn1], axis=0)

    return (y_prompt, y_sample, p_gla[None], p_conv_b[None], p_kv[0], p_kv[1], p_kv[2], p_ffn,
            s_gla[None], s_conv_b[None], s_kv[0], s_kv[1], s_kv[2], s_sgu_v[None], s_ffn)
```

```python
import functools
import math

import numpy as np
import jax
import jax.numpy as jnp
from jax import lax
from jax.experimental import pallas as pl
from jax.experimental.pallas import tpu as pltpu

F32 = jnp.float32
BF16 = jnp.bfloat16

D_MODEL = 1024
EPS = 1e-6
NEG = -1e30

A_HEADS = 4
A_DK = 64
A_DV = 128
A_QK = A_HEADS * A_DK
A_WIDTH = A_HEADS * A_DV
A_GATE_RANK = 16
A_GATE_NORM = 16.0
GLA_STEP = 16
MIX0_PIECE = 128
SAMPLE_NSEQ = 8
B_WIDTH = 512
B_CONV = 31
CONV_HDR = 32
C_PATTERNS = ((128, 1), (512, 4), (2048, 16))
C_HEADS = 4
C_DH = 64
C_WIDTH = C_HEADS * C_DH
C_QBLOCK = 128
N_BUCKETS = 32
MAX_DIST = 2048
D_WIDTH = 256
D_DH = 64
D_CHUNK = 128
D_FF = 2816
FFN_CHUNK = 256
FFN_NCHUNK = D_FF // FFN_CHUNK
Z_WIDTH = 2816

VMEM_LIMIT_BYTES = 56 * 1024 * 1024

NT_DIMS = (((1,), (1,)), ((), ()))
TN_DIMS = (((0,), (0,)), ((), ()))


def _params(n_axes):
    return pltpu.CompilerParams(dimension_semantics=("arbitrary",) * n_axes,
                                vmem_limit_bytes=VMEM_LIMIT_BYTES)


def _rms(x, g):
    return x * lax.rsqrt(jnp.mean(x * x, axis=-1, keepdims=True) + EPS) * g


def _layernorm(x, g, b):
    mu = jnp.mean(x, axis=-1, keepdims=True)
    xc = x - mu
    var = jnp.mean(xc * xc, axis=-1, keepdims=True)
    return xc * lax.rsqrt(var + EPS) * g + b


def _sigmoid(x):
    return 1.0 / (1.0 + jnp.exp(-x))


def _head_mask(rows, lanes, rows_per_head, lanes_per_head):
    r = lax.broadcasted_iota(jnp.int32, (rows, lanes), 0) >> int(math.log2(rows_per_head))
    c = lax.broadcasted_iota(jnp.int32, (rows, lanes), 1) >> int(math.log2(lanes_per_head))
    return r == c


def _gelu(x):
    c = math.sqrt(2.0 / math.pi)
    return 0.5 * x * (1.0 + jnp.tanh(c * (x + 0.044715 * (x * x * x))))


def _inproj_body(x_ref, g_ref, w_ref, o_ref):
    h = _rms(x_ref[...], g_ref[...])
    o_ref[...] = jnp.dot(h.astype(BF16), w_ref[...], preferred_element_type=F32)


def _inproj(x2, g, w, tm):
    rows, d = x2.shape
    n = w.shape[1]
    return pl.pallas_call(
        _inproj_body,
        grid=(rows // tm,),
        in_specs=[pl.BlockSpec((tm, d), lambda i: (i, 0)),
                  pl.BlockSpec((1, d), lambda i: (0, 0)),
                  pl.BlockSpec((d, n), lambda i: (0, 0))],
        out_specs=pl.BlockSpec((tm, n), lambda i: (i, 0)),
        out_shape=jax.ShapeDtypeStruct((rows, n), F32),
        compiler_params=_params(1),
        name="inproj",
    )(x2, g, w)


def _mixer0_body(q_ref, k_ref, v_ref, glr_ref, ga_ref, gg_ref, wg_ref, bg_ref, segb_ref, s0_ref,
                 hdr_ref, cw_ref, cb_ref, lng_ref, lnb_ref,
                 o_ref, sn_ref, c_ref, ut_ref, b_sc, st_sc, xp, *, tm, SB, nseq, tail, piece):
    j = pl.program_id(1)
    last_j = pl.num_programs(1) - 1

    glr = glr_ref[...].reshape(nseq * tm, A_QK)
    xg = jnp.dot(glr.astype(BF16), wg_ref[...], preferred_element_type=F32) + bg_ref[...]
    la = (jnp.minimum(xg, 0.0) - jnp.log(1.0 + jnp.exp(-jnp.abs(xg)))) * (1.0 / A_GATE_NORM)
    tl = lax.broadcasted_iota(jnp.int32, (nseq * tm, A_QK), 0) & (SB - 1)
    sh = 1
    while sh < SB:
        la = la + jnp.where(tl >= sh, pltpu.roll(la, sh, axis=0), 0.0)
        sh *= 2
    b_sc[...] = la.reshape(nseq, tm, A_QK)

    for si in range(nseq):
        xs = xp.at[si]

        @pl.when(j == 0)
        def _(xs=xs, si=si):
            st_sc[si] = s0_ref[si].T
            xs[0:CONV_HDR, :] = hdr_ref[si]
            xs[CONV_HDR + tm:CONV_HDR + tm + 8, :] = jnp.zeros((8, B_WIDTH), F32)

        u = ga_ref[si] * _sigmoid(gg_ref[si])
        xs[CONV_HDR:CONV_HDR + tm, :] = u
        ut_ref[si] = u[tm - tail:tm, :]

    hm = _head_mask(A_HEADS * SB, A_QK, SB, A_DK)
    tri = lax.broadcasted_iota(jnp.int32, (8, A_QK), 0)
    ntile = SB // 8

    def gla_step(si, rows):
        q = q_ref[si, rows, :] * (A_DK ** -0.5)
        k = k_ref[si, rows, :]
        v = v_ref[si, rows, :]
        b = b_sc[si, rows, :]
        bl = b[SB - 1:SB, :]
        qt = q * jnp.exp(b)
        kh = k * jnp.exp(bl - b)
        st = st_sc[si]
        qbd = jnp.where(hm, jnp.concatenate([qt] * A_HEADS, axis=0), 0.0).astype(BF16)
        o_int = lax.dot_general(qbd, st.astype(BF16), NT_DIMS, preferred_element_type=F32)
        kbd = jnp.where(hm, jnp.concatenate([kh] * A_HEADS, axis=0), 0.0).astype(BF16)
        vst = jnp.concatenate([v[:, h * A_DV:(h + 1) * A_DV] for h in range(A_HEADS)],
                              axis=0).astype(BF16)
        upd = lax.dot_general(vst, kbd, TN_DIMS, preferred_element_type=F32)
        st_sc[si] = st * jnp.exp(bl) + upd
        ps = []
        for s in range(SB):
            lo = 8 * (s // 8)
            bs = b[s:s + 1, :]
            ks = k[s:s + 1, :]
            dd = b[lo:lo + 8, :] - bs
            if s % 8:
                dd = jnp.where(tri >= s % 8, dd, NEG)
            ps.append(q[lo:lo + 8, :] * jnp.exp(dd) * ks)
            if lo + 8 < SB:
                ps.append(q[lo + 8:SB, :] * jnp.exp(b[lo + 8:SB, :] - bs) * ks)
        pall = jnp.concatenate(ps, axis=0).astype(BF16)
        r = jnp.dot(pall, segb_ref[...], preferred_element_type=F32)
        od = [None] * ntile
        off = 0
        for s in range(SB):
            for tq in range(s // 8, ntile):
                term = r[off:off + 8, :] * v[s:s + 1, :]
                od[tq] = term if od[tq] is None else od[tq] + term
                off += 8
        o = (jnp.concatenate([o_int[h * SB:(h + 1) * SB] for h in range(A_HEADS)], axis=1)
             + jnp.concatenate(od, axis=0))
        o_ref[si, rows, :] = o

    def conv_rows(si, r0, static):
        xs = xp.at[si]
        n = piece + 8
        acc = None
        for r in range(7, -1, -1):
            z = None
            for q in range(5):
                kk = 8 * q + r - (CONV_HDR - (B_CONV - 1))
                if 0 <= kk < B_CONV:
                    start = 8 * q + r0 if static else pl.multiple_of(8 * q + r0, 8)
                    term = cw_ref[kk:kk + 1, :] * xs[pl.ds(start, n), :]
                    z = term if z is None else z + term
            acc = z if acc is None else z + pltpu.roll(acc, n - 1, axis=0)
        y = _layernorm(acc[0:piece, :] + cb_ref[...], lng_ref[...], lnb_ref[...])
        c_ref[si, pl.ds(r0, piece), :] = y * _sigmoid(y)

    npiece = tm // piece
    if npiece == 1:
        for si in range(nseq):
            for s in range(piece // SB):
                gla_step(si, slice(s * SB, (s + 1) * SB))
            conv_rows(si, 0, True)
    else:
        def do_piece(p, carry):
            r0 = pl.multiple_of(p * piece, piece)
            for si in range(nseq):
                for s in range(piece // SB):
                    gla_step(si, pl.ds(pl.multiple_of(r0 + s * SB, SB), SB))
                conv_rows(si, r0, False)
            return carry

        lax.fori_loop(0, npiece, do_piece, 0)

    for si in range(nseq):
        xs = xp.at[si]
        nxt = xs[tm:tm + CONV_HDR, :]
        xs[0:CONV_HDR, :] = nxt

        @pl.when(j == last_j)
        def _(si=si):
            sn_ref[si] = st_sc[si].T


def _mixer0(z3, wg2p, bgate, segb, s0, hdr, cw, cb, lng, lnb, tm, nseq):
    nb, L, _ = z3.shape
    assert nseq == 1 or L == tm
    SB = GLA_STEP if tm % GLA_STEP == 0 else tm
    piece = min(tm, MIX0_PIECE)
    tail = min(CONV_HDR, tm)
    fix = lambda b, j: (0, 0)

    def zcol(block):
        return lambda b, j: (b, j, block)

    body = functools.partial(_mixer0_body, tm=tm, SB=SB, nseq=nseq, tail=tail, piece=piece)
    return pl.pallas_call(
        body,
        grid=(nb // nseq, L // tm),
        in_specs=[pl.BlockSpec((nseq, tm, A_QK), zcol(0)),
                  pl.BlockSpec((nseq, tm, A_QK), zcol(1)),
                  pl.BlockSpec((nseq, tm, A_WIDTH), zcol(1)),
                  pl.BlockSpec((nseq, tm, A_QK), zcol(10)),
                  pl.BlockSpec((nseq, tm, B_WIDTH), zcol(3)),
                  pl.BlockSpec((nseq, tm, B_WIDTH), zcol(4)),
                  pl.BlockSpec((A_QK, A_QK), fix),
                  pl.BlockSpec((1, A_QK), fix),
                  pl.BlockSpec((A_QK, A_WIDTH), fix),
                  pl.BlockSpec((nseq, A_QK, A_DV), lambda b, j: (b, 0, 0)),
                  pl.BlockSpec((nseq, CONV_HDR, B_WIDTH), lambda b, j: (b, 0, 0)),
                  pl.BlockSpec((CONV_HDR, B_WIDTH), fix),
                  pl.BlockSpec((1, B_WIDTH), fix),
                  pl.BlockSpec((1, B_WIDTH), fix),
                  pl.BlockSpec((1, B_WIDTH), fix)],
        out_specs=[pl.BlockSpec((nseq, tm, A_WIDTH), lambda b, j: (b, j, 0)),
                   pl.BlockSpec((nseq, A_QK, A_DV), lambda b, j: (b, 0, 0)),
                   pl.BlockSpec((nseq, tm, B_WIDTH), lambda b, j: (b, j, 0)),
                   pl.BlockSpec((nseq, tail, B_WIDTH), lambda b, j: (b, 0, 0))],
        out_shape=[jax.ShapeDtypeStruct((nb, L, A_WIDTH), F32),
                   jax.ShapeDtypeStruct((nb, A_QK, A_DV), F32),
                   jax.ShapeDtypeStruct((nb, L, B_WIDTH), F32),
                   jax.ShapeDtypeStruct((nb, tail, B_WIDTH), F32)],
        scratch_shapes=[pltpu.VMEM((nseq, tm, A_QK), F32),
                        pltpu.VMEM((nseq, A_DV, A_QK), F32),
                        pltpu.VMEM((nseq, CONV_HDR + tm + 8, B_WIDTH), F32)],
        compiler_params=_params(2),
        name="mixer0",
    )(z3, z3, z3, z3, z3, z3, wg2p, bgate, segb, s0, hdr, cw, cb, lng, lnb)


def _even_mix(o_ref, r_ref, c_ref, x_ref, gn_ref, w_ref, post_ref):
    o = o_ref[...]
    gn = gn_ref[...]
    oa = jnp.concatenate([_rms(o[:, h * A_DV:(h + 1) * A_DV], gn) for h in range(A_HEADS)], axis=1)
    r = r_ref[...]
    oa = oa * (r * _sigmoid(r))
    y = (jnp.dot(oa.astype(BF16), w_ref[0:A_WIDTH, :], preferred_element_type=F32)
         + jnp.dot(c_ref[...].astype(BF16), w_ref[A_WIDTH:A_WIDTH + B_WIDTH, :],
                   preferred_element_type=F32))
    return x_ref[...] + _rms(y, post_ref[...])


def _gelu_twice(x):
    c = math.sqrt(2.0 / math.pi)
    return x + x * jnp.tanh(x * (c + (c * 0.044715) * (x * x)))


N_MIX_REFS = {None: 1, "even": 7, "odd": 10}


def _ffn_body(*refs, tm, seq8, mix):
    head, refs = refs[:N_MIX_REFS[mix]], refs[N_MIX_REFS[mix]:]
    if seq8:
        (pg_ref, wup_ref, dw_ref, b_ref, wdn_ref, post_ref, st_ref,
         xo_ref, tail_ref, act_sc, p1_sc, p2_sc, u_sc) = refs
    else:
        (pg_ref, wup_ref, dw_ref, b_ref, wdn_ref, post_ref,
         xo_ref, tail_ref, act_sc, car) = refs
    i = pl.program_id(1)
    ns = tm // 8

    @pl.when(i == 0)
    def _():
        if seq8:
            p1_sc[...] = jnp.zeros(p1_sc.shape, F32)
            p2_sc[...] = jnp.zeros(p2_sc.shape, F32)
        else:
            car[...] = jnp.zeros((8, 2 * D_FF), F32)

    if mix == "even":
        x = _even_mix(*head)
    elif mix == "odd":
        x = _odd_mix(*head)
    else:
        x = head[0][...]
    h = _rms(x, pg_ref[...]).astype(BF16)
    t = lax.broadcasted_iota(jnp.int32, (tm, FFN_CHUNK), 0) & 7

    def half(cols):
        u = jnp.dot(h, wup_ref[:, cols], preferred_element_type=F32)
        if seq8:
            for hf in range(FFN_CHUNK // 128):
                lo = cols.start + 128 * hf
                c0 = slice(lo, lo + 128)
                c1 = slice(2 * D_FF + lo, 2 * D_FF + lo + 128)
                p1_sc[hf, pl.ds(0, ns, stride=8), :] = st_ref[:, c1]
                p2_sc[hf, pl.ds(0, ns, stride=8), :] = st_ref[:, c0]
                p2_sc[hf, pl.ds(1, ns, stride=8), :] = st_ref[:, c1]
                u_sc[hf] = u[:, 128 * hf:128 * (hf + 1)]
                tail_ref[:, c0] = u_sc[hf, pl.ds(6, ns, stride=8), :]
                tail_ref[:, c1] = u_sc[hf, pl.ds(7, ns, stride=8), :]
            p1 = jnp.concatenate([p1_sc[hf] for hf in range(FFN_CHUNK // 128)], axis=1)
            p2 = jnp.concatenate([p2_sc[hf] for hf in range(FFN_CHUNK // 128)], axis=1)
            u1 = jnp.where(t >= 1, pltpu.roll(u, 1, axis=0), p1)
            u2 = jnp.where(t >= 2, pltpu.roll(u, 2, axis=0), p2)
        else:
            xp = jnp.concatenate([car[:, cols], u], axis=0)
            u1 = pltpu.roll(xp, 1, axis=0)[8:, :]
            u2 = pltpu.roll(xp, 2, axis=0)[8:, :]
            last = u[tm - 8:tm, :]
            car[:, cols] = last
            tail_ref[0, :, cols] = last
        dw = dw_ref[:, cols]
        return dw[0:1, :] * u2 + dw[1:2, :] * u1 + dw[2:3, :] * u + b_ref[:, cols]

    for c in range(FFN_NCHUNK):
        g = half(slice(c * FFN_CHUNK, (c + 1) * FFN_CHUNK))
        val = half(slice(D_FF + c * FFN_CHUNK, D_FF + (c + 1) * FFN_CHUNK))
        act_sc[:, c * FFN_CHUNK:(c + 1) * FFN_CHUNK] = (_gelu_twice(g) * val).astype(BF16)
    f = jnp.dot(act_sc[...], wdn_ref[...], preferred_element_type=F32)
    xo_ref[...] = x + _rms(f, post_ref[...])


def _ffn(x_in, pg, wup, dw8, dwb, wdn_half, post, nb, tm, st2=None, mix=None):
    seq8 = st2 is not None
    xmap = lambda b, i: (b * nt + i, 0)
    fix = lambda b, i: (0, 0)
    fix3 = lambda b, i: (0, 0, 0)
    once = pl.Buffered(1)

    def col(block):
        return lambda b, i: (b * nt + i, block)

    if mix == "even":
        o2, z2, c2, x2, gn, w_mix, post_mix = x_in
        in_specs = [pl.BlockSpec((tm, A_WIDTH), xmap),
                    pl.BlockSpec((tm, A_WIDTH), col(2)),
                    pl.BlockSpec((tm, B_WIDTH), xmap),
                    pl.BlockSpec((tm, D_MODEL), xmap),
                    pl.BlockSpec((1, A_DV), fix),
                    pl.BlockSpec((A_WIDTH + B_WIDTH, D_MODEL), fix, pipeline_mode=once),
                    pl.BlockSpec((1, D_MODEL), fix)]
        args = [o2, z2, c2, x2, gn, w_mix, post_mix]
    elif mix == "odd":
        oc2, z2, x2, lng, lnb, wsm, sbe, w_mix, post_mix = x_in
        in_specs = [pl.BlockSpec((tm, C_WIDTH), xmap),
                    pl.BlockSpec((tm, D_WIDTH), col(9)),
                    pl.BlockSpec((tm, D_WIDTH), col(10)),
                    pl.BlockSpec((1, D_WIDTH), fix),
                    pl.BlockSpec((1, D_WIDTH), fix),
                    pl.BlockSpec((4, D_CHUNK, D_CHUNK), fix3),
                    pl.BlockSpec((D_CHUNK, D_WIDTH), fix),
                    pl.BlockSpec((tm, D_MODEL), xmap),
                    pl.BlockSpec((C_WIDTH + D_WIDTH, D_MODEL), fix, pipeline_mode=once),
                    pl.BlockSpec((1, D_MODEL), fix)]
        args = [oc2, z2, z2, lng, lnb, wsm, sbe, x2, w_mix, post_mix]
    else:
        x2 = x_in
        in_specs = [pl.BlockSpec((tm, D_MODEL), xmap)]
        args = [x2]
    assert len(args) == N_MIX_REFS[mix]
    rows = x2.shape[0]
    nt = rows // (nb * tm)
    in_specs += [pl.BlockSpec((1, D_MODEL), fix),
                 pl.BlockSpec((D_MODEL, 2 * D_FF), fix, pipeline_mode=once),
                 pl.BlockSpec((8, 2 * D_FF), fix),
                 pl.BlockSpec((1, 2 * D_FF), fix),
                 pl.BlockSpec((D_FF, D_MODEL), fix, pipeline_mode=once),
                 pl.BlockSpec((1, D_MODEL), fix)]
    args += [pg, wup, dw8, dwb, wdn_half, post]
    scratch = [pltpu.VMEM((tm, D_FF), BF16)]
    if seq8:
        ns = tm // 8
        in_specs.append(pl.BlockSpec((ns, 4 * D_FF), xmap))
        args.append(st2)
        tail_spec = pl.BlockSpec((ns, 4 * D_FF), xmap)
        tail_shape = jax.ShapeDtypeStruct((rows // 8, 4 * D_FF), F32)
        scratch += [pltpu.VMEM((FFN_CHUNK // 128, tm, 128), F32)] * 3
    else:
        tail_spec = pl.BlockSpec((1, 8, 2 * D_FF), lambda b, i: (b * nt + i, 0, 0))
        tail_shape = jax.ShapeDtypeStruct((nb * nt, 8, 2 * D_FF), F32)
        scratch.append(pltpu.VMEM((8, 2 * D_FF), F32))
    body = functools.partial(_ffn_body, tm=tm, seq8=seq8, mix=mix)
    return pl.pallas_call(
        body,
        grid=(nb, nt),
        in_specs=in_specs,
        out_specs=[pl.BlockSpec((tm, D_MODEL), xmap), tail_spec],
        out_shape=[jax.ShapeDtypeStruct((rows, D_MODEL), F32), tail_shape],
        scratch_shapes=scratch,
        compiler_params=_params(2),
        name="ffn" + ("_seq8" if seq8 else "") + ("_" + mix if mix else ""),
    )(*args)


def _attn_p_body(q0, k0, v0, q1, k1, v1, q2, k2, v2, bias_ref, o_ref, kv0, kv1, kv2,
                 og, lg, *, L):
    lane = lax.broadcasted_iota(jnp.int32, (C_QBLOCK, 2 * C_DH), 1)
    first = lane < C_DH
    col = lax.broadcasted_iota(jnp.int32, (C_QBLOCK, 2 * C_QBLOCK), 1)
    groups = ((q0, k0, v0), (q1, k1, v1), (q2, k2, v2))
    for g, (qr, kr, vr) in enumerate(groups):
        kvr = (kv0, kv1, kv2)[g]
        wl = kvr.shape[-1]
        piece = min(wl, 512)
        for c0 in range(0, wl, piece):
            src = slice(L - wl + c0, L - wl + c0 + piece)
            kvr[0, 0, :, c0:c0 + piece] = kr[0, src, :].T
            kvr[0, 1, :, c0:c0 + piece] = vr[0, src, :].T
        d = C_PATTERNS[g][1]
        nblk = L // d // C_QBLOCK
        for res in range(d):
            for m in range(nblk):
                def rows_of(mm, res=res, d=d):
                    if d == 1:
                        return slice(mm * C_QBLOCK, (mm + 1) * C_QBLOCK)
                    return pl.ds(res + d * C_QBLOCK * mm, C_QBLOCK, stride=d)

                rq = rows_of(m)
                qb = qr[0, rq, :] * (C_DH ** -0.5)
                rp = rows_of(max(m - 1, 0))
                kcat = jnp.concatenate([kr[0, rp, :], kr[0, rq, :]], axis=0).astype(BF16)
                vcat = jnp.concatenate([vr[0, rp, :], vr[0, rq, :]], axis=0).astype(BF16)
                outs, lses = [], []
                for hh in range(2):
                    keep = first if hh == 0 else jnp.logical_not(first)
                    qm = jnp.where(keep, qb, 0.0).astype(BF16)
                    s = lax.dot_general(qm, kcat, NT_DIMS, preferred_element_type=F32)
                    s = s + bias_ref[g, hh]
                    if m == 0:
                        s = jnp.where(col < C_QBLOCK, NEG, s)
                    mx = jnp.max(s, axis=-1, keepdims=True)
                    p = jnp.exp(s - mx)
                    l = jnp.sum(p, axis=-1, keepdims=True)
                    r = jnp.dot(p.astype(BF16), vcat, preferred_element_type=F32)
                    outs.append(r / l)
                    lses.append(mx + jnp.log(l))
                og[g, rq, :] = jnp.where(first, outs[0], outs[1])
                lg[g, rq, :] = jnp.where(first, lses[0], lses[1])
    cr = 256
    for cidx in range(L // cr):
        rows = slice(cidx * cr, (cidx + 1) * cr)
        l0, l1, l2 = lg[0, rows, :], lg[1, rows, :], lg[2, rows, :]
        mx = jnp.maximum(jnp.maximum(l0, l1), l2)
        w0, w1, w2 = jnp.exp(l0 - mx), jnp.exp(l1 - mx), jnp.exp(l2 - mx)
        num = w0 * og[0, rows, :] + w1 * og[1, rows, :] + w2 * og[2, rows, :]
        o_ref[0, rows, :] = num / (w0 + w1 + w2)


def _attn_prompt(z3, biasmat):
    nb, L, _ = z3.shape
    lw = 2 * C_DH
    in_specs = []
    for g in range(3):
        for part in range(3):
            base = part * 6 + g * 2
            in_specs.append(pl.BlockSpec((1, L, lw), lambda b, p, base=base: (b, 0, base + p)))
    in_specs.append(pl.BlockSpec((3, 2, C_QBLOCK, 2 * C_QBLOCK), lambda b, p: (0, p, 0, 0)))
    out_specs = [pl.BlockSpec((1, L, lw), lambda b, p: (b, 0, p))]
    out_shape = [jax.ShapeDtypeStruct((nb, L, C_WIDTH), F32)]
    for window, _ in C_PATTERNS:
        wl = min(window, L)
        out_specs.append(pl.BlockSpec((1, 2, lw, wl), lambda b, p: (b, 0, p, 0)))
        out_shape.append(jax.ShapeDtypeStruct((nb, 2, C_WIDTH, wl), F32))
    body = functools.partial(_attn_p_body, L=L)
    return pl.pallas_call(
        body,
        grid=(nb, 2),
        in_specs=in_specs,
        out_specs=out_specs,
        out_shape=out_shape,
        scratch_shapes=[pltpu.VMEM((3, L, lw), F32)] * 2,
        compiler_params=_params(2),
        name="attn_prompt",
    )(*([z3] * 9), biasmat)


def _attn_s_body(z_ref, c0, c1, c2, ma0, ma1, ma2, mb_ref, o_ref, n0, n1, n2):
    nq = 8
    z = z_ref[0]
    hm = _head_mask(C_HEADS * nq, C_WIDTH, nq, C_DH)
    lane = lax.broadcasted_iota(jnp.int32, (2 * C_WIDTH, C_QBLOCK), 1)
    outs, lses = [], []
    for g, (cref, mref, nref) in enumerate(((c0, ma0, n0), (c1, ma1, n1), (c2, ma2, n2))):
        W = C_PATTERNS[g][0]
        q = z[:, g * C_WIDTH:(g + 1) * C_WIDTH] * (C_DH ** -0.5)
        kn = z[:, (3 + g) * C_WIDTH:(4 + g) * C_WIDTH]
        vn = z[:, (6 + g) * C_WIDTH:(7 + g) * C_WIDTH]
        new = jnp.concatenate([jnp.zeros((C_QBLOCK - nq, 2 * C_WIDTH), F32),
                               jnp.concatenate([kn, vn], axis=1)], axis=0)
        new_t = new.T
        old = cref[0]
        sh = pltpu.roll(old, W - nq, axis=1)
        if W > C_QBLOCK:
            nref[0, :, 0:W - C_QBLOCK] = sh[:, 0:W - C_QBLOCK]
        nref[0, :, W - C_QBLOCK:W] = jnp.where(lane >= C_QBLOCK - nq, new_t,
                                               sh[:, W - C_QBLOCK:W])
        qbd = jnp.where(hm, jnp.concatenate([q] * C_HEADS, axis=0), 0.0).astype(BF16)
        ka = old[0:C_WIDTH, :].astype(BF16)
        va = old[C_WIDTH:2 * C_WIDTH, :].astype(BF16)
        kb = new_t[0:C_WIDTH, :].astype(BF16)
        vb = new_t[C_WIDTH:2 * C_WIDTH, :].astype(BF16)
        sa = jnp.dot(qbd, ka, preferred_element_type=F32) + mref[...]
        sb = jnp.dot(qbd, kb, preferred_element_type=F32) + mb_ref[g]
        mx = jnp.maximum(jnp.max(sa, axis=-1, keepdims=True), jnp.max(sb, axis=-1, keepdims=True))
        pa = jnp.exp(sa - mx)
        pb = jnp.exp(sb - mx)
        l = jnp.sum(pa, axis=-1, keepdims=True) + jnp.sum(pb, axis=-1, keepdims=True)
        r = (lax.dot_general(pa.astype(BF16), va, NT_DIMS, preferred_element_type=F32)
             + lax.dot_general(pb.astype(BF16), vb, NT_DIMS, preferred_element_type=F32))
        outs.append(r / l)
        lses.append(mx + jnp.log(l))
    mx = jnp.maximum(jnp.maximum(lses[0], lses[1]), lses[2])
    ws = [jnp.exp(ls - mx) for ls in lses]
    o32 = (ws[0] * outs[0] + ws[1] * outs[1] + ws[2] * outs[2]) / (ws[0] + ws[1] + ws[2])
    o32 = jnp.where(hm, o32, 0.0)
    o_ref[0] = o32[0:nq] + o32[nq:2 * nq] + o32[2 * nq:3 * nq] + o32[3 * nq:4 * nq]


def _attn_sample(z3, caches, mas, mb):
    nb, nq, _ = z3.shape
    kvw = 2 * C_WIDTH
    in_specs = [pl.BlockSpec((1, nq, Z_WIDTH), lambda b: (b, 0, 0))]
    out_specs = [pl.BlockSpec((1, nq, C_WIDTH), lambda b: (b, 0, 0))]
    out_shape = [jax.ShapeDtypeStruct((nb, nq, C_WIDTH), F32)]
    for W, _ in C_PATTERNS:
        in_specs.append(pl.BlockSpec((1, kvw, W), lambda b: (b, 0, 0)))
        out_specs.append(pl.BlockSpec((1, kvw, W), lambda b: (b, 0, 0)))
        out_shape.append(jax.ShapeDtypeStruct((nb, kvw, W), F32))
    for W, _ in C_PATTERNS:
        in_specs.append(pl.BlockSpec((C_HEADS * nq, W), lambda b: (0, 0)))
    in_specs.append(pl.BlockSpec((3, C_HEADS * nq, C_QBLOCK), lambda b: (0, 0, 0)))
    return pl.pallas_call(
        _attn_s_body,
        grid=(nb,),
        in_specs=in_specs,
        out_specs=out_specs,
        out_shape=out_shape,
        compiler_params=_params(1),
        name="attn_sample",
    )(z3, *caches, *mas, mb)


def _odd_mix(oc_ref, du_ref, dv_ref, lng_ref, lnb_ref, ws_ref, sb_ref, x_ref, w_ref, post_ref):
    tm = x_ref.shape[0]
    u = _gelu(du_ref[...])
    vn = _layernorm(_gelu(dv_ref[...]), lng_ref[...], lnb_ref[...])
    lane_g = lax.broadcasted_iota(jnp.int32, (D_CHUNK, D_WIDTH), 1) >> int(math.log2(D_DH))
    parts = []
    for cidx in range(tm // D_CHUNK):
        vc = vn[cidx * D_CHUNK:(cidx + 1) * D_CHUNK, :]
        m = sb_ref[...]
        for g in range(D_WIDTH // D_DH):
            vm = jnp.where(lane_g == g, vc, 0.0).astype(BF16)
            m = m + jnp.dot(ws_ref[g], vm, preferred_element_type=F32)
        parts.append(m)
    od = u * jnp.concatenate(parts, axis=0)
    y = (jnp.dot(oc_ref[...].astype(BF16), w_ref[0:C_WIDTH, :], preferred_element_type=F32)
         + jnp.dot(od.astype(BF16), w_ref[C_WIDTH:C_WIDTH + D_WIDTH, :],
                   preferred_element_type=F32))
    return x_ref[...] + _rms(y, post_ref[...])


def _oddout_s_body(oc_ref, du_ref, dv_ref, lng_ref, lnb_ref, we_ref, sb_ref, x_ref, w_ref,
                   post_ref, xo_ref, vn_ref):
    nb, nq, _ = du_ref.shape
    u = _gelu(du_ref[...])
    vn = _layernorm(_gelu(dv_ref[...]), lng_ref[...], lnb_ref[...])
    vn_ref[...] = vn
    m = jnp.zeros((nb, nq, D_WIDTH), F32) + sb_ref[...]
    for s in range(nq):
        m = m + we_ref[s] * vn[:, s:s + 1, :]
    od = (u * m).reshape(nb * nq, D_WIDTH)
    y = (jnp.dot(oc_ref[...].astype(BF16), w_ref[0:C_WIDTH, :], preferred_element_type=F32)
         + jnp.dot(od.astype(BF16), w_ref[C_WIDTH:C_WIDTH + D_WIDTH, :],
                   preferred_element_type=F32))
    xo_ref[...] = x_ref[...] + _rms(y, post_ref[...])


def _oddout_sample(oc2, z3, x2, lng, lnb, wexp, sbe, w, post):
    nb, nq, _ = z3.shape
    rows = nb * nq
    return pl.pallas_call(
        _oddout_s_body,
        grid=(1,),
        in_specs=[pl.BlockSpec((rows, C_WIDTH), lambda i: (0, 0)),
                  pl.BlockSpec((nb, nq, D_WIDTH), lambda i: (0, 0, 9)),
                  pl.BlockSpec((nb, nq, D_WIDTH), lambda i: (0, 0, 10)),
                  pl.BlockSpec((1, D_WIDTH), lambda i: (0, 0)),
                  pl.BlockSpec((1, D_WIDTH), lambda i: (0, 0)),
                  pl.BlockSpec((nq, nq, D_WIDTH), lambda i: (0, 0, 0)),
                  pl.BlockSpec((nq, D_WIDTH), lambda i: (0, 0)),
                  pl.BlockSpec((rows, D_MODEL), lambda i: (0, 0)),
                  pl.BlockSpec((C_WIDTH + D_WIDTH, D_MODEL), lambda i: (0, 0)),
                  pl.BlockSpec((1, D_MODEL), lambda i: (0, 0))],
        out_specs=[pl.BlockSpec((rows, D_MODEL), lambda i: (0, 0)),
                   pl.BlockSpec((nb, nq, D_WIDTH), lambda i: (0, 0, 0))],
        out_shape=[jax.ShapeDtypeStruct((rows, D_MODEL), F32),
                   jax.ShapeDtypeStruct((nb, nq, D_WIDTH), F32)],
        compiler_params=_params(1),
        name="oddout_sample",
    )(oc2, z3, z3, lng, lnb, wexp, sbe, x2, w, post)


def _t5_bucket(dist):
    max_exact = N_BUCKETS // 2
    d32 = jnp.maximum(dist, 1).astype(F32)
    large = max_exact + (jnp.log(d32 / max_exact) / math.log(MAX_DIST / max_exact)
                         * (N_BUCKETS - max_exact)).astype(jnp.int32)
    large = jnp.minimum(large, N_BUCKETS - 1)
    return jnp.where(dist < max_exact, dist, large)


def _step_bias(rel_bias, g):
    window, dil = C_PATTERNS[g]
    j = jnp.arange(window // dil + 1, dtype=jnp.int32)
    return rel_bias[_t5_bucket(dil * j), g * C_HEADS:(g + 1) * C_HEADS].T.astype(F32)


def _skew(v, rows):
    n = v.shape[-1]
    lead = v.shape[:-1]
    t = jnp.broadcast_to(v[..., None, :], lead + (rows, n)).reshape(lead + (rows * n,))
    return t[..., :rows * (n - 1)].reshape(lead + (rows, n - 1))


def _prompt_bias(rel_bias):
    mats = []
    for g in range(len(C_PATTERNS)):
        bj = _step_bias(rel_bias, g)
        v = jnp.concatenate([bj[:, ::-1], jnp.full((C_HEADS, C_QBLOCK), NEG, F32)], axis=1)
        mats.append(_skew(v, C_QBLOCK))
    return jnp.stack(mats, axis=0)


def _sample_bias(rel_bias, nq):
    mas, mbs = [], []
    i = np.arange(nq)[:, None]
    i2 = np.arange(nq)[None, :]
    for g, (window, dil) in enumerate(C_PATTERNS):
        bj = _step_bias(rel_bias, g)
        on_grid = jnp.asarray(np.arange(window + 1) % dil == 0)
        bd = jnp.where(on_grid[None], jnp.repeat(bj, dil, axis=1)[:, :window + 1], NEG)
        e = jnp.concatenate([bd[:, ::-1], jnp.full((C_HEADS, nq - 1), NEG, F32)], axis=1)
        ma = _skew(e, nq)[:, :, :window]
        mas.append(ma.reshape(C_HEADS * nq, window))
        dist = i - i2
        ok = jnp.asarray((dist >= 0) & (dist % dil == 0))
        mb = jnp.where(ok[None], bd[:, np.maximum(dist, 0)], NEG)
        mb = jnp.pad(mb, ((0, 0), (0, 0), (C_QBLOCK - nq, 0)), constant_values=NEG)
        mbs.append(mb.reshape(C_HEADS * nq, C_QBLOCK))
    return mas, jnp.stack(mbs, axis=0)


def _row(v):
    return v.reshape(1, -1).astype(F32)


def kernel(x_prompt, x_sample, state_gla, state_conv_b, cache_c_w128, cache_c_w512, cache_c_w2048,
           state_ffn_conv, norm_pre_mix, norm_post_mix, norm_pre_ffn, norm_post_ffn, w_in_even,
           w_gate2, b_gate, gla_norm, conv_b_w, conv_b_b, ln_b_g, ln_b_b, w_out_even, w_in_odd,
           rel_bias, sgu_ln_g, sgu_ln_b, sgu_w, sgu_b, w_out_odd, w_up, ffn_dw_w, ffn_dw_b, w_down):
    nbp, lp, d = x_prompt.shape
    nbs, ls, _ = x_sample.shape

    we = w_in_even[0]
    split = 2 * A_QK + A_WIDTH
    w_e = jnp.concatenate(
        [we[:, :split], we[:, split + A_GATE_RANK:],
         we[:, split:split + A_GATE_RANK],
         jnp.zeros((d, A_QK - A_GATE_RANK), F32)], axis=1).astype(BF16)
    wg2p = jnp.zeros((A_QK, A_QK), F32).at[:A_GATE_RANK].set(w_gate2[0]).astype(BF16)
    segb = jnp.asarray(np.kron(np.eye(A_HEADS), np.ones((A_DK, A_DV))), BF16)
    cw = jnp.zeros((CONV_HDR, B_WIDTH), F32).at[:B_CONV].set(conv_b_w[0])
    w_oe = w_out_even[0].astype(BF16)
    w_o = w_in_odd[0].astype(BF16)
    w_oo = w_out_odd[0].astype(BF16)
    w_up_b = w_up.astype(BF16)
    w_dn_b = (0.5 * w_down).astype(BF16)
    dw8 = jnp.zeros((2, 8, 2 * D_FF), F32).at[:, :3].set(ffn_dw_w)
    tril = jnp.tril(jnp.ones((D_CHUNK, D_CHUNK), F32))
    sgu_wm = sgu_w[0] * tril
    bias_p = _prompt_bias(rel_bias)
    mas, mbs = _sample_bias(rel_bias, ls)

    def ffn_params(layer):
        return (_row(norm_pre_ffn[layer]), w_up_b[layer], dw8[layer], _row(ffn_dw_b[layer]),
                w_dn_b[layer], _row(norm_post_ffn[layer]))

    def even_layer(x2, nb, L, s0t, hdr, conv_tm, nseq):
        z = _inproj(x2, _row(norm_pre_mix[0]), w_e, 1024 if nb * L > 1024 else 512)
        z3 = z.reshape(nb, L, Z_WIDTH)
        o, st, c, ut = _mixer0(z3, wg2p, _row(b_gate[0]), segb, s0t, hdr, cw, _row(conv_b_b[0]),
                               _row(ln_b_g[0]), _row(ln_b_b[0]), conv_tm, nseq)
        mix_in = (o.reshape(nb * L, A_WIDTH), z, c.reshape(nb * L, B_WIDTH), x2,
                  _row(gla_norm[0]), w_oe, _row(norm_post_mix[0]))
        return mix_in, st.reshape(nb, A_HEADS, A_DK, A_DV), ut

    def kv_window(t):
        return t.reshape(t.shape[0], 2, C_HEADS, C_DH, t.shape[-1]).transpose(0, 4, 1, 2, 3)[None]

    xp2 = x_prompt.reshape(nbp * lp, d)
    mix0, p_gla, ut = even_layer(xp2, nbp, lp, jnp.zeros((nbp, A_QK, A_DV), F32),
                                 jnp.zeros((nbp, CONV_HDR, B_WIDTH), F32), 512, 1)
    p_conv_b = ut[:, CONV_HDR - (B_CONV - 1):]
    x2, tail0 = _ffn(mix0, *ffn_params(0), nbp, 512, mix="even")
    z = _inproj(x2, _row(norm_pre_mix[1]), w_o, 1024)
    z3 = z.reshape(nbp, lp, Z_WIDTH)
    oc, *p_kvt = _attn_prompt(z3, bias_p)
    wsm = sgu_wm.astype(BF16)
    sbe = jnp.repeat(sgu_b[0].T, D_DH, axis=-1)
    mix1 = (oc.reshape(nbp * lp, C_WIDTH), z, x2, _row(sgu_ln_g[0]), _row(sgu_ln_b[0]), wsm, sbe,
            w_oo, _row(norm_post_mix[1]))
    x4, tail1 = _ffn(mix1, *ffn_params(1), nbp, 512, mix="odd")
    y_prompt = x4.reshape(nbp, lp, d)
    p_kv = [kv_window(t) for t in p_kvt]
    p_ffn = jnp.stack([t.reshape(nbp, -1, 8, 2 * D_FF)[:, -1, 6:8] for t in (tail0, tail1)], axis=0)

    xs2 = x_sample.reshape(nbs * ls, d)
    hdr = jnp.pad(state_conv_b[0], ((0, 0), (CONV_HDR - (B_CONV - 1), 0), (0, 0)))
    smix0, s_gla, us = even_layer(xs2, nbs, ls, state_gla[0].reshape(nbs, A_QK, A_DV), hdr, ls,
                                  SAMPLE_NSEQ)
    s_conv_b = jnp.concatenate([state_conv_b[0][:, ls:], us], axis=1)

    def ffn_sample(x_in, layer, mix):
        st2 = state_ffn_conv[layer].reshape(nbs, 4 * D_FF)
        xo, tail = _ffn(x_in, *ffn_params(layer), 1, 256, st2, mix)
        return xo, tail.reshape(nbs, 2, 2 * D_FF)

    y2, s_ffn0 = ffn_sample(smix0, 0, "even")
    zs = _inproj(y2, _row(norm_pre_mix[1]), w_o, 512)
    zs3 = zs.reshape(nbs, ls, Z_WIDTH)
    caches = [c[0].transpose(0, 2, 3, 4, 1).reshape(nbs, 2 * C_WIDTH, c.shape[2])
              for c in (cache_c_w128, cache_c_w512, cache_c_w2048)]
    ocs, n0, n1, n2 = _attn_sample(zs3, caches, mas, mbs)
    wexp = jnp.repeat(sgu_wm[:, :ls, :ls].transpose(2, 1, 0), D_DH, axis=-1)
    sbes = jnp.repeat(sgu_b[0][:, :ls].T, D_DH, axis=-1)
    y3, s_sgu_v = _oddout_sample(ocs.reshape(nbs * ls, C_WIDTH), zs3, y2, _row(sgu_ln_g[0]),
                                 _row(sgu_ln_b[0]), wexp, sbes, w_oo, _row(norm_post_mix[1]))
    y4, s_ffn1 = ffn_sample(y3, 1, None)
    y_sample = y4.reshape(nbs, ls, d)
    s_kv = [kv_window(n.reshape(nbs, 2, C_WIDTH, n.shape[2])) for n in (n0, n1, n2)]
    s_ffn = jnp.stack([s_ffn0, s_ffn1], axis=0)

    return (y_prompt, y_sample, p_gla[None], p_conv_b[None], p_kv[0], p_kv[1], p_kv[2], p_ffn,
            s_gla[None], s_conv_b[None], s_kv[0], s_kv[1], s_kv[2], s_sgu_v[None], s_ffn)
```

```python
import functools
import math

import numpy as np
import jax
import jax.numpy as jnp
from jax import lax
from jax.experimental import pallas as pl
from jax.experimental.pallas import tpu as pltpu

F32 = jnp.float32
BF16 = jnp.bfloat16

D_MODEL = 1024
EPS = 1e-6
NEG = -1e30

A_HEADS = 4
A_DK = 64
A_DV = 128
A_QK = A_HEADS * A_DK
A_WIDTH = A_HEADS * A_DV
A_GATE_RANK = 16
A_GATE_NORM = 16.0
GLA_STEP = 16
MIX0_PIECE = 128
SAMPLE_NSEQ = 8
B_WIDTH = 512
B_CONV = 31
CONV_HDR = 32
C_PATTERNS = ((128, 1), (512, 4), (2048, 16))
C_HEADS = 4
C_DH = 64
C_WIDTH = C_HEADS * C_DH
C_QBLOCK = 128
N_BUCKETS = 32
MAX_DIST = 2048
D_WIDTH = 256
D_DH = 64
D_CHUNK = 128
D_FF = 2816
FFN_CHUNK = 256
FFN_NCHUNK = D_FF // FFN_CHUNK
Z_WIDTH = 2816

VMEM_LIMIT_BYTES = 56 * 1024 * 1024

NT_DIMS = (((1,), (1,)), ((), ()))
TN_DIMS = (((0,), (0,)), ((), ()))


def _params(n_axes):
    return pltpu.CompilerParams(dimension_semantics=("arbitrary",) * n_axes,
                                vmem_limit_bytes=VMEM_LIMIT_BYTES)


def _rms(x, g):
    return x * lax.rsqrt(jnp.mean(x * x, axis=-1, keepdims=True) + EPS) * g


def _layernorm(x, g, b):
    mu = jnp.mean(x, axis=-1, keepdims=True)
    xc = x - mu
    var = jnp.mean(xc * xc, axis=-1, keepdims=True)
    return xc * lax.rsqrt(var + EPS) * g + b


def _sigmoid(x):
    return 1.0 / (1.0 + jnp.exp(-x))


def _head_mask(rows, lanes, rows_per_head, lanes_per_head):
    r = lax.broadcasted_iota(jnp.int32, (rows, lanes), 0) >> int(math.log2(rows_per_head))
    c = lax.broadcasted_iota(jnp.int32, (rows, lanes), 1) >> int(math.log2(lanes_per_head))
    return r == c


def _gelu(x):
    c = math.sqrt(2.0 / math.pi)
    return 0.5 * x * (1.0 + jnp.tanh(c * (x + 0.044715 * (x * x * x))))


def _inproj_body(x_ref, g_ref, w_ref, o_ref):
    h = _rms(x_ref[...], g_ref[...])
    o_ref[...] = jnp.dot(h.astype(BF16), w_ref[...], preferred_element_type=F32)


def _inproj(x2, g, w, tm):
    rows, d = x2.shape
    n = w.shape[1]
    return pl.pallas_call(
        _inproj_body,
        grid=(rows // tm,),
        in_specs=[pl.BlockSpec((tm, d), lambda i: (i, 0)),
                  pl.BlockSpec((1, d), lambda i: (0, 0)),
                  pl.BlockSpec((d, n), lambda i: (0, 0))],
        out_specs=pl.BlockSpec((tm, n), lambda i: (i, 0)),
        out_shape=jax.ShapeDtypeStruct((rows, n), F32),
        compiler_params=_params(1),
        name="inproj",
    )(x2, g, w)


def _mixer0_body(q_ref, k_ref, v_ref, glr_ref, ga_ref, gg_ref, wg_ref, bg_ref, segb_ref, s0_ref,
                 hdr_ref, cw_ref, cb_ref, lng_ref, lnb_ref,
                 o_ref, sn_ref, c_ref, ut_ref, b_sc, st_sc, xp, *, tm, SB, nseq, tail, piece):
    j = pl.program_id(1)
    last_j = pl.num_programs(1) - 1

    glr = glr_ref[...].reshape(nseq * tm, A_QK)
    xg = jnp.dot(glr.astype(BF16), wg_ref[...], preferred_element_type=F32) + bg_ref[...]
    la = (jnp.minimum(xg, 0.0) - jnp.log(1.0 + jnp.exp(-jnp.abs(xg)))) * (1.0 / A_GATE_NORM)
    tl = lax.broadcasted_iota(jnp.int32, (nseq * tm, A_QK), 0) & (SB - 1)
    sh = 1
    while sh < SB:
        la = la + jnp.where(tl >= sh, pltpu.roll(la, sh, axis=0), 0.0)
        sh *= 2
    b_sc[...] = la.reshape(nseq, tm, A_QK)

    for si in range(nseq):
        xs = xp.at[si]

        @pl.when(j == 0)
        def _(xs=xs, si=si):
            st_sc[si] = s0_ref[si].T
            xs[0:CONV_HDR, :] = hdr_ref[si]
            xs[CONV_HDR + tm:CONV_HDR + tm + 8, :] = jnp.zeros((8, B_WIDTH), F32)

        u = ga_ref[si] * _sigmoid(gg_ref[si])
        xs[CONV_HDR:CONV_HDR + tm, :] = u
        ut_ref[si] = u[tm - tail:tm, :]

    hm = _head_mask(A_HEADS * SB, A_QK, SB, A_DK)
    tri = lax.broadcasted_iota(jnp.int32, (8, A_QK), 0)
    ntile = SB // 8

    def gla_step(si, rows):
        q = q_ref[si, rows, :] * (A_DK ** -0.5)
        k = k_ref[si, rows, :]
        v = v_ref[si, rows, :]
        b = b_sc[si, rows, :]
        bl = b[SB - 1:SB, :]
        qt = q * jnp.exp(b)
        kh = k * jnp.exp(bl - b)
        st = st_sc[si]
        qbd = jnp.where(hm, jnp.concatenate([qt] * A_HEADS, axis=0), 0.0).astype(BF16)
        o_int = lax.dot_general(qbd, st.astype(BF16), NT_DIMS, preferred_element_type=F32)
        kbd = jnp.where(hm, jnp.concatenate([kh] * A_HEADS, axis=0), 0.0).astype(BF16)
        vst = jnp.concatenate([v[:, h * A_DV:(h + 1) * A_DV] for h in range(A_HEADS)],
                              axis=0).astype(BF16)
        upd = lax.dot_general(vst, kbd, TN_DIMS, preferred_element_type=F32)
        st_sc[si] = st * jnp.exp(bl) + upd
        ps = []
        for s in range(SB):
            lo = 8 * (s // 8)
            bs = b[s:s + 1, :]
            ks = k[s:s + 1, :]
            dd = b[lo:lo + 8, :] - bs
            if s % 8:
                dd = jnp.where(tri >= s % 8, dd, NEG)
            ps.append(q[lo:lo + 8, :] * jnp.exp(dd) * ks)
            if lo + 8 < SB:
                ps.append(q[lo + 8:SB, :] * jnp.exp(b[lo + 8:SB, :] - bs) * ks)
        pall = jnp.concatenate(ps, axis=0).astype(BF16)
        r = jnp.dot(pall, segb_ref[...], preferred_element_type=F32)
        od = [None] * ntile
        off = 0
        for s in range(SB):
            for tq in range(s // 8, ntile):
                term = r[off:off + 8, :] * v[s:s + 1, :]
                od[tq] = term if od[tq] is None else od[tq] + term
                off += 8
        o = (jnp.concatenate([o_int[h * SB:(h + 1) * SB] for h in range(A_HEADS)], axis=1)
             + jnp.concatenate(od, axis=0))
        o_ref[si, rows, :] = o

    def conv_rows(si, r0, static):
        xs = xp.at[si]
        n = piece + 8
        ys = []
        for cc in range(B_WIDTH // 128):
            lanes = slice(128 * cc, 128 * (cc + 1))
            acc = None
            for r in range(7, -1, -1):
                z = None
                for q in range(5):
                    kk = 8 * q + r - (CONV_HDR - (B_CONV - 1))
                    if 0 <= kk < B_CONV:
                        start = 8 * q + r0 if static else pl.multiple_of(8 * q + r0, 8)
                        term = cw_ref[kk:kk + 1, lanes] * xs[pl.ds(start, n), lanes]
                        z = term if z is None else z + term
                acc = z if acc is None else z + pltpu.roll(acc, n - 1, axis=0)
            ys.append(acc[0:piece, :] + cb_ref[:, lanes])
        y = _layernorm(jnp.concatenate(ys, axis=1), lng_ref[...], lnb_ref[...])
        c_ref[si, pl.ds(r0, piece), :] = y * _sigmoid(y)

    npiece = tm // piece
    if npiece == 1:
        for si in range(nseq):
            for s in range(piece // SB):
                gla_step(si, slice(s * SB, (s + 1) * SB))
            conv_rows(si, 0, True)
    else:
        def do_piece(p, carry):
            r0 = pl.multiple_of(p * piece, piece)
            for si in range(nseq):
                for s in range(piece // SB):
                    gla_step(si, pl.ds(pl.multiple_of(r0 + s * SB, SB), SB))
                conv_rows(si, r0, False)
            return carry

        lax.fori_loop(0, npiece, do_piece, 0)

    for si in range(nseq):
        xs = xp.at[si]
        nxt = xs[tm:tm + CONV_HDR, :]
        xs[0:CONV_HDR, :] = nxt

        @pl.when(j == last_j)
        def _(si=si):
            sn_ref[si] = st_sc[si].T


def _mixer0(z3, wg2p, bgate, segb, s0, hdr, cw, cb, lng, lnb, tm, nseq):
    nb, L, _ = z3.shape
    assert nseq == 1 or L == tm
    SB = GLA_STEP if tm % GLA_STEP == 0 else tm
    piece = min(tm, MIX0_PIECE)
    tail = min(CONV_HDR, tm)
    fix = lambda b, j: (0, 0)

    def zcol(block):
        return lambda b, j: (b, j, block)

    body = functools.partial(_mixer0_body, tm=tm, SB=SB, nseq=nseq, tail=tail, piece=piece)
    return pl.pallas_call(
        body,
        grid=(nb // nseq, L // tm),
        in_specs=[pl.BlockSpec((nseq, tm, A_QK), zcol(0)),
                  pl.BlockSpec((nseq, tm, A_QK), zcol(1)),
                  pl.BlockSpec((nseq, tm, A_WIDTH), zcol(1)),
                  pl.BlockSpec((nseq, tm, A_QK), zcol(10)),
                  pl.BlockSpec((nseq, tm, B_WIDTH), zcol(3)),
                  pl.BlockSpec((nseq, tm, B_WIDTH), zcol(4)),
                  pl.BlockSpec((A_QK, A_QK), fix),
                  pl.BlockSpec((1, A_QK), fix),
                  pl.BlockSpec((A_QK, A_WIDTH), fix),
                  pl.BlockSpec((nseq, A_QK, A_DV), lambda b, j: (b, 0, 0)),
                  pl.BlockSpec((nseq, CONV_HDR, B_WIDTH), lambda b, j: (b, 0, 0)),
                  pl.BlockSpec((CONV_HDR, B_WIDTH), fix),
                  pl.BlockSpec((1, B_WIDTH), fix),
                  pl.BlockSpec((1, B_WIDTH), fix),
                  pl.BlockSpec((1, B_WIDTH), fix)],
        out_specs=[pl.BlockSpec((nseq, tm, A_WIDTH), lambda b, j: (b, j, 0)),
                   pl.BlockSpec((nseq, A_QK, A_DV), lambda b, j: (b, 0, 0)),
                   pl.BlockSpec((nseq, tm, B_WIDTH), lambda b, j: (b, j, 0)),
                   pl.BlockSpec((nseq, tail, B_WIDTH), lambda b, j: (b, 0, 0))],
        out_shape=[jax.ShapeDtypeStruct((nb, L, A_WIDTH), F32),
                   jax.ShapeDtypeStruct((nb, A_QK, A_DV), F32),
                   jax.ShapeDtypeStruct((nb, L, B_WIDTH), F32),
                   jax.ShapeDtypeStruct((nb, tail, B_WIDTH), F32)],
        scratch_shapes=[pltpu.VMEM((nseq, tm, A_QK), F32),
                        pltpu.VMEM((nseq, A_DV, A_QK), F32),
                        pltpu.VMEM((nseq, CONV_HDR + tm + 8, B_WIDTH), F32)],
        compiler_params=_params(2),
        name="mixer0",
    )(z3, z3, z3, z3, z3, z3, wg2p, bgate, segb, s0, hdr, cw, cb, lng, lnb)


def _even_mix(o_ref, r_ref, c_ref, x_ref, gn_ref, w_ref, post_ref):
    o = o_ref[...]
    gn = gn_ref[...]
    oa = jnp.concatenate([_rms(o[:, h * A_DV:(h + 1) * A_DV], gn) for h in range(A_HEADS)], axis=1)
    r = r_ref[...]
    oa = oa * (r * _sigmoid(r))
    y = (jnp.dot(oa.astype(BF16), w_ref[0:A_WIDTH, :], preferred_element_type=F32)
         + jnp.dot(c_ref[...].astype(BF16), w_ref[A_WIDTH:A_WIDTH + B_WIDTH, :],
                   preferred_element_type=F32))
    return x_ref[...] + _rms(y, post_ref[...])


def _gelu_twice(x):
    c = math.sqrt(2.0 / math.pi)
    return x + x * jnp.tanh(x * (c + (c * 0.044715) * (x * x)))


N_MIX_REFS = {None: 1, "even": 7, "odd": 10}


def _ffn_body(*refs, tm, seq8, mix):
    head, refs = refs[:N_MIX_REFS[mix]], refs[N_MIX_REFS[mix]:]
    if seq8:
        (pg_ref, wup_ref, dw_ref, b_ref, wdn_ref, post_ref, st_ref,
         xo_ref, tail_ref, act_sc, p1_sc, p2_sc, u_sc) = refs
    else:
        (pg_ref, wup_ref, dw_ref, b_ref, wdn_ref, post_ref,
         xo_ref, tail_ref, act_sc, car) = refs
    i = pl.program_id(1)
    ns = tm // 8

    @pl.when(i == 0)
    def _():
        if seq8:
            p1_sc[...] = jnp.zeros(p1_sc.shape, F32)
            p2_sc[...] = jnp.zeros(p2_sc.shape, F32)
        else:
            car[...] = jnp.zeros((8, 2 * D_FF), F32)

    if mix == "even":
        x = _even_mix(*head)
    elif mix == "odd":
        x = _odd_mix(*head)
    else:
        x = head[0][...]
    h = _rms(x, pg_ref[...]).astype(BF16)
    t = lax.broadcasted_iota(jnp.int32, (tm, FFN_CHUNK), 0) & 7

    def half(cols):
        u = jnp.dot(h, wup_ref[:, cols], preferred_element_type=F32)
        if seq8:
            for hf in range(FFN_CHUNK // 128):
                lo = cols.start + 128 * hf
                c0 = slice(lo, lo + 128)
                c1 = slice(2 * D_FF + lo, 2 * D_FF + lo + 128)
                p1_sc[hf, pl.ds(0, ns, stride=8), :] = st_ref[:, c1]
                p2_sc[hf, pl.ds(0, ns, stride=8), :] = st_ref[:, c0]
                p2_sc[hf, pl.ds(1, ns, stride=8), :] = st_ref[:, c1]
                u_sc[hf] = u[:, 128 * hf:128 * (hf + 1)]
                tail_ref[:, c0] = u_sc[hf, pl.ds(6, ns, stride=8), :]
                tail_ref[:, c1] = u_sc[hf, pl.ds(7, ns, stride=8), :]
            p1 = jnp.concatenate([p1_sc[hf] for hf in range(FFN_CHUNK // 128)], axis=1)
            p2 = jnp.concatenate([p2_sc[hf] for hf in range(FFN_CHUNK // 128)], axis=1)
            u1 = jnp.where(t >= 1, pltpu.roll(u, 1, axis=0), p1)
            u2 = jnp.where(t >= 2, pltpu.roll(u, 2, axis=0), p2)
        else:
            xp = jnp.concatenate([car[:, cols], u], axis=0)
            u1 = pltpu.roll(xp, 1, axis=0)[8:, :]
            u2 = pltpu.roll(xp, 2, axis=0)[8:, :]
            last = u[tm - 8:tm, :]
            car[:, cols] = last
            tail_ref[0, :, cols] = last
        dw = dw_ref[:, cols]
        return dw[0:1, :] * u2 + dw[1:2, :] * u1 + dw[2:3, :] * u + b_ref[:, cols]

    for c in range(FFN_NCHUNK):
        g = half(slice(c * FFN_CHUNK, (c + 1) * FFN_CHUNK))
        val = half(slice(D_FF + c * FFN_CHUNK, D_FF + (c + 1) * FFN_CHUNK))
        act_sc[:, c * FFN_CHUNK:(c + 1) * FFN_CHUNK] = (_gelu_twice(g) * val).astype(BF16)
    f = jnp.dot(act_sc[...], wdn_ref[...], preferred_element_type=F32)
    xo_ref[...] = x + _rms(f, post_ref[...])


def _ffn(x_in, pg, wup, dw8, dwb, wdn_half, post, nb, tm, st2=None, mix=None):
    seq8 = st2 is not None
    xmap = lambda b, i: (b * nt + i, 0)
    fix = lambda b, i: (0, 0)
    fix3 = lambda b, i: (0, 0, 0)
    once = pl.Buffered(1)

    def col(block):
        return lambda b, i: (b * nt + i, block)

    if mix == "even":
        o2, z2, c2, x2, gn, w_mix, post_mix = x_in
        in_specs = [pl.BlockSpec((tm, A_WIDTH), xmap),
                    pl.BlockSpec((tm, A_WIDTH), col(2)),
                    pl.BlockSpec((tm, B_WIDTH), xmap),
                    pl.BlockSpec((tm, D_MODEL), xmap),
                    pl.BlockSpec((1, A_DV), fix),
                    pl.BlockSpec((A_WIDTH + B_WIDTH, D_MODEL), fix, pipeline_mode=once),
                    pl.BlockSpec((1, D_MODEL), fix)]
        args = [o2, z2, c2, x2, gn, w_mix, post_mix]
    elif mix == "odd":
        oc2, z2, x2, lng, lnb, wsm, sbe, w_mix, post_mix = x_in
        in_specs = [pl.BlockSpec((tm, C_WIDTH), xmap),
                    pl.BlockSpec((tm, D_WIDTH), col(9)),
                    pl.BlockSpec((tm, D_WIDTH), col(10)),
                    pl.BlockSpec((1, D_WIDTH), fix),
                    pl.BlockSpec((1, D_WIDTH), fix),
                    pl.BlockSpec((4, D_CHUNK, D_CHUNK), fix3),
                    pl.BlockSpec((D_CHUNK, D_WIDTH), fix),
                    pl.BlockSpec((tm, D_MODEL), xmap),
                    pl.BlockSpec((C_WIDTH + D_WIDTH, D_MODEL), fix, pipeline_mode=once),
                    pl.BlockSpec((1, D_MODEL), fix)]
        args = [oc2, z2, z2, lng, lnb, wsm, sbe, x2, w_mix, post_mix]
    else:
        x2 = x_in
        in_specs = [pl.BlockSpec((tm, D_MODEL), xmap)]
        args = [x2]
    assert len(args) == N_MIX_REFS[mix]
    rows = x2.shape[0]
    nt = rows // (nb * tm)
    in_specs += [pl.BlockSpec((1, D_MODEL), fix),
                 pl.BlockSpec((D_MODEL, 2 * D_FF), fix, pipeline_mode=once),
                 pl.BlockSpec((8, 2 * D_FF), fix),
                 pl.BlockSpec((1, 2 * D_FF), fix),
                 pl.BlockSpec((D_FF, D_MODEL), fix, pipeline_mode=once),
                 pl.BlockSpec((1, D_MODEL), fix)]
    args += [pg, wup, dw8, dwb, wdn_half, post]
    scratch = [pltpu.VMEM((tm, D_FF), BF16)]
    if seq8:
        ns = tm // 8
        in_specs.append(pl.BlockSpec((ns, 4 * D_FF), xmap))
        args.append(st2)
        tail_spec = pl.BlockSpec((ns, 4 * D_FF), xmap)
        tail_shape = jax.ShapeDtypeStruct((rows // 8, 4 * D_FF), F32)
        scratch += [pltpu.VMEM((FFN_CHUNK // 128, tm, 128), F32)] * 3
    else:
        tail_spec = pl.BlockSpec((1, 8, 2 * D_FF), lambda b, i: (b * nt + i, 0, 0))
        tail_shape = jax.ShapeDtypeStruct((nb * nt, 8, 2 * D_FF), F32)
        scratch.append(pltpu.VMEM((8, 2 * D_FF), F32))
    body = functools.partial(_ffn_body, tm=tm, seq8=seq8, mix=mix)
    return pl.pallas_call(
        body,
        grid=(nb, nt),
        in_specs=in_specs,
        out_specs=[pl.BlockSpec((tm, D_MODEL), xmap), tail_spec],
        out_shape=[jax.ShapeDtypeStruct((rows, D_MODEL), F32), tail_shape],
        scratch_shapes=scratch,
        compiler_params=_params(2),
        name="ffn" + ("_seq8" if seq8 else "") + ("_" + mix if mix else ""),
    )(*args)


def _attn_p_body(q0, k0, v0, q1, k1, v1, q2, k2, v2, bias_ref, o_ref, kv0, kv1, kv2,
                 og, lg, *, L):
    lane = lax.broadcasted_iota(jnp.int32, (C_QBLOCK, 2 * C_DH), 1)
    first = lane < C_DH
    col = lax.broadcasted_iota(jnp.int32, (C_QBLOCK, 2 * C_QBLOCK), 1)
    groups = ((q0, k0, v0), (q1, k1, v1), (q2, k2, v2))
    for g, (qr, kr, vr) in enumerate(groups):
        kvr = (kv0, kv1, kv2)[g]
        wl = kvr.shape[-1]
        piece = min(wl, 512)
        for c0 in range(0, wl, piece):
            src = slice(L - wl + c0, L - wl + c0 + piece)
            kvr[0, 0, :, c0:c0 + piece] = kr[0, src, :].T
            kvr[0, 1, :, c0:c0 + piece] = vr[0, src, :].T
        d = C_PATTERNS[g][1]
        nblk = L // d // C_QBLOCK
        for res in range(d):
            for m in range(nblk):
                def rows_of(mm, res=res, d=d):
                    if d == 1:
                        return slice(mm * C_QBLOCK, (mm + 1) * C_QBLOCK)
                    return pl.ds(res + d * C_QBLOCK * mm, C_QBLOCK, stride=d)

                rq = rows_of(m)
                qb = qr[0, rq, :] * (C_DH ** -0.5)
                rp = rows_of(max(m - 1, 0))
                kcat = jnp.concatenate([kr[0, rp, :], kr[0, rq, :]], axis=0).astype(BF16)
                vcat = jnp.concatenate([vr[0, rp, :], vr[0, rq, :]], axis=0).astype(BF16)
                outs, lses = [], []
                for hh in range(2):
                    keep = first if hh == 0 else jnp.logical_not(first)
                    qm = jnp.where(keep, qb, 0.0).astype(BF16)
                    s = lax.dot_general(qm, kcat, NT_DIMS, preferred_element_type=F32)
                    s = s + bias_ref[g, hh]
                    if m == 0:
                        s = jnp.where(col < C_QBLOCK, NEG, s)
                    mx = jnp.max(s, axis=-1, keepdims=True)
                    p = jnp.exp(s - mx)
                    l = jnp.sum(p, axis=-1, keepdims=True)
                    r = jnp.dot(p.astype(BF16), vcat, preferred_element_type=F32)
                    outs.append(r / l)
                    lses.append(mx + jnp.log(l))
                og[g, rq, :] = jnp.where(first, outs[0], outs[1])
                lg[g, rq, :] = jnp.where(first, lses[0], lses[1])
    cr = 256
    for cidx in range(L // cr):
        rows = slice(cidx * cr, (cidx + 1) * cr)
        l0, l1, l2 = lg[0, rows, :], lg[1, rows, :], lg[2, rows, :]
        mx = jnp.maximum(jnp.maximum(l0, l1), l2)
        w0, w1, w2 = jnp.exp(l0 - mx), jnp.exp(l1 - mx), jnp.exp(l2 - mx)
        num = w0 * og[0, rows, :] + w1 * og[1, rows, :] + w2 * og[2, rows, :]
        o_ref[0, rows, :] = num / (w0 + w1 + w2)


def _attn_prompt(z3, biasmat):
    nb, L, _ = z3.shape
    lw = 2 * C_DH
    in_specs = []
    for g in range(3):
        for part in range(3):
            base = part * 6 + g * 2
            in_specs.append(pl.BlockSpec((1, L, lw), lambda b, p, base=base: (b, 0, base + p)))
    in_specs.append(pl.BlockSpec((3, 2, C_QBLOCK, 2 * C_QBLOCK), lambda b, p: (0, p, 0, 0)))
    out_specs = [pl.BlockSpec((1, L, lw), lambda b, p: (b, 0, p))]
    out_shape = [jax.ShapeDtypeStruct((nb, L, C_WIDTH), F32)]
    for window, _ in C_PATTERNS:
        wl = min(window, L)
        out_specs.append(pl.BlockSpec((1, 2, lw, wl), lambda b, p: (b, 0, p, 0)))
        out_shape.append(jax.ShapeDtypeStruct((nb, 2, C_WIDTH, wl), F32))
    body = functools.partial(_attn_p_body, L=L)
    return pl.pallas_call(
        body,
        grid=(nb, 2),
        in_specs=in_specs,
        out_specs=out_specs,
        out_shape=out_shape,
        scratch_shapes=[pltpu.VMEM((3, L, lw), F32)] * 2,
        compiler_params=_params(2),
        name="attn_prompt",
    )(*([z3] * 9), biasmat)


def _attn_s_body(z_ref, c0, c1, c2, ma0, ma1, ma2, mb_ref, o_ref, n0, n1, n2):
    nq = 8
    z = z_ref[0]
    hm = _head_mask(C_HEADS * nq, C_WIDTH, nq, C_DH)
    lane = lax.broadcasted_iota(jnp.int32, (2 * C_WIDTH, C_QBLOCK), 1)
    outs, lses = [], []
    for g, (cref, mref, nref) in enumerate(((c0, ma0, n0), (c1, ma1, n1), (c2, ma2, n2))):
        W = C_PATTERNS[g][0]
        q = z[:, g * C_WIDTH:(g + 1) * C_WIDTH] * (C_DH ** -0.5)
        kn = z[:, (3 + g) * C_WIDTH:(4 + g) * C_WIDTH]
        vn = z[:, (6 + g) * C_WIDTH:(7 + g) * C_WIDTH]
        new = jnp.concatenate([jnp.zeros((C_QBLOCK - nq, 2 * C_WIDTH), F32),
                               jnp.concatenate([kn, vn], axis=1)], axis=0)
        new_t = new.T
        old = cref[0]
        sh = pltpu.roll(old, W - nq, axis=1)
        if W > C_QBLOCK:
            nref[0, :, 0:W - C_QBLOCK] = sh[:, 0:W - C_QBLOCK]
        nref[0, :, W - C_QBLOCK:W] = jnp.where(lane >= C_QBLOCK - nq, new_t,
                                               sh[:, W - C_QBLOCK:W])
        qbd = jnp.where(hm, jnp.concatenate([q] * C_HEADS, axis=0), 0.0).astype(BF16)
        ka = old[0:C_WIDTH, :].astype(BF16)
        va = old[C_WIDTH:2 * C_WIDTH, :].astype(BF16)
        kb = new_t[0:C_WIDTH, :].astype(BF16)
        vb = new_t[C_WIDTH:2 * C_WIDTH, :].astype(BF16)
        sa = jnp.dot(qbd, ka, preferred_element_type=F32) + mref[...]
        sb = jnp.dot(qbd, kb, preferred_element_type=F32) + mb_ref[g]
        mx = jnp.maximum(jnp.max(sa, axis=-1, keepdims=True), jnp.max(sb, axis=-1, keepdims=True))
        pa = jnp.exp(sa - mx)
        pb = jnp.exp(sb - mx)
        l = jnp.sum(pa, axis=-1, keepdims=True) + jnp.sum(pb, axis=-1, keepdims=True)
        r = (lax.dot_general(pa.astype(BF16), va, NT_DIMS, preferred_element_type=F32)
             + lax.dot_general(pb.astype(BF16), vb, NT_DIMS, preferred_element_type=F32))
        outs.append(r / l)
        lses.append(mx + jnp.log(l))
    mx = jnp.maximum(jnp.maximum(lses[0], lses[1]), lses[2])
    ws = [jnp.exp(ls - mx) for ls in lses]
    o32 = (ws[0] * outs[0] + ws[1] * outs[1] + ws[2] * outs[2]) / (ws[0] + ws[1] + ws[2])
    o32 = jnp.where(hm, o32, 0.0)
    o_ref[0] = o32[0:nq] + o32[nq:2 * nq] + o32[2 * nq:3 * nq] + o32[3 * nq:4 * nq]


def _attn_sample(z3, caches, mas, mb):
    nb, nq, _ = z3.shape
    kvw = 2 * C_WIDTH
    in_specs = [pl.BlockSpec((1, nq, Z_WIDTH), lambda b: (b, 0, 0))]
    out_specs = [pl.BlockSpec((1, nq, C_WIDTH), lambda b: (b, 0, 0))]
    out_shape = [jax.ShapeDtypeStruct((nb, nq, C_WIDTH), F32)]
    for W, _ in C_PATTERNS:
        in_specs.append(pl.BlockSpec((1, kvw, W), lambda b: (b, 0, 0)))
        out_specs.append(pl.BlockSpec((1, kvw, W), lambda b: (b, 0, 0)))
        out_shape.append(jax.ShapeDtypeStruct((nb, kvw, W), F32))
    for W, _ in C_PATTERNS:
        in_specs.append(pl.BlockSpec((C_HEADS * nq, W), lambda b: (0, 0)))
    in_specs.append(pl.BlockSpec((3, C_HEADS * nq, C_QBLOCK), lambda b: (0, 0, 0)))
    return pl.pallas_call(
        _attn_s_body,
        grid=(nb,),
        in_specs=in_specs,
        out_specs=out_specs,
        out_shape=out_shape,
        compiler_params=_params(1),
        name="attn_sample",
    )(z3, *caches, *mas, mb)


def _odd_mix(oc_ref, du_ref, dv_ref, lng_ref, lnb_ref, ws_ref, sb_ref, x_ref, w_ref, post_ref):
    tm = x_ref.shape[0]
    u = _gelu(du_ref[...])
    vn = _layernorm(_gelu(dv_ref[...]), lng_ref[...], lnb_ref[...])
    lane_g = lax.broadcasted_iota(jnp.int32, (D_CHUNK, D_WIDTH), 1) >> int(math.log2(D_DH))
    parts = []
    for cidx in range(tm // D_CHUNK):
        vc = vn[cidx * D_CHUNK:(cidx + 1) * D_CHUNK, :]
        m = sb_ref[...]
        for g in range(D_WIDTH // D_DH):
            vm = jnp.where(lane_g == g, vc, 0.0).astype(BF16)
            m = m + jnp.dot(ws_ref[g], vm, preferred_element_type=F32)
        parts.append(m)
    od = u * jnp.concatenate(parts, axis=0)
    y = (jnp.dot(oc_ref[...].astype(BF16), w_ref[0:C_WIDTH, :], preferred_element_type=F32)
         + jnp.dot(od.astype(BF16), w_ref[C_WIDTH:C_WIDTH + D_WIDTH, :],
                   preferred_element_type=F32))
    return x_ref[...] + _rms(y, post_ref[...])


def _oddout_s_body(oc_ref, du_ref, dv_ref, lng_ref, lnb_ref, we_ref, sb_ref, x_ref, w_ref,
                   post_ref, xo_ref, vn_ref):
    nb, nq, _ = du_ref.shape
    u = _gelu(du_ref[...])
    vn = _layernorm(_gelu(dv_ref[...]), lng_ref[...], lnb_ref[...])
    vn_ref[...] = vn
    m = jnp.zeros((nb, nq, D_WIDTH), F32) + sb_ref[...]
    for s in range(nq):
        m = m + we_ref[s] * vn[:, s:s + 1, :]
    od = (u * m).reshape(nb * nq, D_WIDTH)
    y = (jnp.dot(oc_ref[...].astype(BF16), w_ref[0:C_WIDTH, :], preferred_element_type=F32)
         + jnp.dot(od.astype(BF16), w_ref[C_WIDTH:C_WIDTH + D_WIDTH, :],
                   preferred_element_type=F32))
    xo_ref[...] = x_ref[...] + _rms(y, post_ref[...])


def _oddout_sample(oc2, z3, x2, lng, lnb, wexp, sbe, w, post):
    nb, nq, _ = z3.shape
    rows = nb * nq
    return pl.pallas_call(
        _oddout_s_body,
        grid=(1,),
        in_specs=[pl.BlockSpec((rows, C_WIDTH), lambda i: (0, 0)),
                  pl.BlockSpec((nb, nq, D_WIDTH), lambda i: (0, 0, 9)),
                  pl.BlockSpec((nb, nq, D_WIDTH), lambda i: (0, 0, 10)),
                  pl.BlockSpec((1, D_WIDTH), lambda i: (0, 0)),
                  pl.BlockSpec((1, D_WIDTH), lambda i: (0, 0)),
                  pl.BlockSpec((nq, nq, D_WIDTH), lambda i: (0, 0, 0)),
                  pl.BlockSpec((nq, D_WIDTH), lambda i: (0, 0)),
                  pl.BlockSpec((rows, D_MODEL), lambda i: (0, 0)),
                  pl.BlockSpec((C_WIDTH + D_WIDTH, D_MODEL), lambda i: (0, 0)),
                  pl.BlockSpec((1, D_MODEL), lambda i: (0, 0))],
        out_specs=[pl.BlockSpec((rows, D_MODEL), lambda i: (0, 0)),
                   pl.BlockSpec((nb, nq, D_WIDTH), lambda i: (0, 0, 0))],
        out_shape=[jax.ShapeDtypeStruct((rows, D_MODEL), F32),
                   jax.ShapeDtypeStruct((nb, nq, D_WIDTH), F32)],
        compiler_params=_params(1),
        name="oddout_sample",
    )(oc2, z3, z3, lng, lnb, wexp, sbe, x2, w, post)


def _t5_bucket(dist):
    max_exact = N_BUCKETS // 2
    d32 = jnp.maximum(dist, 1).astype(F32)
    large = max_exact + (jnp.log(d32 / max_exact) / math.log(MAX_DIST / max_exact)
                         * (N_BUCKETS - max_exact)).astype(jnp.int32)
    large = jnp.minimum(large, N_BUCKETS - 1)
    return jnp.where(dist < max_exact, dist, large)


def _step_bias(rel_bias, g):
    window, dil = C_PATTERNS[g]
    j = jnp.arange(window // dil + 1, dtype=jnp.int32)
    return rel_bias[_t5_bucket(dil * j), g * C_HEADS:(g + 1) * C_HEADS].T.astype(F32)


def _skew(v, rows):
    n = v.shape[-1]
    lead = v.shape[:-1]
    t = jnp.broadcast_to(v[..., None, :], lead + (rows, n)).reshape(lead + (rows * n,))
    return t[..., :rows * (n - 1)].reshape(lead + (rows, n - 1))


def _prompt_bias(rel_bias):
    mats = []
    for g in range(len(C_PATTERNS)):
        bj = _step_bias(rel_bias, g)
        v = jnp.concatenate([bj[:, ::-1], jnp.full((C_HEADS, C_QBLOCK), NEG, F32)], axis=1)
        mats.append(_skew(v, C_QBLOCK))
    return jnp.stack(mats, axis=0)


def _sample_bias(rel_bias, nq):
    mas, mbs = [], []
    i = np.arange(nq)[:, None]
    i2 = np.arange(nq)[None, :]
    for g, (window, dil) in enumerate(C_PATTERNS):
        bj = _step_bias(rel_bias, g)
        on_grid = jnp.asarray(np.arange(window + 1) % dil == 0)
        bd = jnp.where(on_grid[None], jnp.repeat(bj, dil, axis=1)[:, :window + 1], NEG)
        e = jnp.concatenate([bd[:, ::-1], jnp.full((C_HEADS, nq - 1), NEG, F32)], axis=1)
        ma = _skew(e, nq)[:, :, :window]
        mas.append(ma.reshape(C_HEADS * nq, window))
        dist = i - i2
        ok = jnp.asarray((dist >= 0) & (dist % dil == 0))
        mb = jnp.where(ok[None], bd[:, np.maximum(dist, 0)], NEG)
        mb = jnp.pad(mb, ((0, 0), (0, 0), (C_QBLOCK - nq, 0)), constant_values=NEG)
        mbs.append(mb.reshape(C_HEADS * nq, C_QBLOCK))
    return mas, jnp.stack(mbs, axis=0)


def _row(v):
    return v.reshape(1, -1).astype(F32)


def kernel(x_prompt, x_sample, state_gla, state_conv_b, cache_c_w128, cache_c_w512, cache_c_w2048,
           state_ffn_conv, norm_pre_mix, norm_post_mix, norm_pre_ffn, norm_post_ffn, w_in_even,
           w_gate2, b_gate, gla_norm, conv_b_w, conv_b_b, ln_b_g, ln_b_b, w_out_even, w_in_odd,
           rel_bias, sgu_ln_g, sgu_ln_b, sgu_w, sgu_b, w_out_odd, w_up, ffn_dw_w, ffn_dw_b, w_down):
    nbp, lp, d = x_prompt.shape
    nbs, ls, _ = x_sample.shape

    we = w_in_even[0]
    split = 2 * A_QK + A_WIDTH
    w_e = jnp.concatenate(
        [we[:, :split], we[:, split + A_GATE_RANK:],
         we[:, split:split + A_GATE_RANK],
         jnp.zeros((d, A_QK - A_GATE_RANK), F32)], axis=1).astype(BF16)
    wg2p = jnp.zeros((A_QK, A_QK), F32).at[:A_GATE_RANK].set(w_gate2[0]).astype(BF16)
    segb = jnp.asarray(np.kron(np.eye(A_HEADS), np.ones((A_DK, A_DV))), BF16)
    cw = jnp.zeros((CONV_HDR, B_WIDTH), F32).at[:B_CONV].set(conv_b_w[0])
    w_oe = w_out_even[0].astype(BF16)
    w_o = w_in_odd[0].astype(BF16)
    w_oo = w_out_odd[0].astype(BF16)
    w_up_b = w_up.astype(BF16)
    w_dn_b = (0.5 * w_down).astype(BF16)
    dw8 = jnp.zeros((2, 8, 2 * D_FF), F32).at[:, :3].set(ffn_dw_w)
    tril = jnp.tril(jnp.ones((D_CHUNK, D_CHUNK), F32))
    sgu_wm = sgu_w[0] * tril
    bias_p = _prompt_bias(rel_bias)
    mas, mbs = _sample_bias(rel_bias, ls)

    def ffn_params(layer):
        return (_row(norm_pre_ffn[layer]), w_up_b[layer], dw8[layer], _row(ffn_dw_b[layer]),
                w_dn_b[layer], _row(norm_post_ffn[layer]))

    def even_layer(x2, nb, L, s0t, hdr, conv_tm, nseq):
        z = _inproj(x2, _row(norm_pre_mix[0]), w_e, 1024 if nb * L > 1024 else 512)
        z3 = z.reshape(nb, L, Z_WIDTH)
        o, st, c, ut = _mixer0(z3, wg2p, _row(b_gate[0]), segb, s0t, hdr, cw, _row(conv_b_b[0]),
                               _row(ln_b_g[0]), _row(ln_b_b[0]), conv_tm, nseq)
        mix_in = (o.reshape(nb * L, A_WIDTH), z, c.reshape(nb * L, B_WIDTH), x2,
                  _row(gla_norm[0]), w_oe, _row(norm_post_mix[0]))
        return mix_in, st.reshape(nb, A_HEADS, A_DK, A_DV), ut

    def kv_window(t):
        return t.reshape(t.shape[0], 2, C_HEADS, C_DH, t.shape[-1]).transpose(0, 4, 1, 2, 3)[None]

    xp2 = x_prompt.reshape(nbp * lp, d)
    mix0, p_gla, ut = even_layer(xp2, nbp, lp, jnp.zeros((nbp, A_QK, A_DV), F32),
                                 jnp.zeros((nbp, CONV_HDR, B_WIDTH), F32), 512, 1)
    p_conv_b = ut[:, CONV_HDR - (B_CONV - 1):]
    x2, tail0 = _ffn(mix0, *ffn_params(0), nbp, 512, mix="even")
    z = _inproj(x2, _row(norm_pre_mix[1]), w_o, 1024)
    z3 = z.reshape(nbp, lp, Z_WIDTH)
    oc, *p_kvt = _attn_prompt(z3, bias_p)
    wsm = sgu_wm.astype(BF16)
    sbe = jnp.repeat(sgu_b[0].T, D_DH, axis=-1)
    mix1 = (oc.reshape(nbp * lp, C_WIDTH), z, x2, _row(sgu_ln_g[0]), _row(sgu_ln_b[0]), wsm, sbe,
            w_oo, _row(norm_post_mix[1]))
    x4, tail1 = _ffn(mix1, *ffn_params(1), nbp, 512, mix="odd")
    y_prompt = x4.reshape(nbp, lp, d)
    p_kv = [kv_window(t) for t in p_kvt]
    p_ffn = jnp.stack([t.reshape(nbp, -1, 8, 2 * D_FF)[:, -1, 6:8] for t in (tail0, tail1)], axis=0)

    xs2 = x_sample.reshape(nbs * ls, d)
    hdr = jnp.pad(state_conv_b[0], ((0, 0), (CONV_HDR - (B_CONV - 1), 0), (0, 0)))
    smix0, s_gla, us = even_layer(xs2, nbs, ls, state_gla[0].reshape(nbs, A_QK, A_DV), hdr, ls,
                                  SAMPLE_NSEQ)
    s_conv_b = jnp.concatenate([state_conv_b[0][:, ls:], us], axis=1)

    def ffn_sample(x_in, layer, mix):
        st2 = state_ffn_conv[layer].reshape(nbs, 4 * D_FF)
        xo, tail = _ffn(x_in, *ffn_params(layer), 1, 256, st2, mix)
        return xo, tail.reshape(nbs, 2, 2 * D_FF)

    y2, s_ffn0 = ffn_sample(smix0, 0, "even")
    zs = _inproj(y2, _row(norm_pre_mix[1]), w_o, 512)
    zs3 = zs.reshape(nbs, ls, Z_WIDTH)
    caches = [c[0].transpose(0, 2, 3, 4, 1).reshape(nbs, 2 * C_WIDTH, c.shape[2])
              for c in (cache_c_w128, cache_c_w512, cache_c_w2048)]
    ocs, n0, n1, n2 = _attn_sample(zs3, caches, mas, mbs)
    wexp = jnp.repeat(sgu_wm[:, :ls, :ls].transpose(2, 1, 0), D_DH, axis=-1)
    sbes = jnp.repeat(sgu_b[0][:, :ls].T, D_DH, axis=-1)
    y3, s_sgu_v = _oddout_sample(ocs.reshape(nbs * ls, C_WIDTH), zs3, y2, _row(sgu_ln_g[0]),
                                 _row(sgu_ln_b[0]), wexp, sbes, w_oo, _row(norm_post_mix[1]))
    y4, s_ffn1 = ffn_sample(y3, 1, None)
    y_sample = y4.reshape(nbs, ls, d)
    s_kv = [kv_window(n.reshape(nbs, 2, C_WIDTH, n.shape[2])) for n in (n0, n1, n2)]
    s_ffn = jnp.stack([s_ffn0, s_ffn1], axis=0)

    return (y_prompt, y_sample, p_gla[None], p_conv_b[None], p_kv[0], p_kv[1], p_kv[2], p_ffn,
            s_gla[None], s_conv_b[None], s_kv[0], s_kv[1], s_kv[2], s_sgu_v[None], s_ffn)
```

```python
import functools
import math

import numpy as np
import jax
import jax.numpy as jnp
from jax import lax
from jax.experimental import pallas as pl
from jax.experimental.pallas import tpu as pltpu

F32 = jnp.float32
BF16 = jnp.bfloat16

D_MODEL = 1024
EPS = 1e-6
NEG = -1e30

A_HEADS = 4
A_DK = 64
A_DV = 128
A_QK = A_HEADS * A_DK
A_WIDTH = A_HEADS * A_DV
A_GATE_RANK = 16
A_GATE_NORM = 16.0
GLA_STEP = 16
MIX0_PIECE = 128
SAMPLE_NSEQ = 8
B_WIDTH = 512
B_CONV = 31
CONV_HDR = 32
C_PATTERNS = ((128, 1), (512, 4), (2048, 16))
C_HEADS = 4
C_DH = 64
C_WIDTH = C_HEADS * C_DH
C_QBLOCK = 128
N_BUCKETS = 32
MAX_DIST = 2048
D_WIDTH = 256
D_DH = 64
D_CHUNK = 128
D_FF = 2816
FFN_CHUNK = 256
FFN_NCHUNK = D_FF // FFN_CHUNK
Z_WIDTH = 2816

V7X_VMEM_BYTES = 64 * 1024 * 1024
VMEM_LIMIT_BYTES = V7X_VMEM_BYTES - 8 * 1024 * 1024

NT_DIMS = (((1,), (1,)), ((), ()))
TN_DIMS = (((0,), (0,)), ((), ()))


def _params(n_axes):
    return pltpu.CompilerParams(dimension_semantics=("arbitrary",) * n_axes,
                                vmem_limit_bytes=VMEM_LIMIT_BYTES)


def _rms(x, g):
    return x * lax.rsqrt(jnp.mean(x * x, axis=-1, keepdims=True) + EPS) * g


def _layernorm(x, g, b):
    mu = jnp.mean(x, axis=-1, keepdims=True)
    xc = x - mu
    var = jnp.mean(xc * xc, axis=-1, keepdims=True)
    return xc * lax.rsqrt(var + EPS) * g + b


def _sigmoid(x):
    return 1.0 / (1.0 + jnp.exp(-x))


def _head_mask(rows, lanes, rows_per_head, lanes_per_head):
    r = lax.broadcasted_iota(jnp.int32, (rows, lanes), 0) >> int(math.log2(rows_per_head))
    c = lax.broadcasted_iota(jnp.int32, (rows, lanes), 1) >> int(math.log2(lanes_per_head))
    return r == c


def _gelu(x):
    c = math.sqrt(2.0 / math.pi)
    return 0.5 * x * (1.0 + jnp.tanh(c * (x + 0.044715 * (x * x * x))))


def _inproj_body(x_ref, g_ref, w_ref, o_ref):
    h = _rms(x_ref[...], g_ref[...])
    o_ref[...] = jnp.dot(h.astype(BF16), w_ref[...], preferred_element_type=F32)


def _inproj(x2, g, w, tm):
    rows, d = x2.shape
    n = w.shape[1]
    return pl.pallas_call(
        _inproj_body,
        grid=(rows // tm,),
        in_specs=[pl.BlockSpec((tm, d), lambda i: (i, 0)),
                  pl.BlockSpec((1, d), lambda i: (0, 0)),
                  pl.BlockSpec((d, n), lambda i: (0, 0))],
        out_specs=pl.BlockSpec((tm, n), lambda i: (i, 0)),
        out_shape=jax.ShapeDtypeStruct((rows, n), F32),
        compiler_params=_params(1),
        name="inproj",
    )(x2, g, w)


def _mixer0_body(q_ref, k_ref, v_ref, glr_ref, ga_ref, gg_ref, wg_ref, bg_ref, segb_ref, s0_ref,
                 hdr_ref, cw_ref, cb_ref, lng_ref, lnb_ref,
                 o_ref, sn_ref, c_ref, ut_ref, b_sc, st_sc, xp, *, tm, SB, nseq, tail, piece):
    j = pl.program_id(1)
    last_j = pl.num_programs(1) - 1

    glr = glr_ref[...].reshape(nseq * tm, A_QK)
    xg = jnp.dot(glr.astype(BF16), wg_ref[...], preferred_element_type=F32) + bg_ref[...]
    la = (jnp.minimum(xg, 0.0) - jnp.log(1.0 + jnp.exp(-jnp.abs(xg)))) * (1.0 / A_GATE_NORM)
    tl = lax.broadcasted_iota(jnp.int32, (nseq * tm, A_QK), 0) & (SB - 1)
    sh = 1
    while sh < SB:
        la = la + jnp.where(tl >= sh, pltpu.roll(la, sh, axis=0), 0.0)
        sh *= 2
    b_sc[...] = la.reshape(nseq, tm, A_QK)

    for si in range(nseq):
        xs = xp.at[si]

        @pl.when(j == 0)
        def _(xs=xs, si=si):
            st_sc[si] = s0_ref[si].T
            xs[0:CONV_HDR, :] = hdr_ref[si]
            xs[CONV_HDR + tm:CONV_HDR + tm + 8, :] = jnp.zeros((8, B_WIDTH), F32)

        u = ga_ref[si] * _sigmoid(gg_ref[si])
        xs[CONV_HDR:CONV_HDR + tm, :] = u
        ut_ref[si] = u[tm - tail:tm, :]

    hm = _head_mask(A_HEADS * SB, A_QK, SB, A_DK)
    tri = lax.broadcasted_iota(jnp.int32, (8, A_QK), 0)
    ntile = SB // 8

    def gla_step(si, rows):
        q = q_ref[si, rows, :] * (A_DK ** -0.5)
        k = k_ref[si, rows, :]
        v = v_ref[si, rows, :]
        b = b_sc[si, rows, :]
        bl = b[SB - 1:SB, :]
        qt = q * jnp.exp(b)
        kh = k * jnp.exp(bl - b)
        st = st_sc[si]
        qbd = jnp.where(hm, jnp.concatenate([qt] * A_HEADS, axis=0), 0.0).astype(BF16)
        o_int = lax.dot_general(qbd, st.astype(BF16), NT_DIMS, preferred_element_type=F32)
        kbd = jnp.where(hm, jnp.concatenate([kh] * A_HEADS, axis=0), 0.0).astype(BF16)
        vst = jnp.concatenate([v[:, h * A_DV:(h + 1) * A_DV] for h in range(A_HEADS)],
                              axis=0).astype(BF16)
        upd = lax.dot_general(vst, kbd, TN_DIMS, preferred_element_type=F32)
        st_sc[si] = st * jnp.exp(bl) + upd
        ps = []
        for s in range(SB):
            lo = 8 * (s // 8)
            bs = b[s:s + 1, :]
            ks = k[s:s + 1, :]
            dd = b[lo:lo + 8, :] - bs
            if s % 8:
                dd = jnp.where(tri >= s % 8, dd, NEG)
            ps.append(q[lo:lo + 8, :] * jnp.exp(dd) * ks)
            if lo + 8 < SB:
                ps.append(q[lo + 8:SB, :] * jnp.exp(b[lo + 8:SB, :] - bs) * ks)
        pall = jnp.concatenate(ps, axis=0).astype(BF16)
        r = jnp.dot(pall, segb_ref[...], preferred_element_type=F32)
        od = [None] * ntile
        off = 0
        for s in range(SB):
            for tq in range(s // 8, ntile):
                term = r[off:off + 8, :] * v[s:s + 1, :]
                od[tq] = term if od[tq] is None else od[tq] + term
                off += 8
        o = (jnp.concatenate([o_int[h * SB:(h + 1) * SB] for h in range(A_HEADS)], axis=1)
             + jnp.concatenate(od, axis=0))
        o_ref[si, rows, :] = o

    def conv_rows(si, r0, static):
        xs = xp.at[si]
        n = piece + 8
        acc = None
        for r in range(7, -1, -1):
            z = None
            for q in range(5):
                kk = 8 * q + r - (CONV_HDR - (B_CONV - 1))
                if 0 <= kk < B_CONV:
                    start = 8 * q + r0 if static else pl.multiple_of(8 * q + r0, 8)
                    term = cw_ref[kk:kk + 1, :] * xs[pl.ds(start, n), :]
                    z = term if z is None else z + term
            acc = z if acc is None else z + pltpu.roll(acc, n - 1, axis=0)
        y = _layernorm(acc[0:piece, :] + cb_ref[...], lng_ref[...], lnb_ref[...])
        c_ref[si, pl.ds(r0, piece), :] = y * _sigmoid(y)

    npiece = tm // piece
    if npiece == 1:
        for si in range(nseq):
            for s in range(piece // SB):
                gla_step(si, slice(s * SB, (s + 1) * SB))
            conv_rows(si, 0, True)
    else:
        def do_piece(p, carry):
            r0 = pl.multiple_of(p * piece, piece)
            for si in range(nseq):
                for s in range(piece // SB):
                    gla_step(si, pl.ds(pl.multiple_of(r0 + s * SB, SB), SB))
                conv_rows(si, r0, False)
            return carry

        lax.fori_loop(0, npiece, do_piece, 0)

    for si in range(nseq):
        xs = xp.at[si]
        nxt = xs[tm:tm + CONV_HDR, :]
        xs[0:CONV_HDR, :] = nxt

        @pl.when(j == last_j)
        def _(si=si):
            sn_ref[si] = st_sc[si].T


def _mixer0(z3, wg2p, bgate, segb, s0, hdr, cw, cb, lng, lnb, tm, nseq):
    nb, L, _ = z3.shape
    assert nseq == 1 or L == tm
    SB = GLA_STEP if tm % GLA_STEP == 0 else tm
    piece = min(tm, MIX0_PIECE)
    tail = min(CONV_HDR, tm)
    fix = lambda b, j: (0, 0)

    def zcol(block):
        return lambda b, j: (b, j, block)

    body = functools.partial(_mixer0_body, tm=tm, SB=SB, nseq=nseq, tail=tail, piece=piece)
    return pl.pallas_call(
        body,
        grid=(nb // nseq, L // tm),
        in_specs=[pl.BlockSpec((nseq, tm, A_QK), zcol(0)),
                  pl.BlockSpec((nseq, tm, A_QK), zcol(1)),
                  pl.BlockSpec((nseq, tm, A_WIDTH), zcol(1)),
                  pl.BlockSpec((nseq, tm, A_QK), zcol(10)),
                  pl.BlockSpec((nseq, tm, B_WIDTH), zcol(3)),
                  pl.BlockSpec((nseq, tm, B_WIDTH), zcol(4)),
                  pl.BlockSpec((A_QK, A_QK), fix),
                  pl.BlockSpec((1, A_QK), fix),
                  pl.BlockSpec((A_QK, A_WIDTH), fix),
                  pl.BlockSpec((nseq, A_QK, A_DV), lambda b, j: (b, 0, 0)),
                  pl.BlockSpec((nseq, CONV_HDR, B_WIDTH), lambda b, j: (b, 0, 0)),
                  pl.BlockSpec((CONV_HDR, B_WIDTH), fix),
                  pl.BlockSpec((1, B_WIDTH), fix),
                  pl.BlockSpec((1, B_WIDTH), fix),
                  pl.BlockSpec((1, B_WIDTH), fix)],
        out_specs=[pl.BlockSpec((nseq, tm, A_WIDTH), lambda b, j: (b, j, 0)),
                   pl.BlockSpec((nseq, A_QK, A_DV), lambda b, j: (b, 0, 0)),
                   pl.BlockSpec((nseq, tm, B_WIDTH), lambda b, j: (b, j, 0)),
                   pl.BlockSpec((nseq, tail, B_WIDTH), lambda b, j: (b, 0, 0))],
        out_shape=[jax.ShapeDtypeStruct((nb, L, A_WIDTH), F32),
                   jax.ShapeDtypeStruct((nb, A_QK, A_DV), F32),
                   jax.ShapeDtypeStruct((nb, L, B_WIDTH), F32),
                   jax.ShapeDtypeStruct((nb, tail, B_WIDTH), F32)],
        scratch_shapes=[pltpu.VMEM((nseq, tm, A_QK), F32),
                        pltpu.VMEM((nseq, A_DV, A_QK), F32),
                        pltpu.VMEM((nseq, CONV_HDR + tm + 8, B_WIDTH), F32)],
        compiler_params=_params(2),
        name="mixer0",
    )(z3, z3, z3, z3, z3, z3, wg2p, bgate, segb, s0, hdr, cw, cb, lng, lnb)


def _even_mix(o_ref, r_ref, c_ref, x_ref, gn_ref, w_ref, post_ref):
    o = o_ref[...]
    gn = gn_ref[...]
    oa = jnp.concatenate([_rms(o[:, h * A_DV:(h + 1) * A_DV], gn) for h in range(A_HEADS)], axis=1)
    r = r_ref[...]
    oa = oa * (r * _sigmoid(r))
    y = (jnp.dot(oa.astype(BF16), w_ref[0:A_WIDTH, :], preferred_element_type=F32)
         + jnp.dot(c_ref[...].astype(BF16), w_ref[A_WIDTH:A_WIDTH + B_WIDTH, :],
                   preferred_element_type=F32))
    return x_ref[...] + _rms(y, post_ref[...])


def _gelu_twice(x):
    c = math.sqrt(2.0 / math.pi)
    return x + x * jnp.tanh(x * (c + (c * 0.044715) * (x * x)))


N_MIX_REFS = {None: 1, "even": 7, "odd": 10}


def _ffn_body(*refs, tm, seq8, mix):
    head, refs = refs[:N_MIX_REFS[mix]], refs[N_MIX_REFS[mix]:]
    if seq8:
        (pg_ref, wup_ref, dw_ref, b_ref, wdn_ref, post_ref, st_ref,
         xo_ref, tail_ref, act_sc, p1_sc, p2_sc, u_sc) = refs
    else:
        (pg_ref, wup_ref, dw_ref, b_ref, wdn_ref, post_ref,
         xo_ref, tail_ref, act_sc, car) = refs
    i = pl.program_id(1)
    ns = tm // 8

    @pl.when(i == 0)
    def _():
        if seq8:
            p1_sc[...] = jnp.zeros(p1_sc.shape, F32)
            p2_sc[...] = jnp.zeros(p2_sc.shape, F32)
        else:
            car[...] = jnp.zeros((8, 2 * D_FF), F32)

    if mix == "even":
        x = _even_mix(*head)
    elif mix == "odd":
        x = _odd_mix(*head)
    else:
        x = head[0][...]
    h = _rms(x, pg_ref[...]).astype(BF16)

    def half(cols):
        u = jnp.dot(h, wup_ref[:, cols], preferred_element_type=F32)
        if seq8:
            t = lax.broadcasted_iota(jnp.int32, (tm, FFN_CHUNK), 0) & 7
            for hf in range(FFN_CHUNK // 128):
                lo = cols.start + 128 * hf
                c0 = slice(lo, lo + 128)
                c1 = slice(2 * D_FF + lo, 2 * D_FF + lo + 128)
                p1_sc[hf, pl.ds(0, ns, stride=8), :] = st_ref[:, c1]
                p2_sc[hf, pl.ds(0, ns, stride=8), :] = st_ref[:, c0]
                p2_sc[hf, pl.ds(1, ns, stride=8), :] = st_ref[:, c1]
                u_sc[hf] = u[:, 128 * hf:128 * (hf + 1)]
                tail_ref[:, c0] = u_sc[hf, pl.ds(6, ns, stride=8), :]
                tail_ref[:, c1] = u_sc[hf, pl.ds(7, ns, stride=8), :]
            p1 = jnp.concatenate([p1_sc[hf] for hf in range(FFN_CHUNK // 128)], axis=1)
            p2 = jnp.concatenate([p2_sc[hf] for hf in range(FFN_CHUNK // 128)], axis=1)
            u1 = jnp.where(t >= 1, pltpu.roll(u, 1, axis=0), p1)
            u2 = jnp.where(t >= 2, pltpu.roll(u, 2, axis=0), p2)
        else:
            xp = jnp.concatenate([car[:, cols], u], axis=0)
            u1 = pltpu.roll(xp, 1, axis=0)[8:, :]
            u2 = pltpu.roll(xp, 2, axis=0)[8:, :]
            last = u[tm - 8:tm, :]
            car[:, cols] = last
            tail_ref[0, :, cols] = last
        dw = dw_ref[:, cols]
        return dw[0:1, :] * u2 + dw[1:2, :] * u1 + dw[2:3, :] * u + b_ref[:, cols]

    for c in range(FFN_NCHUNK):
        g = half(slice(c * FFN_CHUNK, (c + 1) * FFN_CHUNK))
        val = half(slice(D_FF + c * FFN_CHUNK, D_FF + (c + 1) * FFN_CHUNK))
        act_sc[:, c * FFN_CHUNK:(c + 1) * FFN_CHUNK] = (_gelu_twice(g) * val).astype(BF16)
    f = jnp.dot(act_sc[...], wdn_ref[...], preferred_element_type=F32)
    xo_ref[...] = x + _rms(f, post_ref[...])


def _ffn(x_in, pg, wup, dw8, dwb, wdn_half, post, nb, tm, st2=None, mix=None):
    seq8 = st2 is not None
    xmap = lambda b, i: (b * nt + i, 0)
    fix = lambda b, i: (0, 0)
    fix3 = lambda b, i: (0, 0, 0)
    once = pl.Buffered(1)

    def col(block):
        return lambda b, i: (b * nt + i, block)

    if mix == "even":
        o2, z2, c2, x2, gn, w_mix, post_mix = x_in
        in_specs = [pl.BlockSpec((tm, A_WIDTH), xmap),
                    pl.BlockSpec((tm, A_WIDTH), col(2)),
                    pl.BlockSpec((tm, B_WIDTH), xmap),
                    pl.BlockSpec((tm, D_MODEL), xmap),
                    pl.BlockSpec((1, A_DV), fix),
                    pl.BlockSpec((A_WIDTH + B_WIDTH, D_MODEL), fix, pipeline_mode=once),
                    pl.BlockSpec((1, D_MODEL), fix)]
        args = [o2, z2, c2, x2, gn, w_mix, post_mix]
    elif mix == "odd":
        oc2, z2, x2, lng, lnb, wsm, sbe, w_mix, post_mix = x_in
        in_specs = [pl.BlockSpec((tm, C_WIDTH), xmap),
                    pl.BlockSpec((tm, D_WIDTH), col(9)),
                    pl.BlockSpec((tm, D_WIDTH), col(10)),
                    pl.BlockSpec((1, D_WIDTH), fix),
                    pl.BlockSpec((1, D_WIDTH), fix),
                    pl.BlockSpec((4, D_CHUNK, D_CHUNK), fix3),
                    pl.BlockSpec((D_CHUNK, D_WIDTH), fix),
                    pl.BlockSpec((tm, D_MODEL), xmap),
                    pl.BlockSpec((C_WIDTH + D_WIDTH, D_MODEL), fix, pipeline_mode=once),
                    pl.BlockSpec((1, D_MODEL), fix)]
        args = [oc2, z2, z2, lng, lnb, wsm, sbe, x2, w_mix, post_mix]
    else:
        x2 = x_in
        in_specs = [pl.BlockSpec((tm, D_MODEL), xmap)]
        args = [x2]
    assert len(args) == N_MIX_REFS[mix]
    rows = x2.shape[0]
    nt = rows // (nb * tm)
    in_specs += [pl.BlockSpec((1, D_MODEL), fix),
                 pl.BlockSpec((D_MODEL, 2 * D_FF), fix, pipeline_mode=once),
                 pl.BlockSpec((8, 2 * D_FF), fix),
                 pl.BlockSpec((1, 2 * D_FF), fix),
                 pl.BlockSpec((D_FF, D_MODEL), fix, pipeline_mode=once),
                 pl.BlockSpec((1, D_MODEL), fix)]
    args += [pg, wup, dw8, dwb, wdn_half, post]
    scratch = [pltpu.VMEM((tm, D_FF), BF16)]
    if seq8:
        ns = tm // 8
        in_specs.append(pl.BlockSpec((ns, 4 * D_FF), xmap))
        args.append(st2)
        tail_spec = pl.BlockSpec((ns, 4 * D_FF), xmap)
        tail_shape = jax.ShapeDtypeStruct((rows // 8, 4 * D_FF), F32)
        scratch += [pltpu.VMEM((FFN_CHUNK // 128, tm, 128), F32)] * 3
    else:
        tail_spec = pl.BlockSpec((1, 8, 2 * D_FF), lambda b, i: (b * nt + i, 0, 0))
        tail_shape = jax.ShapeDtypeStruct((nb * nt, 8, 2 * D_FF), F32)
        scratch.append(pltpu.VMEM((8, 2 * D_FF), F32))
    body = functools.partial(_ffn_body, tm=tm, seq8=seq8, mix=mix)
    return pl.pallas_call(
        body,
        grid=(nb, nt),
        in_specs=in_specs,
        out_specs=[pl.BlockSpec((tm, D_MODEL), xmap), tail_spec],
        out_shape=[jax.ShapeDtypeStruct((rows, D_MODEL), F32), tail_shape],
        scratch_shapes=scratch,
        compiler_params=_params(2),
        name="ffn" + ("_seq8" if seq8 else "") + ("_" + mix if mix else ""),
    )(*args)


def _attn_p_body(q0, k0, v0, q1, k1, v1, q2, k2, v2, bias_ref, o_ref, kv0, kv1, kv2,
                 og, lg, *, L):
    lane = lax.broadcasted_iota(jnp.int32, (C_QBLOCK, 2 * C_DH), 1)
    first = lane < C_DH
    col = lax.broadcasted_iota(jnp.int32, (C_QBLOCK, 2 * C_QBLOCK), 1)
    groups = ((q0, k0, v0), (q1, k1, v1), (q2, k2, v2))
    for g, (qr, kr, vr) in enumerate(groups):
        kvr = (kv0, kv1, kv2)[g]
        wl = kvr.shape[-1]
        piece = min(wl, 512)
        for c0 in range(0, wl, piece):
            src = slice(L - wl + c0, L - wl + c0 + piece)
            kvr[0, 0, :, c0:c0 + piece] = kr[0, src, :].T
            kvr[0, 1, :, c0:c0 + piece] = vr[0, src, :].T
        d = C_PATTERNS[g][1]
        nblk = L // d // C_QBLOCK
        for res in range(d):
            for m in range(nblk):
                def rows_of(mm, res=res, d=d):
                    if d == 1:
                        return slice(mm * C_QBLOCK, (mm + 1) * C_QBLOCK)
                    return pl.ds(res + d * C_QBLOCK * mm, C_QBLOCK, stride=d)

                rq = rows_of(m)
                qb = qr[0, rq, :] * (C_DH ** -0.5)
                rp = rows_of(max(m - 1, 0))
                kcat = jnp.concatenate([kr[0, rp, :], kr[0, rq, :]], axis=0).astype(BF16)
                vcat = jnp.concatenate([vr[0, rp, :], vr[0, rq, :]], axis=0).astype(BF16)
                outs, lses = [], []
                for hh in range(2):
                    keep = first if hh == 0 else jnp.logical_not(first)
                    qm = jnp.where(keep, qb, 0.0).astype(BF16)
                    s = lax.dot_general(qm, kcat, NT_DIMS, preferred_element_type=F32)
                    s = s + bias_ref[g, hh]
                    if m == 0:
                        s = jnp.where(col < C_QBLOCK, NEG, s)
                    mx = jnp.max(s, axis=-1, keepdims=True)
                    p = jnp.exp(s - mx)
                    l = jnp.sum(p, axis=-1, keepdims=True)
                    r = jnp.dot(p.astype(BF16), vcat, preferred_element_type=F32)
                    outs.append(r / l)
                    lses.append(mx + jnp.log(l))
                og[g, rq, :] = jnp.where(first, outs[0], outs[1])
                lg[g, rq, :] = jnp.where(first, lses[0], lses[1])
    cr = 256
    for cidx in range(L // cr):
        rows = slice(cidx * cr, (cidx + 1) * cr)
        l0, l1, l2 = lg[0, rows, :], lg[1, rows, :], lg[2, rows, :]
        mx = jnp.maximum(jnp.maximum(l0, l1), l2)
        w0, w1, w2 = jnp.exp(l0 - mx), jnp.exp(l1 - mx), jnp.exp(l2 - mx)
        num = w0 * og[0, rows, :] + w1 * og[1, rows, :] + w2 * og[2, rows, :]
        o_ref[0, rows, :] = num / (w0 + w1 + w2)


def _attn_prompt(z3, biasmat):
    nb, L, _ = z3.shape
    lw = 2 * C_DH
    in_specs = []
    for g in range(3):
        for part in range(3):
            base = part * 6 + g * 2
            in_specs.append(pl.BlockSpec((1, L, lw), lambda b, p, base=base: (b, 0, base + p)))
    in_specs.append(pl.BlockSpec((3, 2, C_QBLOCK, 2 * C_QBLOCK), lambda b, p: (0, p, 0, 0)))
    out_specs = [pl.BlockSpec((1, L, lw), lambda b, p: (b, 0, p))]
    out_shape = [jax.ShapeDtypeStruct((nb, L, C_WIDTH), F32)]
    for window, _ in C_PATTERNS:
        wl = min(window, L)
        out_specs.append(pl.BlockSpec((1, 2, lw, wl), lambda b, p: (b, 0, p, 0)))
        out_shape.append(jax.ShapeDtypeStruct((nb, 2, C_WIDTH, wl), F32))
    body = functools.partial(_attn_p_body, L=L)
    return pl.pallas_call(
        body,
        grid=(nb, 2),
        in_specs=in_specs,
        out_specs=out_specs,
        out_shape=out_shape,
        scratch_shapes=[pltpu.VMEM((3, L, lw), F32)] * 2,
        compiler_params=_params(2),
        name="attn_prompt",
    )(*([z3] * 9), biasmat)


def _attn_s_body(z_ref, c0, c1, c2, ma0, ma1, ma2, mb_ref, o_ref, n0, n1, n2):
    nq = 8
    z = z_ref[0]
    hm = _head_mask(C_HEADS * nq, C_WIDTH, nq, C_DH)
    lane = lax.broadcasted_iota(jnp.int32, (2 * C_WIDTH, C_QBLOCK), 1)
    outs, lses = [], []
    for g, (cref, mref, nref) in enumerate(((c0, ma0, n0), (c1, ma1, n1), (c2, ma2, n2))):
        W = C_PATTERNS[g][0]
        q = z[:, g * C_WIDTH:(g + 1) * C_WIDTH] * (C_DH ** -0.5)
        kn = z[:, (3 + g) * C_WIDTH:(4 + g) * C_WIDTH]
        vn = z[:, (6 + g) * C_WIDTH:(7 + g) * C_WIDTH]
        new = jnp.concatenate([jnp.zeros((C_QBLOCK - nq, 2 * C_WIDTH), F32),
                               jnp.concatenate([kn, vn], axis=1)], axis=0)
        new_t = new.T
        old = cref[0]
        sh = pltpu.roll(old, W - nq, axis=1)
        if W > C_QBLOCK:
            nref[0, :, 0:W - C_QBLOCK] = sh[:, 0:W - C_QBLOCK]
        nref[0, :, W - C_QBLOCK:W] = jnp.where(lane >= C_QBLOCK - nq, new_t,
                                               sh[:, W - C_QBLOCK:W])
        qbd = jnp.where(hm, jnp.concatenate([q] * C_HEADS, axis=0), 0.0).astype(BF16)
        ka = old[0:C_WIDTH, :].astype(BF16)
        va = old[C_WIDTH:2 * C_WIDTH, :].astype(BF16)
        kb = new_t[0:C_WIDTH, :].astype(BF16)
        vb = new_t[C_WIDTH:2 * C_WIDTH, :].astype(BF16)
        sa = jnp.dot(qbd, ka, preferred_element_type=F32) + mref[...]
        sb = jnp.dot(qbd, kb, preferred_element_type=F32) + mb_ref[g]
        mx = jnp.maximum(jnp.max(sa, axis=-1, keepdims=True), jnp.max(sb, axis=-1, keepdims=True))
        pa = jnp.exp(sa - mx)
        pb = jnp.exp(sb - mx)
        l = jnp.sum(pa, axis=-1, keepdims=True) + jnp.sum(pb, axis=-1, keepdims=True)
        r = (lax.dot_general(pa.astype(BF16), va, NT_DIMS, preferred_element_type=F32)
             + lax.dot_general(pb.astype(BF16), vb, NT_DIMS, preferred_element_type=F32))
        outs.append(r / l)
        lses.append(mx + jnp.log(l))
    mx = jnp.maximum(jnp.maximum(lses[0], lses[1]), lses[2])
    ws = [jnp.exp(ls - mx) for ls in lses]
    o32 = (ws[0] * outs[0] + ws[1] * outs[1] + ws[2] * outs[2]) / (ws[0] + ws[1] + ws[2])
    o32 = jnp.where(hm, o32, 0.0)
    o_ref[0] = o32[0:nq] + o32[nq:2 * nq] + o32[2 * nq:3 * nq] + o32[3 * nq:4 * nq]


def _attn_sample(z3, caches, mas, mb):
    nb, nq, _ = z3.shape
    kvw = 2 * C_WIDTH
    in_specs = [pl.BlockSpec((1, nq, Z_WIDTH), lambda b: (b, 0, 0))]
    out_specs = [pl.BlockSpec((1, nq, C_WIDTH), lambda b: (b, 0, 0))]
    out_shape = [jax.ShapeDtypeStruct((nb, nq, C_WIDTH), F32)]
    for W, _ in C_PATTERNS:
        in_specs.append(pl.BlockSpec((1, kvw, W), lambda b: (b, 0, 0)))
        out_specs.append(pl.BlockSpec((1, kvw, W), lambda b: (b, 0, 0)))
        out_shape.append(jax.ShapeDtypeStruct((nb, kvw, W), F32))
    for W, _ in C_PATTERNS:
        in_specs.append(pl.BlockSpec((C_HEADS * nq, W), lambda b: (0, 0)))
    in_specs.append(pl.BlockSpec((3, C_HEADS * nq, C_QBLOCK), lambda b: (0, 0, 0)))
    return pl.pallas_call(
        _attn_s_body,
        grid=(nb,),
        in_specs=in_specs,
        out_specs=out_specs,
        out_shape=out_shape,
        compiler_params=_params(1),
        name="attn_sample",
    )(z3, *caches, *mas, mb)


def _odd_mix(oc_ref, du_ref, dv_ref, lng_ref, lnb_ref, ws_ref, sb_ref, x_ref, w_ref, post_ref):
    tm = x_ref.shape[0]
    u = _gelu(du_ref[...])
    vn = _layernorm(_gelu(dv_ref[...]), lng_ref[...], lnb_ref[...])
    lane_g = lax.broadcasted_iota(jnp.int32, (D_CHUNK, D_WIDTH), 1) >> int(math.log2(D_DH))
    parts = []
    for cidx in range(tm // D_CHUNK):
        vc = vn[cidx * D_CHUNK:(cidx + 1) * D_CHUNK, :]
        m = sb_ref[...]
        for g in range(D_WIDTH // D_DH):
            vm = jnp.where(lane_g == g, vc, 0.0).astype(BF16)
            m = m + jnp.dot(ws_ref[g], vm, preferred_element_type=F32)
        parts.append(m)
    od = u * jnp.concatenate(parts, axis=0)
    y = (jnp.dot(oc_ref[...].astype(BF16), w_ref[0:C_WIDTH, :], preferred_element_type=F32)
         + jnp.dot(od.astype(BF16), w_ref[C_WIDTH:C_WIDTH + D_WIDTH, :],
                   preferred_element_type=F32))
    return x_ref[...] + _rms(y, post_ref[...])


def _oddout_s_body(oc_ref, du_ref, dv_ref, lng_ref, lnb_ref, we_ref, sb_ref, x_ref, w_ref,
                   post_ref, xo_ref, vn_ref):
    nb, nq, _ = du_ref.shape
    u = _gelu(du_ref[...])
    vn = _layernorm(_gelu(dv_ref[...]), lng_ref[...], lnb_ref[...])
    vn_ref[...] = vn
    m = jnp.zeros((nb, nq, D_WIDTH), F32) + sb_ref[...]
    for s in range(nq):
        m = m + we_ref[s] * vn[:, s:s + 1, :]
    od = (u * m).reshape(nb * nq, D_WIDTH)
    y = (jnp.dot(oc_ref[...].astype(BF16), w_ref[0:C_WIDTH, :], preferred_element_type=F32)
         + jnp.dot(od.astype(BF16), w_ref[C_WIDTH:C_WIDTH + D_WIDTH, :],
                   preferred_element_type=F32))
    xo_ref[...] = x_ref[...] + _rms(y, post_ref[...])


def _oddout_sample(oc2, z3, x2, lng, lnb, wexp, sbe, w, post):
    nb, nq, _ = z3.shape
    rows = nb * nq
    return pl.pallas_call(
        _oddout_s_body,
        grid=(1,),
        in_specs=[pl.BlockSpec((rows, C_WIDTH), lambda i: (0, 0)),
                  pl.BlockSpec((nb, nq, D_WIDTH), lambda i: (0, 0, 9)),
                  pl.BlockSpec((nb, nq, D_WIDTH), lambda i: (0, 0, 10)),
                  pl.BlockSpec((1, D_WIDTH), lambda i: (0, 0)),
                  pl.BlockSpec((1, D_WIDTH), lambda i: (0, 0)),
                  pl.BlockSpec((nq, nq, D_WIDTH), lambda i: (0, 0, 0)),
                  pl.BlockSpec((nq, D_WIDTH), lambda i: (0, 0)),
                  pl.BlockSpec((rows, D_MODEL), lambda i: (0, 0)),
                  pl.BlockSpec((C_WIDTH + D_WIDTH, D_MODEL), lambda i: (0, 0)),
                  pl.BlockSpec((1, D_MODEL), lambda i: (0, 0))],
        out_specs=[pl.BlockSpec((rows, D_MODEL), lambda i: (0, 0)),
                   pl.BlockSpec((nb, nq, D_WIDTH), lambda i: (0, 0, 0))],
        out_shape=[jax.ShapeDtypeStruct((rows, D_MODEL), F32),
                   jax.ShapeDtypeStruct((nb, nq, D_WIDTH), F32)],
        compiler_params=_params(1),
        name="oddout_sample",
    )(oc2, z3, z3, lng, lnb, wexp, sbe, x2, w, post)


def _t5_bucket(dist):
    max_exact = N_BUCKETS // 2
    d32 = jnp.maximum(dist, 1).astype(F32)
    large = max_exact + (jnp.log(d32 / max_exact) / math.log(MAX_DIST / max_exact)
                         * (N_BUCKETS - max_exact)).astype(jnp.int32)
    large = jnp.minimum(large, N_BUCKETS - 1)
    return jnp.where(dist < max_exact, dist, large)


def _step_bias(rel_bias, g):
    window, dil = C_PATTERNS[g]
    j = jnp.arange(window // dil + 1, dtype=jnp.int32)
    return rel_bias[_t5_bucket(dil * j), g * C_HEADS:(g + 1) * C_HEADS].T.astype(F32)


def _skew(v, rows):
    n = v.shape[-1]
    lead = v.shape[:-1]
    t = jnp.broadcast_to(v[..., None, :], lead + (rows, n)).reshape(lead + (rows * n,))
    return t[..., :rows * (n - 1)].reshape(lead + (rows, n - 1))


def _prompt_bias(rel_bias):
    mats = []
    for g in range(len(C_PATTERNS)):
        bj = _step_bias(rel_bias, g)
        v = jnp.concatenate([bj[:, ::-1], jnp.full((C_HEADS, C_QBLOCK), NEG, F32)], axis=1)
        mats.append(_skew(v, C_QBLOCK))
    return jnp.stack(mats, axis=0)


def _sample_bias(rel_bias, nq):
    mas, mbs = [], []
    i = np.arange(nq)[:, None]
    i2 = np.arange(nq)[None, :]
    for g, (window, dil) in enumerate(C_PATTERNS):
        bj = _step_bias(rel_bias, g)
        on_grid = jnp.asarray(np.arange(window + 1) % dil == 0)
        bd = jnp.where(on_grid[None], jnp.repeat(bj, dil, axis=1)[:, :window + 1], NEG)
        e = jnp.concatenate([bd[:, ::-1], jnp.full((C_HEADS, nq - 1), NEG, F32)], axis=1)
        ma = _skew(e, nq)[:, :, :window]
        mas.append(ma.reshape(C_HEADS * nq, window))
        dist = i - i2
        ok = jnp.asarray((dist >= 0) & (dist % dil == 0))
        mb = jnp.where(ok[None], bd[:, np.maximum(dist, 0)], NEG)
        mb = jnp.pad(mb, ((0, 0), (0, 0), (C_QBLOCK - nq, 0)), constant_values=NEG)
        mbs.append(mb.reshape(C_HEADS * nq, C_QBLOCK))
    return mas, jnp.stack(mbs, axis=0)


def _row(v):
    return v.reshape(1, -1).astype(F32)


def kernel(x_prompt, x_sample, state_gla, state_conv_b, cache_c_w128, cache_c_w512, cache_c_w2048,
           state_ffn_conv, norm_pre_mix, norm_post_mix, norm_pre_ffn, norm_post_ffn, w_in_even,
           w_gate2, b_gate, gla_norm, conv_b_w, conv_b_b, ln_b_g, ln_b_b, w_out_even, w_in_odd,
           rel_bias, sgu_ln_g, sgu_ln_b, sgu_w, sgu_b, w_out_odd, w_up, ffn_dw_w, ffn_dw_b, w_down):
    nbp, lp, d = x_prompt.shape
    nbs, ls, _ = x_sample.shape

    we = w_in_even[0]
    split = 2 * A_QK + A_WIDTH
    w_e = jnp.concatenate(
        [we[:, :split], we[:, split + A_GATE_RANK:],
         we[:, split:split + A_GATE_RANK],
         jnp.zeros((d, A_QK - A_GATE_RANK), F32)], axis=1).astype(BF16)
    wg2p = jnp.zeros((A_QK, A_QK), F32).at[:A_GATE_RANK].set(w_gate2[0]).astype(BF16)
    segb = jnp.asarray(np.kron(np.eye(A_HEADS), np.ones((A_DK, A_DV))), BF16)
    cw = jnp.zeros((CONV_HDR, B_WIDTH), F32).at[:B_CONV].set(conv_b_w[0])
    w_oe = w_out_even[0].astype(BF16)
    w_o = w_in_odd[0].astype(BF16)
    w_oo = w_out_odd[0].astype(BF16)
    w_up_b = w_up.astype(BF16)
    w_dn_b = (0.5 * w_down).astype(BF16)
    dw8 = jnp.zeros((2, 8, 2 * D_FF), F32).at[:, :3].set(ffn_dw_w)
    tril = jnp.tril(jnp.ones((D_CHUNK, D_CHUNK), F32))
    sgu_wm = sgu_w[0] * tril
    bias_p = _prompt_bias(rel_bias)
    mas, mbs = _sample_bias(rel_bias, ls)

    def ffn_params(layer):
        return (_row(norm_pre_ffn[layer]), w_up_b[layer], dw8[layer], _row(ffn_dw_b[layer]),
                w_dn_b[layer], _row(norm_post_ffn[layer]))

    def even_layer(x2, nb, L, s0t, hdr, conv_tm, nseq):
        z = _inproj(x2, _row(norm_pre_mix[0]), w_e, 1024 if nb * L > 1024 else 512)
        z3 = z.reshape(nb, L, Z_WIDTH)
        o, st, c, ut = _mixer0(z3, wg2p, _row(b_gate[0]), segb, s0t, hdr, cw, _row(conv_b_b[0]),
                               _row(ln_b_g[0]), _row(ln_b_b[0]), conv_tm, nseq)
        mix_in = (o.reshape(nb * L, A_WIDTH), z, c.reshape(nb * L, B_WIDTH), x2,
                  _row(gla_norm[0]), w_oe, _row(norm_post_mix[0]))
        return mix_in, st.reshape(nb, A_HEADS, A_DK, A_DV), ut

    def kv_window(t):
        return t.reshape(t.shape[0], 2, C_HEADS, C_DH, t.shape[-1]).transpose(0, 4, 1, 2, 3)[None]

    xp2 = x_prompt.reshape(nbp * lp, d)
    mix0, p_gla, ut = even_layer(xp2, nbp, lp, jnp.zeros((nbp, A_QK, A_DV), F32),
                                 jnp.zeros((nbp, CONV_HDR, B_WIDTH), F32), 512, 1)
    p_conv_b = ut[:, CONV_HDR - (B_CONV - 1):]
    x2, tail0 = _ffn(mix0, *ffn_params(0), nbp, 512, mix="even")
    z = _inproj(x2, _row(norm_pre_mix[1]), w_o, 1024)
    z3 = z.reshape(nbp, lp, Z_WIDTH)
    oc, *p_kvt = _attn_prompt(z3, bias_p)
    wsm = sgu_wm.astype(BF16)
    sbe = jnp.repeat(sgu_b[0].T, D_DH, axis=-1)
    mix1 = (oc.reshape(nbp * lp, C_WIDTH), z, x2, _row(sgu_ln_g[0]), _row(sgu_ln_b[0]), wsm, sbe,
            w_oo, _row(norm_post_mix[1]))
    x4, tail1 = _ffn(mix1, *ffn_params(1), nbp, 512, mix="odd")
    y_prompt = x4.reshape(nbp, lp, d)
    p_kv = [kv_window(t) for t in p_kvt]
    p_ffn = jnp.stack([t.reshape(nbp, -1, 8, 2 * D_FF)[:, -1, 6:8] for t in (tail0, tail1)], axis=0)

    xs2 = x_sample.reshape(nbs * ls, d)
    hdr = jnp.pad(state_conv_b[0], ((0, 0), (CONV_HDR - (B_CONV - 1), 0), (0, 0)))
    smix0, s_gla, us = even_layer(xs2, nbs, ls, state_gla[0].reshape(nbs, A_QK, A_DV), hdr, ls,
                                  SAMPLE_NSEQ)
    s_conv_b = jnp.concatenate([state_conv_b[0][:, ls:], us], axis=1)

    def ffn_sample(x_in, layer, mix):
        st2 = state_ffn_conv[layer].reshape(nbs, 4 * D_FF)
        xo, tail = _ffn(x_in, *ffn_params(layer), 1, 256, st2, mix)
        return xo, tail.reshape(nbs, 2, 2 * D_FF)

    y2, s_ffn0 = ffn_sample(smix0, 0, "even")
    zs = _inproj(y2, _row(norm_pre_mix[1]), w_o, 512)
    zs3 = zs.reshape(nbs, ls, Z_WIDTH)
    caches = [c[0].transpose(0, 2, 3, 4, 1).reshape(nbs, 2 * C_WIDTH, c.shape[2])
              for c in (cache_c_w128, cache_c_w512, cache_c_w2048)]
    ocs, n0, n1, n2 = _attn_sample(zs3, caches, mas, mbs)
    wexp = jnp.repeat(sgu_wm[:, :ls, :ls].transpose(2, 1, 0), D_DH, axis=-1)
    sbes = jnp.repeat(sgu_b[0][:, :ls].T, D_DH, axis=-1)
    y3, s_sgu_v = _oddout_sample(ocs.reshape(nbs * ls, C_WIDTH), zs3, y2, _row(sgu_ln_g[0]),
                                 _row(sgu_ln_b[0]), wexp, sbes, w_oo, _row(norm_post_mix[1]))
    y4, s_ffn1 = ffn_sample(y3, 1, None)
    y_sample = y4.reshape(nbs, ls, d)
    s_kv = [kv_window(n.reshape(nbs, 2, C_WIDTH, n.shape[2])) for n in (n0, n1, n2)]
    s_ffn = jnp.stack([s_ffn0, s_ffn1], axis=0)

    return (y_prompt, y_sample, p_gla[None], p_conv_b[None], p_kv[0], p_kv[1], p_kv[2], p_ffn,
            s_gla[None], s_conv_b[None], s_kv[0], s_kv[1], s_kv[2], s_sgu_v[None], s_ffn)
```

```python
import functools
import math

import numpy as np
import jax
import jax.numpy as jnp
from jax import lax
from jax.experimental import pallas as pl
from jax.experimental.pallas import tpu as pltpu

F32 = jnp.float32
BF16 = jnp.bfloat16

D_MODEL = 1024
EPS = 1e-6
NEG = -1e30

A_HEADS = 4
A_DK = 64
A_DV = 128
A_QK = A_HEADS * A_DK
A_WIDTH = A_HEADS * A_DV
A_GATE_RANK = 16
A_GATE_NORM = 16.0
GLA_STEP = 16
MIX0_PIECE = 128
SAMPLE_NSEQ = 8
B_WIDTH = 512
B_CONV = 31
CONV_HDR = 32
C_PATTERNS = ((128, 1), (512, 4), (2048, 16))
C_HEADS = 4
C_DH = 64
C_WIDTH = C_HEADS * C_DH
C_QBLOCK = 128
N_BUCKETS = 32
MAX_DIST = 2048
D_WIDTH = 256
D_DH = 64
D_CHUNK = 128
D_FF = 2816
FFN_CHUNK = 256
FFN_NCHUNK = D_FF // FFN_CHUNK
Z_WIDTH = 2816

VMEM_LIMIT_BYTES = 56 * 1024 * 1024

NT_DIMS = (((1,), (1,)), ((), ()))
TN_DIMS = (((0,), (0,)), ((), ()))


def _params(n_axes):
    return pltpu.CompilerParams(dimension_semantics=("arbitrary",) * n_axes,
                                vmem_limit_bytes=VMEM_LIMIT_BYTES)


def _rms(x, g):
    return x * lax.rsqrt(jnp.mean(x * x, axis=-1, keepdims=True) + EPS) * g


def _layernorm(x, g, b):
    mu = jnp.mean(x, axis=-1, keepdims=True)
    xc = x - mu
    var = jnp.mean(xc * xc, axis=-1, keepdims=True)
    return xc * lax.rsqrt(var + EPS) * g + b


def _sigmoid(x):
    return 1.0 / (1.0 + jnp.exp(-x))


def _head_mask(rows, lanes, rows_per_head, lanes_per_head):
    r = lax.broadcasted_iota(jnp.int32, (rows, lanes), 0) >> int(math.log2(rows_per_head))
    c = lax.broadcasted_iota(jnp.int32, (rows, lanes), 1) >> int(math.log2(lanes_per_head))
    return r == c


def _gelu(x):
    c = math.sqrt(2.0 / math.pi)
    return 0.5 * x * (1.0 + jnp.tanh(c * (x + 0.044715 * (x * x * x))))


def _inproj_body(x_ref, g_ref, w_ref, o_ref):
    h = _rms(x_ref[...], g_ref[...])
    o_ref[...] = jnp.dot(h.astype(BF16), w_ref[...], preferred_element_type=F32)


def _inproj(x2, g, w, tm):
    rows, d = x2.shape
    n = w.shape[1]
    return pl.pallas_call(
        _inproj_body,
        grid=(rows // tm,),
        in_specs=[pl.BlockSpec((tm, d), lambda i: (i, 0)),
                  pl.BlockSpec((1, d), lambda i: (0, 0)),
                  pl.BlockSpec((d, n), lambda i: (0, 0), pipeline_mode=pl.Buffered(1))],
        out_specs=pl.BlockSpec((tm, n), lambda i: (i, 0)),
        out_shape=jax.ShapeDtypeStruct((rows, n), F32),
        compiler_params=_params(1),
        name="inproj",
    )(x2, g, w)


def _mixer0_body(q_ref, k_ref, v_ref, glr_ref, ga_ref, gg_ref, wg_ref, bg_ref, segb_ref, s0_ref,
                 hdr_ref, cw_ref, cb_ref, lng_ref, lnb_ref,
                 o_ref, sn_ref, c_ref, ut_ref, b_sc, st_sc, xp, *, tm, SB, nseq, tail, piece,
                 tmajor):
    j = pl.program_id(1)
    last_j = pl.num_programs(1) - 1

    glr = glr_ref[...].reshape(nseq * tm, A_QK)
    xg = jnp.dot(glr.astype(BF16), wg_ref[...], preferred_element_type=F32) + bg_ref[...]
    la = (jnp.minimum(xg, 0.0) - jnp.log(1.0 + jnp.exp(-jnp.abs(xg)))) * (1.0 / A_GATE_NORM)
    tl = lax.broadcasted_iota(jnp.int32, (nseq * tm, A_QK), 0) & (SB - 1)
    sh = 1
    while sh < SB:
        la = la + jnp.where(tl >= sh, pltpu.roll(la, sh, axis=0), 0.0)
        sh *= 2
    b_sc[...] = la.reshape(nseq, tm, A_QK)

    for si in range(nseq):
        xs = xp.at[si]

        @pl.when(j == 0)
        def _(xs=xs, si=si):
            st_sc[si] = s0_ref[si].T
            if tmajor:
                xs[0:CONV_HDR - (B_CONV - 1), :] = jnp.zeros((CONV_HDR - (B_CONV - 1), B_WIDTH), F32)
                xs[CONV_HDR - (B_CONV - 1):CONV_HDR, :] = hdr_ref[:, si, :]
            else:
                xs[0:CONV_HDR, :] = hdr_ref[si]
            xs[CONV_HDR + tm:CONV_HDR + tm + 8, :] = jnp.zeros((8, B_WIDTH), F32)

        u = ga_ref[si] * _sigmoid(gg_ref[si])
        xs[CONV_HDR:CONV_HDR + tm, :] = u
        if tmajor:
            ut_ref[:, si, :] = u[tm - tail:tm, :]
        else:
            ut_ref[si] = u[tm - tail:tm, :]

    hm = _head_mask(A_HEADS * SB, A_QK, SB, A_DK)
    tri = lax.broadcasted_iota(jnp.int32, (8, A_QK), 0)
    ntile = SB // 8

    def gla_step(si, rows):
        q = q_ref[si, rows, :] * (A_DK ** -0.5)
        k = k_ref[si, rows, :]
        v = v_ref[si, rows, :]
        b = b_sc[si, rows, :]
        bl = b[SB - 1:SB, :]
        qt = q * jnp.exp(b)
        kh = k * jnp.exp(bl - b)
        st = st_sc[si]
        qbd = jnp.where(hm, jnp.concatenate([qt] * A_HEADS, axis=0), 0.0).astype(BF16)
        o_int = lax.dot_general(qbd, st.astype(BF16), NT_DIMS, preferred_element_type=F32)
        kbd = jnp.where(hm, jnp.concatenate([kh] * A_HEADS, axis=0), 0.0).astype(BF16)
        vst = jnp.concatenate([v[:, h * A_DV:(h + 1) * A_DV] for h in range(A_HEADS)],
                              axis=0).astype(BF16)
        upd = lax.dot_general(vst, kbd, TN_DIMS, preferred_element_type=F32)
        st_sc[si] = st * jnp.exp(bl) + upd
        ps = []
        for s in range(SB):
            lo = 8 * (s // 8)
            bs = b[s:s + 1, :]
            ks = k[s:s + 1, :]
            dd = b[lo:lo + 8, :] - bs
            if s % 8:
                dd = jnp.where(tri >= s % 8, dd, NEG)
            ps.append(q[lo:lo + 8, :] * jnp.exp(dd) * ks)
            if lo + 8 < SB:
                ps.append(q[lo + 8:SB, :] * jnp.exp(b[lo + 8:SB, :] - bs) * ks)
        pall = jnp.concatenate(ps, axis=0).astype(BF16)
        r = jnp.dot(pall, segb_ref[...], preferred_element_type=F32)
        od = [None] * ntile
        off = 0
        for s in range(SB):
            for tq in range(s // 8, ntile):
                term = r[off:off + 8, :] * v[s:s + 1, :]
                od[tq] = term if od[tq] is None else od[tq] + term
                off += 8
        o = (jnp.concatenate([o_int[h * SB:(h + 1) * SB] for h in range(A_HEADS)], axis=1)
             + jnp.concatenate(od, axis=0))
        o_ref[si, rows, :] = o

    def conv_rows(si, r0, static):
        xs = xp.at[si]
        n = piece + 8
        acc = None
        for r in range(7, -1, -1):
            z = None
            for q in range(5):
                kk = 8 * q + r - (CONV_HDR - (B_CONV - 1))
                if 0 <= kk < B_CONV:
                    start = 8 * q + r0 if static else pl.multiple_of(8 * q + r0, 8)
                    term = cw_ref[kk:kk + 1, :] * xs[pl.ds(start, n), :]
                    z = term if z is None else z + term
            acc = z if acc is None else z + pltpu.roll(acc, n - 1, axis=0)
        y = _layernorm(acc[0:piece, :] + cb_ref[...], lng_ref[...], lnb_ref[...])
        c_ref[si, pl.ds(r0, piece), :] = y * _sigmoid(y)

    npiece = tm // piece
    if npiece == 1:
        for si in range(nseq):
            for s in range(piece // SB):
                gla_step(si, slice(s * SB, (s + 1) * SB))
            conv_rows(si, 0, True)
    else:
        def do_piece(p, carry):
            r0 = pl.multiple_of(p * piece, piece)
            for si in range(nseq):
                for s in range(piece // SB):
                    gla_step(si, pl.ds(pl.multiple_of(r0 + s * SB, SB), SB))
                conv_rows(si, r0, False)
            return carry

        lax.fori_loop(0, npiece, do_piece, 0)

    for si in range(nseq):
        xs = xp.at[si]
        nxt = xs[tm:tm + CONV_HDR, :]
        xs[0:CONV_HDR, :] = nxt

        @pl.when(j == last_j)
        def _(si=si):
            sn_ref[si] = st_sc[si].T


def _mixer0(z3, wg2p, bgate, segb, s0, hdr, cw, cb, lng, lnb, tm, nseq, tmajor):
    nb, L, _ = z3.shape
    assert nseq == 1 or L == tm
    SB = GLA_STEP if tm % GLA_STEP == 0 else tm
    piece = min(tm, MIX0_PIECE)
    tail = min(CONV_HDR, tm)
    fix = lambda b, j: (0, 0)

    def zcol(block):
        return lambda b, j: (b, j, block)

    if tmajor:
        hdr_spec = pl.BlockSpec((B_CONV - 1, nseq, B_WIDTH), lambda b, j: (0, b, 0))
        ut_spec = pl.BlockSpec((tail, nseq, B_WIDTH), lambda b, j: (0, b, 0))
        ut_shape = jax.ShapeDtypeStruct((tail, nb, B_WIDTH), F32)
    else:
        hdr_spec = pl.BlockSpec((nseq, CONV_HDR, B_WIDTH), lambda b, j: (b, 0, 0))
        ut_spec = pl.BlockSpec((nseq, tail, B_WIDTH), lambda b, j: (b, 0, 0))
        ut_shape = jax.ShapeDtypeStruct((nb, tail, B_WIDTH), F32)
    body = functools.partial(_mixer0_body, tm=tm, SB=SB, nseq=nseq, tail=tail, piece=piece,
                             tmajor=tmajor)
    return pl.pallas_call(
        body,
        grid=(nb // nseq, L // tm),
        in_specs=[pl.BlockSpec((nseq, tm, A_QK), zcol(0)),
                  pl.BlockSpec((nseq, tm, A_QK), zcol(1)),
                  pl.BlockSpec((nseq, tm, A_WIDTH), zcol(1)),
                  pl.BlockSpec((nseq, tm, A_QK), zcol(10)),
                  pl.BlockSpec((nseq, tm, B_WIDTH), zcol(3)),
                  pl.BlockSpec((nseq, tm, B_WIDTH), zcol(4)),
                  pl.BlockSpec((A_QK, A_QK), fix),
                  pl.BlockSpec((1, A_QK), fix),
                  pl.BlockSpec((A_QK, A_WIDTH), fix),
                  pl.BlockSpec((nseq, A_QK, A_DV), lambda b, j: (b, 0, 0)),
                  hdr_spec,
                  pl.BlockSpec((CONV_HDR, B_WIDTH), fix),
                  pl.BlockSpec((1, B_WIDTH), fix),
                  pl.BlockSpec((1, B_WIDTH), fix),
                  pl.BlockSpec((1, B_WIDTH), fix)],
        out_specs=[pl.BlockSpec((nseq, tm, A_WIDTH), lambda b, j: (b, j, 0)),
                   pl.BlockSpec((nseq, A_QK, A_DV), lambda b, j: (b, 0, 0)),
                   pl.BlockSpec((nseq, tm, B_WIDTH), lambda b, j: (b, j, 0)),
                   ut_spec],
        out_shape=[jax.ShapeDtypeStruct((nb, L, A_WIDTH), F32),
                   jax.ShapeDtypeStruct((nb, A_QK, A_DV), F32),
                   jax.ShapeDtypeStruct((nb, L, B_WIDTH), F32),
                   ut_shape],
        scratch_shapes=[pltpu.VMEM((nseq, tm, A_QK), F32),
                        pltpu.VMEM((nseq, A_DV, A_QK), F32),
                        pltpu.VMEM((nseq, CONV_HDR + tm + 8, B_WIDTH), F32)],
        compiler_params=_params(2),
        name="mixer0",
    )(z3, z3, z3, z3, z3, z3, wg2p, bgate, segb, s0, hdr, cw, cb, lng, lnb)


def _even_mix(o_ref, r_ref, c_ref, x_ref, gn_ref, w_ref, post_ref):
    o = o_ref[...]
    gn = gn_ref[...]
    oa = jnp.concatenate([_rms(o[:, h * A_DV:(h + 1) * A_DV], gn) for h in range(A_HEADS)], axis=1)
    r = r_ref[...]
    oa = oa * (r * _sigmoid(r))
    y = (jnp.dot(oa.astype(BF16), w_ref[0:A_WIDTH, :], preferred_element_type=F32)
         + jnp.dot(c_ref[...].astype(BF16), w_ref[A_WIDTH:A_WIDTH + B_WIDTH, :],
                   preferred_element_type=F32))
    return x_ref[...] + _rms(y, post_ref[...])


def _gelu_twice(x):
    c = math.sqrt(2.0 / math.pi)
    return x + x * jnp.tanh(x * (c + (c * 0.044715) * (x * x)))


N_MIX_REFS = {None: 1, "even": 7, "odd": 10}


def _ffn_body(*refs, tm, seq8, mix):
    head, refs = refs[:N_MIX_REFS[mix]], refs[N_MIX_REFS[mix]:]
    if seq8:
        (pg_ref, wup_ref, dw_ref, b_ref, wdn_ref, post_ref, st_ref,
         xo_ref, tail_ref, act_sc, p1_sc, p2_sc, u_sc) = refs
    else:
        (pg_ref, wup_ref, dw_ref, b_ref, wdn_ref, post_ref,
         xo_ref, tail_ref, act_sc, car) = refs
    i = pl.program_id(1)
    ns = tm // 8

    @pl.when(i == 0)
    def _():
        if seq8:
            p1_sc[...] = jnp.zeros(p1_sc.shape, F32)
            p2_sc[...] = jnp.zeros(p2_sc.shape, F32)
        else:
            car[...] = jnp.zeros((8, 2 * D_FF), F32)

    if mix == "even":
        x = _even_mix(*head)
    elif mix == "odd":
        x = _odd_mix(*head)
    else:
        x = head[0][...]
    h = _rms(x, pg_ref[...]).astype(BF16)
    t = lax.broadcasted_iota(jnp.int32, (tm, FFN_CHUNK), 0) & 7

    def half(cols):
        u = jnp.dot(h, wup_ref[:, cols], preferred_element_type=F32)
        if seq8:
            for hf in range(FFN_CHUNK // 128):
                jt = cols.start // 128 + hf
                p1_sc[hf, pl.ds(0, ns, stride=8), :] = st_ref[:, 2 * jt + 1, :]
                p2_sc[hf, pl.ds(0, ns, stride=8), :] = st_ref[:, 2 * jt, :]
                p2_sc[hf, pl.ds(1, ns, stride=8), :] = st_ref[:, 2 * jt + 1, :]
                u_sc[hf] = u[:, 128 * hf:128 * (hf + 1)]
                tail_ref[:, 2 * jt, :] = u_sc[hf, pl.ds(6, ns, stride=8), :]
                tail_ref[:, 2 * jt + 1, :] = u_sc[hf, pl.ds(7, ns, stride=8), :]
            p1 = jnp.concatenate([p1_sc[hf] for hf in range(FFN_CHUNK // 128)], axis=1)
            p2 = jnp.concatenate([p2_sc[hf] for hf in range(FFN_CHUNK // 128)], axis=1)
            u1 = jnp.where(t >= 1, pltpu.roll(u, 1, axis=0), p1)
            u2 = jnp.where(t >= 2, pltpu.roll(u, 2, axis=0), p2)
        else:
            xp = jnp.concatenate([car[:, cols], u], axis=0)
            u1 = pltpu.roll(xp, 1, axis=0)[8:, :]
            u2 = pltpu.roll(xp, 2, axis=0)[8:, :]
            last = u[tm - 8:tm, :]
            car[:, cols] = last
            tail_ref[0, :, cols] = last
        dw = dw_ref[:, cols]
        return dw[0:1, :] * u2 + dw[1:2, :] * u1 + dw[2:3, :] * u + b_ref[:, cols]

    for c in range(FFN_NCHUNK):
        g = half(slice(c * FFN_CHUNK, (c + 1) * FFN_CHUNK))
        val = half(slice(D_FF + c * FFN_CHUNK, D_FF + (c + 1) * FFN_CHUNK))
        act_sc[:, c * FFN_CHUNK:(c + 1) * FFN_CHUNK] = (_gelu_twice(g) * val).astype(BF16)
    f = jnp.dot(act_sc[...], wdn_ref[...], preferred_element_type=F32)
    xo_ref[...] = x + _rms(f, post_ref[...])


def _ffn(x_in, pg, wup, dw8, dwb, wdn_half, post, nb, tm, st2=None, mix=None):
    seq8 = st2 is not None
    xmap = lambda b, i: (b * nt + i, 0)
    fix = lambda b, i: (0, 0)
    fix3 = lambda b, i: (0, 0, 0)
    once = pl.Buffered(1)

    def col(block):
        return lambda b, i: (b * nt + i, block)

    if mix == "even":
        o2, z2, c2, x2, gn, w_mix, post_mix = x_in
        in_specs = [pl.BlockSpec((tm, A_WIDTH), xmap),
                    pl.BlockSpec((tm, A_WIDTH), col(2)),
                    pl.BlockSpec((tm, B_WIDTH), xmap),
                    pl.BlockSpec((tm, D_MODEL), xmap),
                    pl.BlockSpec((1, A_DV), fix),
                    pl.BlockSpec((A_WIDTH + B_WIDTH, D_MODEL), fix, pipeline_mode=once),
                    pl.BlockSpec((1, D_MODEL), fix)]
        args = [o2, z2, c2, x2, gn, w_mix, post_mix]
    elif mix == "odd":
        oc2, z2, x2, lng, lnb, wsm, sbe, w_mix, post_mix = x_in
        in_specs = [pl.BlockSpec((tm, C_WIDTH), xmap),
                    pl.BlockSpec((tm, D_WIDTH), col(9)),
                    pl.BlockSpec((tm, D_WIDTH), col(10)),
                    pl.BlockSpec((1, D_WIDTH), fix),
                    pl.BlockSpec((1, D_WIDTH), fix),
                    pl.BlockSpec((4, D_CHUNK, D_CHUNK), fix3),
                    pl.BlockSpec((D_CHUNK, D_WIDTH), fix),
                    pl.BlockSpec((tm, D_MODEL), xmap),
                    pl.BlockSpec((C_WIDTH + D_WIDTH, D_MODEL), fix, pipeline_mode=once),
                    pl.BlockSpec((1, D_MODEL), fix)]
        args = [oc2, z2, z2, lng, lnb, wsm, sbe, x2, w_mix, post_mix]
    else:
        x2 = x_in
        in_specs = [pl.BlockSpec((tm, D_MODEL), xmap)]
        args = [x2]
    assert len(args) == N_MIX_REFS[mix]
    rows = x2.shape[0]
    nt = rows // (nb * tm)
    in_specs += [pl.BlockSpec((1, D_MODEL), fix),
                 pl.BlockSpec((D_MODEL, 2 * D_FF), fix, pipeline_mode=once),
                 pl.BlockSpec((8, 2 * D_FF), fix),
                 pl.BlockSpec((1, 2 * D_FF), fix),
                 pl.BlockSpec((D_FF, D_MODEL), fix, pipeline_mode=once),
                 pl.BlockSpec((1, D_MODEL), fix)]
    args += [pg, wup, dw8, dwb, wdn_half, post]
    scratch = [pltpu.VMEM((tm, D_FF), BF16)]
    if seq8:
        ns = tm // 8
        st_block = (ns, 4 * D_FF // 128, 128)
        in_specs.append(pl.BlockSpec(st_block, lambda b, i: (b * nt + i, 0, 0)))
        args.append(st2)
        tail_spec = pl.BlockSpec(st_block, lambda b, i: (b * nt + i, 0, 0))
        tail_shape = jax.ShapeDtypeStruct((rows // 8,) + st_block[1:], F32)
        scratch += [pltpu.VMEM((FFN_CHUNK // 128, tm, 128), F32)] * 3
    else:
        tail_spec = pl.BlockSpec((1, 8, 2 * D_FF), lambda b, i: (b * nt + i, 0, 0))
        tail_shape = jax.ShapeDtypeStruct((nb * nt, 8, 2 * D_FF), F32)
        scratch.append(pltpu.VMEM((8, 2 * D_FF), F32))
    body = functools.partial(_ffn_body, tm=tm, seq8=seq8, mix=mix)
    return pl.pallas_call(
        body,
        grid=(nb, nt),
        in_specs=in_specs,
        out_specs=[pl.BlockSpec((tm, D_MODEL), xmap), tail_spec],
        out_shape=[jax.ShapeDtypeStruct((rows, D_MODEL), F32), tail_shape],
        scratch_shapes=scratch,
        compiler_params=_params(2),
        name="ffn" + ("_seq8" if seq8 else "") + ("_" + mix if mix else ""),
    )(*args)


def _attn_p_body(q0, k0, v0, q1, k1, v1, q2, k2, v2, bias_ref, o_ref, kv0, kv1, kv2,
                 og, lg, *, L):
    lane = lax.broadcasted_iota(jnp.int32, (C_QBLOCK, 2 * C_DH), 1)
    first = lane < C_DH
    col = lax.broadcasted_iota(jnp.int32, (C_QBLOCK, 2 * C_QBLOCK), 1)
    groups = ((q0, k0, v0), (q1, k1, v1), (q2, k2, v2))
    for g, (qr, kr, vr) in enumerate(groups):
        kvr = (kv0, kv1, kv2)[g]
        wl = kvr.shape[-1]
        piece = min(wl, 512)
        for c0 in range(0, wl, piece):
            src = slice(L - wl + c0, L - wl + c0 + piece)
            kvr[0, 0, :, c0:c0 + piece] = kr[0, src, :].T
            kvr[0, 1, :, c0:c0 + piece] = vr[0, src, :].T
        d = C_PATTERNS[g][1]
        nblk = L // d // C_QBLOCK
        for res in range(d):
            for m in range(nblk):
                def rows_of(mm, res=res, d=d):
                    if d == 1:
                        return slice(mm * C_QBLOCK, (mm + 1) * C_QBLOCK)
                    return pl.ds(res + d * C_QBLOCK * mm, C_QBLOCK, stride=d)

                rq = rows_of(m)
                qb = qr[0, rq, :] * (C_DH ** -0.5)
                rp = rows_of(max(m - 1, 0))
                kcat = jnp.concatenate([kr[0, rp, :], kr[0, rq, :]], axis=0).astype(BF16)
                vcat = jnp.concatenate([vr[0, rp, :], vr[0, rq, :]], axis=0).astype(BF16)
                outs, lses = [], []
                for hh in range(2):
                    keep = first if hh == 0 else jnp.logical_not(first)
                    qm = jnp.where(keep, qb, 0.0).astype(BF16)
                    s = lax.dot_general(qm, kcat, NT_DIMS, preferred_element_type=F32)
                    s = s + bias_ref[g, hh]
                    if m == 0:
                        s = jnp.where(col < C_QBLOCK, NEG, s)
                    mx = jnp.max(s, axis=-1, keepdims=True)
                    p = jnp.exp(s - mx)
                    l = jnp.sum(p, axis=-1, keepdims=True)
                    r = jnp.dot(p.astype(BF16), vcat, preferred_element_type=F32)
                    outs.append(r / l)
                    lses.append(mx + jnp.log(l))
                og[g, rq, :] = jnp.where(first, outs[0], outs[1])
                lg[g, rq, :] = jnp.where(first, lses[0], lses[1])
    cr = 256
    for cidx in range(L // cr):
        rows = slice(cidx * cr, (cidx + 1) * cr)
        l0, l1, l2 = lg[0, rows, :], lg[1, rows, :], lg[2, rows, :]
        mx = jnp.maximum(jnp.maximum(l0, l1), l2)
        w0, w1, w2 = jnp.exp(l0 - mx), jnp.exp(l1 - mx), jnp.exp(l2 - mx)
        num = w0 * og[0, rows, :] + w1 * og[1, rows, :] + w2 * og[2, rows, :]
        o_ref[0, rows, :] = num / (w0 + w1 + w2)


def _attn_prompt(z3, biasmat):
    nb, L, _ = z3.shape
    lw = 2 * C_DH
    in_specs = []
    for g in range(3):
        for part in range(3):
            base = part * 6 + g * 2
            in_specs.append(pl.BlockSpec((1, L, lw), lambda b, p, base=base: (b, 0, base + p)))
    in_specs.append(pl.BlockSpec((3, 2, C_QBLOCK, 2 * C_QBLOCK), lambda b, p: (0, p, 0, 0)))
    out_specs = [pl.BlockSpec((1, L, lw), lambda b, p: (b, 0, p))]
    out_shape = [jax.ShapeDtypeStruct((nb, L, C_WIDTH), F32)]
    for window, _ in C_PATTERNS:
        wl = min(window, L)
        out_specs.append(pl.BlockSpec((1, 2, lw, wl), lambda b, p: (b, 0, p, 0)))
        out_shape.append(jax.ShapeDtypeStruct((nb, 2, C_WIDTH, wl), F32))
    body = functools.partial(_attn_p_body, L=L)
    return pl.pallas_call(
        body,
        grid=(nb, 2),
        in_specs=in_specs,
        out_specs=out_specs,
        out_shape=out_shape,
        scratch_shapes=[pltpu.VMEM((3, L, lw), F32)] * 2,
        compiler_params=_params(2),
        name="attn_prompt",
    )(*([z3] * 9), biasmat)


def _attn_s_body(z_ref, c0, c1, c2, ma0, ma1, ma2, mb_ref, o_ref, n0, n1, n2):
    nq = 8
    z = z_ref[0]
    hm = _head_mask(C_HEADS * nq, C_WIDTH, nq, C_DH)
    lane = lax.broadcasted_iota(jnp.int32, (2 * C_WIDTH, C_QBLOCK), 1)
    outs, lses = [], []
    for g, (cref, mref, nref) in enumerate(((c0, ma0, n0), (c1, ma1, n1), (c2, ma2, n2))):
        W = C_PATTERNS[g][0]
        q = z[:, g * C_WIDTH:(g + 1) * C_WIDTH] * (C_DH ** -0.5)
        kn = z[:, (3 + g) * C_WIDTH:(4 + g) * C_WIDTH]
        vn = z[:, (6 + g) * C_WIDTH:(7 + g) * C_WIDTH]
        new = jnp.concatenate([jnp.zeros((C_QBLOCK - nq, 2 * C_WIDTH), F32),
                               jnp.concatenate([kn, vn], axis=1)], axis=0)
        new_t = new.T
        old = cref[0]
        sh = pltpu.roll(old, W - nq, axis=1)
        if W > C_QBLOCK:
            nref[0, :, 0:W - C_QBLOCK] = sh[:, 0:W - C_QBLOCK]
        nref[0, :, W - C_QBLOCK:W] = jnp.where(lane >= C_QBLOCK - nq, new_t,
                                               sh[:, W - C_QBLOCK:W])
        qbd = jnp.where(hm, jnp.concatenate([q] * C_HEADS, axis=0), 0.0).astype(BF16)
        ka = old[0:C_WIDTH, :].astype(BF16)
        va = old[C_WIDTH:2 * C_WIDTH, :].astype(BF16)
        kb = new_t[0:C_WIDTH, :].astype(BF16)
        vb = new_t[C_WIDTH:2 * C_WIDTH, :].astype(BF16)
        sa = jnp.dot(qbd, ka, preferred_element_type=F32) + mref[...]
        sb = jnp.dot(qbd, kb, preferred_element_type=F32) + mb_ref[g]
        mx = jnp.maximum(jnp.max(sa, axis=-1, keepdims=True), jnp.max(sb, axis=-1, keepdims=True))
        pa = jnp.exp(sa - mx)
        pb = jnp.exp(sb - mx)
        l = jnp.sum(pa, axis=-1, keepdims=True) + jnp.sum(pb, axis=-1, keepdims=True)
        r = (lax.dot_general(pa.astype(BF16), va, NT_DIMS, preferred_element_type=F32)
             + lax.dot_general(pb.astype(BF16), vb, NT_DIMS, preferred_element_type=F32))
        outs.append(r / l)
        lses.append(mx + jnp.log(l))
    mx = jnp.maximum(jnp.maximum(lses[0], lses[1]), lses[2])
    ws = [jnp.exp(ls - mx) for ls in lses]
    o32 = (ws[0] * outs[0] + ws[1] * outs[1] + ws[2] * outs[2]) / (ws[0] + ws[1] + ws[2])
    o32 = jnp.where(hm, o32, 0.0)
    o_ref[0] = o32[0:nq] + o32[nq:2 * nq] + o32[2 * nq:3 * nq] + o32[3 * nq:4 * nq]


def _attn_sample(z3, caches, mas, mb):
    nb, nq, _ = z3.shape
    kvw = 2 * C_WIDTH
    in_specs = [pl.BlockSpec((1, nq, Z_WIDTH), lambda b: (b, 0, 0))]
    out_specs = [pl.BlockSpec((1, nq, C_WIDTH), lambda b: (b, 0, 0))]
    out_shape = [jax.ShapeDtypeStruct((nb, nq, C_WIDTH), F32)]
    for W, _ in C_PATTERNS:
        in_specs.append(pl.BlockSpec((1, kvw, W), lambda b: (b, 0, 0)))
        out_specs.append(pl.BlockSpec((1, kvw, W), lambda b: (b, 0, 0)))
        out_shape.append(jax.ShapeDtypeStruct((nb, kvw, W), F32))
    for W, _ in C_PATTERNS:
        in_specs.append(pl.BlockSpec((C_HEADS * nq, W), lambda b: (0, 0)))
    in_specs.append(pl.BlockSpec((3, C_HEADS * nq, C_QBLOCK), lambda b: (0, 0, 0)))
    return pl.pallas_call(
        _attn_s_body,
        grid=(nb,),
        in_specs=in_specs,
        out_specs=out_specs,
        out_shape=out_shape,
        compiler_params=_params(1),
        name="attn_sample",
    )(z3, *caches, *mas, mb)


def _odd_mix(oc_ref, du_ref, dv_ref, lng_ref, lnb_ref, ws_ref, sb_ref, x_ref, w_ref, post_ref):
    tm = x_ref.shape[0]
    u = _gelu(du_ref[...])
    vn = _layernorm(_gelu(dv_ref[...]), lng_ref[...], lnb_ref[...])
    lane_g = lax.broadcasted_iota(jnp.int32, (D_CHUNK, D_WIDTH), 1) >> int(math.log2(D_DH))
    parts = []
    for cidx in range(tm // D_CHUNK):
        vc = vn[cidx * D_CHUNK:(cidx + 1) * D_CHUNK, :]
        m = sb_ref[...]
        for g in range(D_WIDTH // D_DH):
            vm = jnp.where(lane_g == g, vc, 0.0).astype(BF16)
            m = m + jnp.dot(ws_ref[g], vm, preferred_element_type=F32)
        parts.append(m)
    od = u * jnp.concatenate(parts, axis=0)
    y = (jnp.dot(oc_ref[...].astype(BF16), w_ref[0:C_WIDTH, :], preferred_element_type=F32)
         + jnp.dot(od.astype(BF16), w_ref[C_WIDTH:C_WIDTH + D_WIDTH, :],
                   preferred_element_type=F32))
    return x_ref[...] + _rms(y, post_ref[...])


def _oddout_s_body(oc_ref, du_ref, dv_ref, lng_ref, lnb_ref, we_ref, sb_ref, x_ref, w_ref,
                   post_ref, xo_ref, vn_ref):
    nb, nq, _ = du_ref.shape
    u = _gelu(du_ref[...])
    vn = _layernorm(_gelu(dv_ref[...]), lng_ref[...], lnb_ref[...])
    vn_ref[...] = vn
    m = jnp.zeros((nb, nq, D_WIDTH), F32) + sb_ref[...]
    for s in range(nq):
        m = m + we_ref[s] * vn[:, s:s + 1, :]
    od = (u * m).reshape(nb * nq, D_WIDTH)
    y = (jnp.dot(oc_ref[...].astype(BF16), w_ref[0:C_WIDTH, :], preferred_element_type=F32)
         + jnp.dot(od.astype(BF16), w_ref[C_WIDTH:C_WIDTH + D_WIDTH, :],
                   preferred_element_type=F32))
    xo_ref[...] = x_ref[...] + _rms(y, post_ref[...])


def _oddout_sample(oc2, z3, x2, lng, lnb, wexp, sbe, w, post):
    nb, nq, _ = z3.shape
    rows = nb * nq
    return pl.pallas_call(
        _oddout_s_body,
        grid=(1,),
        in_specs=[pl.BlockSpec((rows, C_WIDTH), lambda i: (0, 0)),
                  pl.BlockSpec((nb, nq, D_WIDTH), lambda i: (0, 0, 9)),
                  pl.BlockSpec((nb, nq, D_WIDTH), lambda i: (0, 0, 10)),
                  pl.BlockSpec((1, D_WIDTH), lambda i: (0, 0)),
                  pl.BlockSpec((1, D_WIDTH), lambda i: (0, 0)),
                  pl.BlockSpec((nq, nq, D_WIDTH), lambda i: (0, 0, 0)),
                  pl.BlockSpec((nq, D_WIDTH), lambda i: (0, 0)),
                  pl.BlockSpec((rows, D_MODEL), lambda i: (0, 0)),
                  pl.BlockSpec((C_WIDTH + D_WIDTH, D_MODEL), lambda i: (0, 0)),
                  pl.BlockSpec((1, D_MODEL), lambda i: (0, 0))],
        out_specs=[pl.BlockSpec((rows, D_MODEL), lambda i: (0, 0)),
                   pl.BlockSpec((nb, nq, D_WIDTH), lambda i: (0, 0, 0))],
        out_shape=[jax.ShapeDtypeStruct((rows, D_MODEL), F32),
                   jax.ShapeDtypeStruct((nb, nq, D_WIDTH), F32)],
        compiler_params=_params(1),
        name="oddout_sample",
    )(oc2, z3, z3, lng, lnb, wexp, sbe, x2, w, post)


def _t5_bucket(dist):
    max_exact = N_BUCKETS // 2
    d32 = jnp.maximum(dist, 1).astype(F32)
    large = max_exact + (jnp.log(d32 / max_exact) / math.log(MAX_DIST / max_exact)
                         * (N_BUCKETS - max_exact)).astype(jnp.int32)
    large = jnp.minimum(large, N_BUCKETS - 1)
    return jnp.where(dist < max_exact, dist, large)


def _step_bias(rel_bias, g):
    window, dil = C_PATTERNS[g]
    j = jnp.arange(window // dil + 1, dtype=jnp.int32)
    return rel_bias[_t5_bucket(dil * j), g * C_HEADS:(g + 1) * C_HEADS].T.astype(F32)


def _skew(v, rows):
    n = v.shape[-1]
    lead = v.shape[:-1]
    t = jnp.broadcast_to(v[..., None, :], lead + (rows, n)).reshape(lead + (rows * n,))
    return t[..., :rows * (n - 1)].reshape(lead + (rows, n - 1))


def _prompt_bias(rel_bias):
    mats = []
    for g in range(len(C_PATTERNS)):
        bj = _step_bias(rel_bias, g)
        v = jnp.concatenate([bj[:, ::-1], jnp.full((C_HEADS, C_QBLOCK), NEG, F32)], axis=1)
        mats.append(_skew(v, C_QBLOCK))
    return jnp.stack(mats, axis=0)


def _sample_bias(rel_bias, nq):
    mas, mbs = [], []
    i = np.arange(nq)[:, None]
    i2 = np.arange(nq)[None, :]
    for g, (window, dil) in enumerate(C_PATTERNS):
        bj = _step_bias(rel_bias, g)
        on_grid = jnp.asarray(np.arange(window + 1) % dil == 0)
        bd = jnp.where(on_grid[None], jnp.repeat(bj, dil, axis=1)[:, :window + 1], NEG)
        e = jnp.concatenate([bd[:, ::-1], jnp.full((C_HEADS, nq - 1), NEG, F32)], axis=1)
        ma = _skew(e, nq)[:, :, :window]
        mas.append(ma.reshape(C_HEADS * nq, window))
        dist = i - i2
        ok = jnp.asarray((dist >= 0) & (dist % dil == 0))
        mb = jnp.where(ok[None], bd[:, np.maximum(dist, 0)], NEG)
        mb = jnp.pad(mb, ((0, 0), (0, 0), (C_QBLOCK - nq, 0)), constant_values=NEG)
        mbs.append(mb.reshape(C_HEADS * nq, C_QBLOCK))
    return mas, jnp.stack(mbs, axis=0)


def _row(v):
    return v.reshape(1, -1).astype(F32)


def kernel(x_prompt, x_sample, state_gla, state_conv_b, cache_c_w128, cache_c_w512, cache_c_w2048,
           state_ffn_conv, norm_pre_mix, norm_post_mix, norm_pre_ffn, norm_post_ffn, w_in_even,
           w_gate2, b_gate, gla_norm, conv_b_w, conv_b_b, ln_b_g, ln_b_b, w_out_even, w_in_odd,
           rel_bias, sgu_ln_g, sgu_ln_b, sgu_w, sgu_b, w_out_odd, w_up, ffn_dw_w, ffn_dw_b, w_down):
    nbp, lp, d = x_prompt.shape
    nbs, ls, _ = x_sample.shape

    we = w_in_even[0]
    split = 2 * A_QK + A_WIDTH
    w_e = jnp.concatenate(
        [we[:, :split], we[:, split + A_GATE_RANK:],
         we[:, split:split + A_GATE_RANK],
         jnp.zeros((d, A_QK - A_GATE_RANK), F32)], axis=1).astype(BF16)
    wg2p = jnp.zeros((A_QK, A_QK), F32).at[:A_GATE_RANK].set(w_gate2[0]).astype(BF16)
    segb = jnp.asarray(np.kron(np.eye(A_HEADS), np.ones((A_DK, A_DV))), BF16)
    cw = jnp.zeros((CONV_HDR, B_WIDTH), F32).at[:B_CONV].set(conv_b_w[0])
    w_oe = w_out_even[0].astype(BF16)
    w_o = w_in_odd[0].astype(BF16)
    w_oo = w_out_odd[0].astype(BF16)
    w_up_b = w_up.astype(BF16)
    w_dn_b = (0.5 * w_down).astype(BF16)
    dw8 = jnp.zeros((2, 8, 2 * D_FF), F32).at[:, :3].set(ffn_dw_w)
    tril = jnp.tril(jnp.ones((D_CHUNK, D_CHUNK), F32))
    sgu_wm = sgu_w[0] * tril
    bias_p = _prompt_bias(rel_bias)
    mas, mbs = _sample_bias(rel_bias, ls)

    def ffn_params(layer):
        return (_row(norm_pre_ffn[layer]), w_up_b[layer], dw8[layer], _row(ffn_dw_b[layer]),
                w_dn_b[layer], _row(norm_post_ffn[layer]))

    def even_layer(x2, nb, L, s0t, hdr, conv_tm, nseq, tmajor):
        z = _inproj(x2, _row(norm_pre_mix[0]), w_e, 1024 if nb * L > 1024 else 512)
        z3 = z.reshape(nb, L, Z_WIDTH)
        o, st, c, ut = _mixer0(z3, wg2p, _row(b_gate[0]), segb, s0t, hdr, cw, _row(conv_b_b[0]),
                               _row(ln_b_g[0]), _row(ln_b_b[0]), conv_tm, nseq, tmajor)
        mix_in = (o.reshape(nb * L, A_WIDTH), z, c.reshape(nb * L, B_WIDTH), x2,
                  _row(gla_norm[0]), w_oe, _row(norm_post_mix[0]))
        return mix_in, st.reshape(nb, A_HEADS, A_DK, A_DV), ut

    def kv_window(t):
        return t.reshape(t.shape[0], 2, C_HEADS, C_DH, t.shape[-1]).transpose(0, 4, 1, 2, 3)[None]

    xp2 = x_prompt.reshape(nbp * lp, d)
    mix0, p_gla, ut = even_layer(xp2, nbp, lp, jnp.zeros((nbp, A_QK, A_DV), F32),
                                 jnp.zeros((nbp, CONV_HDR, B_WIDTH), F32), 512, 1, False)
    p_conv_b = ut[:, CONV_HDR - (B_CONV - 1):]
    x2, tail0 = _ffn(mix0, *ffn_params(0), nbp, 512, mix="even")
    z = _inproj(x2, _row(norm_pre_mix[1]), w_o, 1024)
    z3 = z.reshape(nbp, lp, Z_WIDTH)
    oc, *p_kvt = _attn_prompt(z3, bias_p)
    wsm = sgu_wm.astype(BF16)
    sbe = jnp.repeat(sgu_b[0].T, D_DH, axis=-1)
    mix1 = (oc.reshape(nbp * lp, C_WIDTH), z, x2, _row(sgu_ln_g[0]), _row(sgu_ln_b[0]), wsm, sbe,
            w_oo, _row(norm_post_mix[1]))
    x4, tail1 = _ffn(mix1, *ffn_params(1), nbp, 512, mix="odd")
    y_prompt = x4.reshape(nbp, lp, d)
    p_kv = [kv_window(t) for t in p_kvt]
    p_ffn = jnp.stack([t.reshape(nbp, -1, 8, 2 * D_FF)[:, -1, 6:8] for t in (tail0, tail1)], axis=0)

    xs2 = x_sample.reshape(nbs * ls, d)
    hdr = state_conv_b[0].transpose(1, 0, 2)
    smix0, s_gla, us = even_layer(xs2, nbs, ls, state_gla[0].reshape(nbs, A_QK, A_DV), hdr, ls,
                                  SAMPLE_NSEQ, True)
    s_conv_b = jnp.concatenate([hdr[ls:], us], axis=0).transpose(1, 0, 2)

    def ffn_sample(x_in, layer, mix):
        ntile = 2 * D_FF // 128
        st2 = state_ffn_conv[layer].reshape(nbs, 2, ntile, 128).transpose(0, 2, 1, 3)
        xo, tail = _ffn(x_in, *ffn_params(layer), 1, 256, st2.reshape(nbs, 2 * ntile, 128), mix)
        tail = tail.reshape(nbs, ntile, 2, 128).transpose(0, 2, 1, 3)
        return xo, tail.reshape(nbs, 2, 2 * D_FF)

    y2, s_ffn0 = ffn_sample(smix0, 0, "even")
    zs = _inproj(y2, _row(norm_pre_mix[1]), w_o, 512)
    zs3 = zs.reshape(nbs, ls, Z_WIDTH)
    caches = [c[0].transpose(0, 2, 3, 4, 1).reshape(nbs, 2 * C_WIDTH, c.shape[2])
              for c in (cache_c_w128, cache_c_w512, cache_c_w2048)]
    ocs, n0, n1, n2 = _attn_sample(zs3, caches, mas, mbs)
    wexp = jnp.repeat(sgu_wm[:, :ls, :ls].transpose(2, 1, 0), D_DH, axis=-1)
    sbes = jnp.repeat(sgu_b[0][:, :ls].T, D_DH, axis=-1)
    y3, s_sgu_v = _oddout_sample(ocs.reshape(nbs * ls, C_WIDTH), zs3, y2, _row(sgu_ln_g[0]),
                                 _row(sgu_ln_b[0]), wexp, sbes, w_oo, _row(norm_post_mix[1]))
    y4, s_ffn1 = ffn_sample(y3, 1, None)
    y_sample = y4.reshape(nbs, ls, d)
    s_kv = [kv_window(n.reshape(nbs, 2, C_WIDTH, n.shape[2])) for n in (n0, n1, n2)]
    s_ffn = jnp.stack([s_ffn0, s_ffn1], axis=0)

    return (y_prompt, y_sample, p_gla[None], p_conv_b[None], p_kv[0], p_kv[1], p_kv[2], p_ffn,
            s_gla[None], s_conv_b[None], s_kv[0], s_kv[1], s_kv[2], s_sgu_v[None], s_ffn)
```
